```python
import math
import jax, jax.numpy as jnp
from jax import lax
import numpy as np

D_MODEL = 2048
BATCH = 16
SEQ = 256
DEPTH = 4
DEC_BATCH = 2
DEC_SEQ = 1024
PAST_LEN = 512

GRID_W = 64
HEAD_DIM = 128
ROPE_THETA = 10000.0
Q_BLOCK = 128
EPS = 1e-6
GQA_HEADS = 8
GQA_KV_HEADS = 2
MLA_HEADS = 8
MLA_Q_RANK = 512
MLA_KV_RANK = 256
MLA_NOPE = 128
MLA_ROPE = 64
MLA_V = 128
ATTN_SPLITS = (GQA_HEADS * HEAD_DIM, GQA_KV_HEADS * HEAD_DIM, GQA_KV_HEADS * HEAD_DIM,
               MLA_Q_RANK, MLA_KV_RANK, MLA_ROPE)
ATTN_IN = sum(ATTN_SPLITS)
ATTN_MIX = GQA_HEADS * HEAD_DIM + MLA_HEADS * MLA_V
CONV_DIM = D_MODEL // 2
CONV_WIDTH = 31
S5_DIM = D_MODEL // 2
S5_GROUP = 16
S5_GROUPS = S5_DIM // S5_GROUP
S5_STATE = 64
CS_SPLITS = (CONV_DIM, CONV_DIM, S5_DIM)
CS_IN = sum(CS_SPLITS)
CS_MIX = CONV_DIM + S5_DIM
N_EXPERTS = 16
EXPERT_FF = 1024
EC_CAPACITY = 2
N_ATTN_LAYERS = (DEPTH + 1) // 2
N_SSM_LAYERS = DEPTH // 2

kernel_name = "hybrid_dit_gqa_mla_conformer_s5_ecmoe_step"


def rms_norm(x, g):
    xf = x.astype(jnp.float32)
    y = xf * lax.rsqrt(jnp.mean(xf * xf, axis=-1, keepdims=True) + EPS)
    return (y * g.astype(jnp.float32)).astype(x.dtype)


def layer_norm(x, g, b):
    xf = x.astype(jnp.float32)
    mu = jnp.mean(xf, axis=-1, keepdims=True)
    xc = xf - mu
    y = xc * lax.rsqrt(jnp.mean(xc * xc, axis=-1, keepdims=True) + EPS)
    return (y * g.astype(jnp.float32) + b.astype(jnp.float32)).astype(x.dtype)


def split_cols(x, sizes):
    return jnp.split(x, [int(s) for s in np.cumsum(sizes)[:-1]], axis=-1)


def axial_rope(n_tokens, rot_dim):
    rows = n_tokens // GRID_W
    per_axis = rot_dim // 4
    freqs = ROPE_THETA ** (-jnp.arange(per_axis, dtype=jnp.float32) / per_axis)
    row = jnp.repeat(jnp.arange(rows, dtype=jnp.float32), GRID_W)
    col = jnp.tile(jnp.arange(GRID_W, dtype=jnp.float32), rows)
    ang = jnp.concatenate([row[:, None] * freqs, col[:, None] * freqs], axis=-1)
    return jnp.cos(ang), jnp.sin(ang)


def apply_rope(x, cos, sin):
    half = x.shape[-1] // 2
    xf = x.astype(jnp.float32)
    x1, x2 = xf[..., :half], xf[..., half:]
    c, s = cos[None, :, None, :], sin[None, :, None, :]
    return jnp.concatenate([x1 * c - x2 * s, x1 * s + x2 * c], axis=-1).astype(x.dtype)


def attend(q, k, v, scale):
    B, S, Hk, G, Dq = q.shape
    nb = S // Q_BLOCK
    qb = jnp.moveaxis(q.reshape(B, nb, Q_BLOCK, Hk, G, Dq), 1, 0)

    def block(qi):
        s = jnp.einsum('bqhgd,blhd->bhgql', qi, k).astype(jnp.float32) * scale
        p = jax.nn.softmax(s, axis=-1).astype(v.dtype)
        return jnp.einsum('bhgql,blhd->bqhgd', p, v)

    o = lax.map(block, qb)
    return jnp.moveaxis(o, 0, 1).reshape(B, S, Hk, G, v.shape[-1])


def attn_mixer(h, w_in, q_norm, k_norm, qa_norm, w_qb, kva_norm, w_kvb, w_out, rope_g, rope_m, cache):
    B, L, _ = h.shape
    q, k, v, cq, ckv, kpe = split_cols(h @ w_in, ATTN_SPLITS)
    q = rms_norm(q.reshape(B, L, GQA_HEADS, HEAD_DIM), q_norm)
    k = rms_norm(k.reshape(B, L, GQA_KV_HEADS, HEAD_DIM), k_norm)
    v = v.reshape(B, L, GQA_KV_HEADS, HEAD_DIM)
    ckv = rms_norm(ckv, kva_norm)
    kpe = kpe.reshape(B, L, 1, MLA_ROPE)
    qm = (rms_norm(cq, qa_norm) @ w_qb).reshape(B, L, MLA_HEADS, MLA_NOPE + MLA_ROPE)
    q_nope, q_pe = qm[..., :MLA_NOPE], qm[..., MLA_NOPE:]
    if cache is None:
        new_ctx = (k, v, ckv, kpe[:, :, 0])
        k_all, v_all, ckv_all, kpe_all = k, v, ckv, kpe
    else:
        new_ctx = None
        q, k = apply_rope(q, *rope_g), apply_rope(k, *rope_g)
        q_pe, kpe = apply_rope(q_pe, *rope_m), apply_rope(kpe, *rope_m)
        ck, cv, cckv, ckpe = cache
        k_all = jnp.concatenate([k, ck], axis=1)
        v_all = jnp.concatenate([v, cv], axis=1)
        ckv_all = jnp.concatenate([ckv, cckv], axis=1)
        kpe_all = jnp.concatenate([kpe, ckpe[:, :, None]], axis=1)
    Lk = k_all.shape[1]
    o_g = attend(q.reshape(B, L, GQA_KV_HEADS, GQA_HEADS // GQA_KV_HEADS, HEAD_DIM),
                 k_all, v_all, HEAD_DIM ** -0.5)
    kv = (ckv_all @ w_kvb).reshape(B, Lk, MLA_HEADS, MLA_NOPE + MLA_V)
    k_m = jnp.concatenate([kv[..., :MLA_NOPE],
                           jnp.broadcast_to(kpe_all, (B, Lk, MLA_HEADS, MLA_ROPE))], axis=-1)
    q_m = jnp.concatenate([q_nope, q_pe], axis=-1)[:, :, :, None, :]
    o_m = attend(q_m, k_m, kv[..., MLA_NOPE:], (MLA_NOPE + MLA_ROPE) ** -0.5)
    o = jnp.concatenate([o_g.reshape(B, L, -1), o_m.reshape(B, L, -1)], axis=-1)
    return o @ w_out, new_ctx


def _linear_combine(e1, e2):
    a1, b1 = e1
    a2, b2 = e2
    return a1 * a2, a2 * b1 + b2


def s5_direction(u, a_re, a_im, log_step, b_re, b_im, c_re, c_im, h0, reverse):
    f32 = jnp.float32
    lam = lax.complex(a_re.astype(f32), a_im.astype(f32))
    step = jnp.exp(log_step.astype(f32))[:, None]
    lam_bar = jnp.exp(lam * step)
    b_bar = ((lam_bar - 1.0) / lam)[..., None] * lax.complex(b_re.astype(f32), b_im.astype(f32))
    bu = jnp.einsum('gpc,blgc->blgp', b_bar, u.astype(jnp.complex64))
    if h0 is not None:
        edge = -1 if reverse else 0
        bu = bu.at[:, edge].add(lam_bar * h0)
    a = jnp.broadcast_to(lam_bar, bu.shape)
    _, states = lax.associative_scan(_linear_combine, (a, bu), axis=1, reverse=reverse)
    cmat = lax.complex(c_re.astype(f32), c_im.astype(f32))
    y = jnp.real(jnp.einsum('gcp,blgp->blgc', cmat, states))
    final = states[:, 0] if reverse else states[:, -1]
    return y, final


def conv_ssm_mixer(h, w_in, conv_w, conv_b, ln_g, ln_b, a_re, a_im, log_step, b_re, b_im,
                   c_re, c_im, d_skip, w_glu, b_glu, w_out, state):
    B, L, _ = h.shape
    xa, xg, us = split_cols(h @ w_in, CS_SPLITS)
    u = xa * jax.nn.sigmoid(xg)
    u = lax.conv_general_dilated(u, conv_w[:, None, :], window_strides=(1,),
                                 padding=[(CONV_WIDTH // 2, CONV_WIDTH // 2)],
                                 dimension_numbers=('NWC', 'WIO', 'NWC'),
                                 feature_group_count=CONV_DIM) + conv_b
    u = jax.nn.silu(layer_norm(u, ln_g, ln_b))
    uf = us.astype(jnp.float32).reshape(B, L, S5_GROUPS, S5_GROUP)
    y = d_skip.astype(jnp.float32).reshape(S5_GROUPS, S5_GROUP) * uf
    finals = []
    for d, rev in enumerate((False, True)):
        h0 = None if state is None else lax.complex(state[:, d, 0].astype(jnp.float32),
                                                    state[:, d, 1].astype(jnp.float32))
        yd, fd = s5_direction(uf, a_re[d], a_im[d], log_step[d], b_re[d], b_im[d],
                              c_re[d], c_im[d], h0, rev)
        y = y + yd
        finals.append(fd)
    y = jax.nn.gelu(y.reshape(B, L, S5_DIM)).astype(h.dtype)
    y = y * jax.nn.sigmoid(y @ w_glu + b_glu)
    out = jnp.concatenate([u, y], axis=-1) @ w_out
    if state is None:
        fin = jnp.stack(finals, axis=1)
        return out, jnp.stack([jnp.real(fin), jnp.imag(fin)], axis=2).astype(h.dtype)
    return out, None


def expert_choice_ffn(h, w_router, w_gate, w_up, w_down):
    B, n, D = h.shape
    cap = EC_CAPACITY * n // N_EXPERTS
    aff = jax.nn.softmax((h @ w_router).astype(jnp.float32), axis=-1)
    gates, idx = lax.top_k(jnp.swapaxes(aff, 1, 2), cap)
    xs = jax.vmap(lambda hb, ib: hb[ib])(h, idx)
    act = jax.nn.silu(jnp.einsum('becd,edf->becf', xs, w_gate)) * jnp.einsum('becd,edf->becf', xs, w_up)
    y = jnp.einsum('becf,efd->becd', act, w_down) * gates[..., None].astype(h.dtype)
    return jax.vmap(lambda yb, ib: jnp.zeros((n, D), h.dtype).at[ib.reshape(-1)].add(yb.reshape(-1, D)))(y, idx)


def modulation(cond, w_mod, b_mod):
    m = jax.nn.silu(cond) @ w_mod + b_mod
    return jnp.split(m[:, None, :], 6, axis=-1)


def setup_inputs(seed: int = 0) -> dict:
    key = jax.random.key(seed)
    ks = iter(jax.random.split(key, 64))
    f32 = jnp.float32

    def nrm(shape, scale=1.0):
        return jax.random.normal(next(ks), shape, f32) * scale

    def gain(shape):
        return 1.0 + nrm(shape, 0.02)

    NA, NS = N_ATTN_LAYERS, N_SSM_LAYERS
    return {
        "x_prompt": nrm((BATCH, SEQ, D_MODEL)),
        "x_sample": nrm((DEC_BATCH, DEC_SEQ, D_MODEL)),
        "cache_gqa_k": nrm((DEC_BATCH, NA, PAST_LEN, GQA_KV_HEADS, HEAD_DIM)),
        "cache_gqa_v": nrm((DEC_BATCH, NA, PAST_LEN, GQA_KV_HEADS, HEAD_DIM)),
        "cache_mla_ckv": nrm((DEC_BATCH, NA, PAST_LEN, MLA_KV_RANK)),
        "cache_mla_kpe": nrm((DEC_BATCH, NA, PAST_LEN, MLA_ROPE)),
        "state_s5": nrm((DEC_BATCH, NS, 2, 2, S5_GROUPS, S5_STATE), 0.5),
        "c": nrm((DEC_BATCH, D_MODEL)),
        "c_ctx": nrm((D_MODEL,)),
        "w_mod": nrm((DEPTH, D_MODEL, 6 * D_MODEL), 0.5 * D_MODEL ** -0.5),
        "b_mod": nrm((DEPTH, 6 * D_MODEL), 0.02),
        "norm1_g": gain((DEPTH, D_MODEL)),
        "norm2_g": gain((DEPTH, D_MODEL)),
        "attn_w_in": nrm((NA, D_MODEL, ATTN_IN), D_MODEL ** -0.5),
        "gqa_q_norm": gain((NA, HEAD_DIM)),
        "gqa_k_norm": gain((NA, HEAD_DIM)),
        "mla_qa_norm": gain((NA, MLA_Q_RANK)),
        "mla_w_qb": nrm((NA, MLA_Q_RANK, MLA_HEADS * (MLA_NOPE + MLA_ROPE)), MLA_Q_RANK ** -0.5),
        "mla_kva_norm": gain((NA, MLA_KV_RANK)),
        "mla_w_kvb": nrm((NA, MLA_KV_RANK, MLA_HEADS * (MLA_NOPE + MLA_V)), MLA_KV_RANK ** -0.5),
        "attn_w_out": nrm((NA, ATTN_MIX, D_MODEL), ATTN_MIX ** -0.5),
        "cs_w_in": nrm((NS, D_MODEL, CS_IN), D_MODEL ** -0.5),
        "conv_w": nrm((NS, CONV_WIDTH, CONV_DIM), CONV_WIDTH ** -0.5),
        "conv_b": nrm((NS, CONV_DIM), 0.02),
        "conv_ln_g": gain((NS, CONV_DIM)),
        "conv_ln_b": nrm((NS, CONV_DIM), 0.02),
        "s5_a_re": -0.5 + nrm((NS, 2, S5_GROUPS, S5_STATE), 0.01),
        "s5_a_im": jnp.pi * jnp.arange(S5_STATE, dtype=f32) + nrm((NS, 2, S5_GROUPS, S5_STATE), 0.01),
        "s5_log_step": jax.random.uniform(next(ks), (NS, 2, S5_GROUPS), f32,
                                          math.log(1e-3), math.log(1e-1)),
        "s5_b_re": nrm((NS, 2, S5_GROUPS, S5_STATE, S5_GROUP), (2 * S5_GROUP) ** -0.5),
        "s5_b_im": nrm((NS, 2, S5_GROUPS, S5_STATE, S5_GROUP), (2 * S5_GROUP) ** -0.5),
        "s5_c_re": nrm((NS, 2, S5_GROUPS, S5_GROUP, S5_STATE), (2 * S5_STATE) ** -0.5),
        "s5_c_im": nrm((NS, 2, S5_GROUPS, S5_GROUP, S5_STATE), (2 * S5_STATE) ** -0.5),
        "s5_d": nrm((NS, S5_DIM), 0.5),
        "s5_w_glu": nrm((NS, S5_DIM, S5_DIM), S5_DIM ** -0.5),
        "s5_b_glu": nrm((NS, S5_DIM), 0.02),
        "cs_w_out": nrm((NS, CS_MIX, D_MODEL), CS_MIX ** -0.5),
        "moe_router": nrm((DEPTH, D_MODEL, N_EXPERTS), D_MODEL ** -0.5),
        "moe_w_gate": nrm((DEPTH, N_EXPERTS, D_MODEL, EXPERT_FF), D_MODEL ** -0.5),
        "moe_w_up": nrm((DEPTH, N_EXPERTS, D_MODEL, EXPERT_FF), D_MODEL ** -0.5),
        "moe_w_down": nrm((DEPTH, N_EXPERTS, EXPERT_FF, D_MODEL), EXPERT_FF ** -0.5),
        "final_norm_g": gain((D_MODEL,)),
    }


def reference(x_prompt, x_sample, cache_gqa_k, cache_gqa_v, cache_mla_ckv, cache_mla_kpe, state_s5,
              c, c_ctx, w_mod, b_mod, norm1_g, norm2_g, attn_w_in, gqa_q_norm, gqa_k_norm,
              mla_qa_norm, mla_w_qb, mla_kva_norm, mla_w_kvb, attn_w_out, cs_w_in, conv_w, conv_b,
              conv_ln_g, conv_ln_b, s5_a_re, s5_a_im, s5_log_step, s5_b_re, s5_b_im, s5_c_re,
              s5_c_im, s5_d, s5_w_glu, s5_b_glu, cs_w_out, moe_router, moe_w_gate, moe_w_up,
              moe_w_down, final_norm_g):

    def run(x, cond, rope_g, rope_m, attn_cache, ssm_state):
        attn_ctx, ssm_ctx = [], []
        for l in range(DEPTH):
            sh1, sc1, g1, sh2, sc2, g2 = modulation(cond, w_mod[l], b_mod[l])
            hmix = rms_norm(x, norm1_g[l]) * (1 + sc1) + sh1
            i = l // 2
            if l % 2 == 0:
                cache = None if attn_cache is None else tuple(a[:, i] for a in attn_cache)
                out, ctx = attn_mixer(hmix, attn_w_in[i], gqa_q_norm[i], gqa_k_norm[i],
                                      mla_qa_norm[i], mla_w_qb[i], mla_kva_norm[i], mla_w_kvb[i],
                                      attn_w_out[i], rope_g, rope_m, cache)
                attn_ctx.append(ctx)
            else:
                st = None if ssm_state is None else ssm_state[:, i]
                out, ctx = conv_ssm_mixer(hmix, cs_w_in[i], conv_w[i], conv_b[i], conv_ln_g[i],
                                          conv_ln_b[i], s5_a_re[i], s5_a_im[i], s5_log_step[i],
                                          s5_b_re[i], s5_b_im[i], s5_c_re[i], s5_c_im[i], s5_d[i],
                                          s5_w_glu[i], s5_b_glu[i], cs_w_out[i], st)
                ssm_ctx.append(ctx)
            x = x + g1 * out
            hff = rms_norm(x, norm2_g[l]) * (1 + sc2) + sh2
            x = x + g2 * expert_choice_ffn(hff, moe_router[l], moe_w_gate[l], moe_w_up[l], moe_w_down[l])
        return rms_norm(x, final_norm_g), attn_ctx, ssm_ctx

    y_prompt, attn_ctx, ssm_ctx = run(x_prompt, c_ctx[None, :], None, None, None, None)
    new_gqa_k = jnp.stack([a[0] for a in attn_ctx], axis=1)
    new_gqa_v = jnp.stack([a[1] for a in attn_ctx], axis=1)
    new_mla_ckv = jnp.stack([a[2] for a in attn_ctx], axis=1)
    new_mla_kpe = jnp.stack([a[3] for a in attn_ctx], axis=1)
    new_s5_state = jnp.stack(ssm_ctx, axis=1)

    n_lat = x_sample.shape[1]
    rope_g = axial_rope(n_lat, HEAD_DIM)
    rope_m = axial_rope(n_lat, MLA_ROPE)
    y_sample, _, _ = run(x_sample, c, rope_g, rope_m,
                         (cache_gqa_k, cache_gqa_v, cache_mla_ckv, cache_mla_kpe), state_s5)

    return (y_prompt, y_sample, new_gqa_k, new_gqa_v, new_mla_ckv, new_mla_kpe, new_s5_state)
```

```python
import functools
import math

import jax
import jax.numpy as jnp
import numpy as np
from jax import lax
from jax.experimental import pallas as pl
from jax.experimental.pallas import tpu as pltpu

F32 = jnp.float32
BF16 = jnp.bfloat16

D_MODEL = 2048
BATCH = 16
SEQ = 256
DEPTH = 4
DEC_BATCH = 2
DEC_SEQ = 1024
PAST_LEN = 512
GRID_W = 64
HEAD_DIM = 128
ROPE_THETA = 10000.0
EPS = 1e-6
GQA_HEADS = 8
GQA_KV_HEADS = 2
MLA_HEADS = 8
MLA_Q_RANK = 512
MLA_KV_RANK = 256
MLA_NOPE = 128
MLA_ROPE = 64
MLA_V = 128
CONV_DIM = D_MODEL // 2
CONV_WIDTH = 31
S5_DIM = D_MODEL // 2
S5_GROUP = 16
S5_GROUPS = S5_DIM // S5_GROUP
S5_STATE = 64
N_EXPERTS = 16
EXPERT_FF = 1024
EC_CAPACITY = 2

LANES = 128
VMEM_LIMIT = 56 * 1024 * 1024

TOK = 256
N_CTX_TILES = BATCH * SEQ // TOK
DEC_TILES_PER_REQ = DEC_SEQ // TOK
N_DEC_TILES = DEC_BATCH * DEC_TILES_PER_REQ
N_TILES = N_CTX_TILES + N_DEC_TILES
N_TOK = N_TILES * TOK
CTX_TOK = N_CTX_TILES * TOK
CAP_CTX = EC_CAPACITY * SEQ // N_EXPERTS
CAP_DEC = EC_CAPACITY * DEC_SEQ // N_EXPERTS
XS_CTX_ROWS = BATCH * CAP_CTX
XS_DEC_ROWS = DEC_BATCH * CAP_DEC

ATTN_IN_PAD = 2432
G_SCALE = HEAD_DIM ** -0.5
M_SCALE = (MLA_NOPE + MLA_ROPE) ** -0.5

S5_TT = 16
S5_NB = TOK // S5_TT
S5_Q = 4
S5_QC = S5_DIM // S5_Q
S5_QS = S5_GROUPS // S5_Q * S5_STATE
S5_STATE_COLS = S5_Q * 2 * S5_QS
S5_FIX_ROWS = 8

_NT = (((1,), (1,)), ((), ()))


def _cparams(n_grid_dims=1):
    return pltpu.CompilerParams(dimension_semantics=("arbitrary",) * n_grid_dims,
                                vmem_limit_bytes=VMEM_LIMIT)


def _const_spec(shape):
    nd = len(shape)
    return pl.BlockSpec(shape, lambda *_: (0,) * nd)


def _rms(x, g):
    return x * lax.rsqrt(jnp.mean(x * x, axis=-1, keepdims=True) + EPS) * g


def _silu(x):
    return x * jax.nn.sigmoid(x)


def _dot(a, b):
    return jnp.dot(a, b, preferred_element_type=F32)


def _dot_nt(a, b):
    return lax.dot_general(a, b, _NT, preferred_element_type=F32)


MOD_TN = 1024


def _mod_kernel(cond_ref, w_ref, b_ref, o_ref):
    a = _silu(cond_ref[...]).astype(BF16)
    o_ref[...] = _dot(a, w_ref[...].astype(BF16)) + b_ref[...]


def _modulation(cond8, w_mod, b_mod):
    n_out = 6 * D_MODEL
    return pl.pallas_call(
        _mod_kernel,
        grid=(DEPTH, n_out // MOD_TN),
        in_specs=[
            pl.BlockSpec((8, D_MODEL), lambda l, n: (0, 0)),
            pl.BlockSpec((None, D_MODEL, MOD_TN), lambda l, n: (l, 0, n)),
            pl.BlockSpec((None, 1, MOD_TN), lambda l, n: (l, 0, n)),
        ],
        out_specs=pl.BlockSpec((None, 8, MOD_TN), lambda l, n: (l, 0, n)),
        out_shape=jax.ShapeDtypeStruct((DEPTH, 8, n_out), F32),
        compiler_params=_cparams(2),
        name="modulation",
    )(cond8, w_mod, b_mod.reshape(DEPTH, 1, n_out))


def _attn_pre_kernel(x_ref, mod_ref, g1_ref, win_ref, qn_ref, kn_ref, qan_ref, kvan_ref, wqb_ref,
                     wkvb_ref, rope_ref,
                     qg_ref, kg_ref, vg_ref, kf_ref, vf_ref, ckvf_ref, kpef_ref, qmn_ref, qmp_ref,
                     kvm_ref, kpeb_ref):
    m = mod_ref[...]
    h = _rms(x_ref[...], g1_ref[...]) * (1.0 + m[1:2]) + m[0:1]
    proj = _dot(h.astype(BF16), win_ref[...])
    rope = rope_ref[...]
    ga, gb, ma, mb, md = [rope[:, i * LANES:(i + 1) * LANES] for i in range(5)]

    def rope_g(xh):
        return xh * ga + pltpu.roll(xh, 64, 1) * gb

    def rope_m(xh):
        return xh * ma + pltpu.roll(xh, 96, 1) * mb + pltpu.roll(xh, 32, 1) * md

    for hh in range(GQA_HEADS):
        sl = slice(hh * HEAD_DIM, (hh + 1) * HEAD_DIM)
        qg_ref[:, sl] = rope_g(_rms(proj[:, sl], qn_ref[...])).astype(BF16)
    k0 = GQA_HEADS * HEAD_DIM
    for hh in range(GQA_KV_HEADS):
        sl = slice(hh * HEAD_DIM, (hh + 1) * HEAD_DIM)
        kh = rope_g(_rms(proj[:, k0 + hh * HEAD_DIM:k0 + (hh + 1) * HEAD_DIM], kn_ref[...]))
        kf_ref[:, sl] = kh
        kg_ref[:, sl] = kh.astype(BF16)
    v0 = k0 + GQA_KV_HEADS * HEAD_DIM
    v = proj[:, v0:v0 + GQA_KV_HEADS * HEAD_DIM]
    vf_ref[...] = v
    vg_ref[...] = v.astype(BF16)
    c0 = v0 + GQA_KV_HEADS * HEAD_DIM
    cq = _rms(proj[:, c0:c0 + MLA_Q_RANK], qan_ref[...])
    qm = _dot(cq.astype(BF16), wqb_ref[...])
    n_nope = MLA_HEADS * MLA_NOPE
    qmn_ref[...] = qm[:, :n_nope].astype(BF16)
    for hh in range(MLA_HEADS):
        sl = slice(hh * LANES, (hh + 1) * LANES)
        qmp_ref[:, sl] = rope_m(qm[:, n_nope + hh * LANES:n_nope + (hh + 1) * LANES]).astype(BF16)
    kv0 = c0 + MLA_Q_RANK
    ckv = _rms(proj[:, kv0:kv0 + MLA_KV_RANK], kvan_ref[...])
    ckvf_ref[...] = ckv
    kvm_ref[...] = _dot(ckv.astype(BF16), wkvb_ref[...]).astype(BF16)
    kpe = rope_m(proj[:, kv0 + MLA_KV_RANK:kv0 + MLA_KV_RANK + LANES])
    kpef_ref[...] = kpe
    kpeb_ref[...] = kpe.astype(BF16)


def _rope_tile(j):
    return jnp.where(j < N_CTX_TILES, 0, 1 + (j - N_CTX_TILES) % DEC_TILES_PER_REQ)


def _attn_pre(x, modt, g1, win, qn, kn, qan, kvan, wqb, wkvb, rope_tab):
    tile = lambda w: pl.BlockSpec((TOK, w), lambda j: (j, 0))
    kvw = GQA_KV_HEADS * HEAD_DIM
    outs = [
        (GQA_HEADS * HEAD_DIM, BF16), (kvw, BF16), (kvw, BF16),
        (kvw, F32), (kvw, F32), (MLA_KV_RANK, F32), (LANES, F32),
        (MLA_HEADS * MLA_NOPE, BF16), (MLA_HEADS * LANES, BF16),
        (MLA_HEADS * (MLA_NOPE + MLA_V), BF16), (LANES, BF16),
    ]
    return pl.pallas_call(
        _attn_pre_kernel,
        grid=(N_TILES,),
        in_specs=[
            tile(D_MODEL),
            pl.BlockSpec((None, 6, D_MODEL), lambda j: (j, 0, 0)),
            _const_spec((1, D_MODEL)),
            _const_spec(win.shape),
            _const_spec((1, HEAD_DIM)), _const_spec((1, HEAD_DIM)),
            _const_spec((1, MLA_Q_RANK)), _const_spec((1, MLA_KV_RANK)),
            _const_spec(wqb.shape), _const_spec(wkvb.shape),
            pl.BlockSpec((TOK, 5 * LANES), lambda j: (_rope_tile(j), 0)),
        ],
        out_specs=[tile(w) for w, _ in outs],
        out_shape=[jax.ShapeDtypeStruct((N_TOK, w), dt) for w, dt in outs],
        compiler_params=_cparams(),
        name="attn_pre",
    )(x, modt, g1, win, qn, kn, qan, kvan, wqb, wkvb, rope_tab)


def _rows_matmul_kernel(a_ref, w_ref, o_ref):
    o_ref[...] = _dot(a_ref[...].astype(BF16), w_ref[...]).astype(o_ref.dtype)


def _rows_matmul(a, w, out_dtype, name):
    rows, k = a.shape
    n = w.shape[1]
    return pl.pallas_call(
        _rows_matmul_kernel,
        grid=(rows // TOK,),
        in_specs=[pl.BlockSpec((TOK, k), lambda i: (i, 0)), _const_spec(w.shape)],
        out_specs=pl.BlockSpec((TOK, n), lambda i: (i, 0)),
        out_shape=jax.ShapeDtypeStruct((rows, n), out_dtype),
        compiler_params=_cparams(),
        name=name,
    )(a, w)


def _attend(scores, values):
    m = scores[0].max(axis=-1, keepdims=True)
    for s in scores[1:]:
        m = jnp.maximum(m, s.max(axis=-1, keepdims=True))
    ps = [jnp.exp(s - m) for s in scores]
    l = ps[0].sum(axis=-1, keepdims=True)
    for p in ps[1:]:
        l = l + p.sum(axis=-1, keepdims=True)
    o = _dot(ps[0].astype(BF16), values[0])
    for p, v in zip(ps[1:], values[1:]):
        o = o + _dot(p.astype(BF16), v)
    return o / l


def _attn_heads(q_ref, qn_ref, qp_ref, segs, o_ref):
    for hh in range(GQA_HEADS):
        sl = slice(hh * HEAD_DIM, (hh + 1) * HEAD_DIM)
        kh = hh // (GQA_HEADS // GQA_KV_HEADS)
        ksl = slice(kh * HEAD_DIM, (kh + 1) * HEAD_DIM)
        q = q_ref[:, sl]
        scores = [_dot_nt(q, s[0][:, ksl]) * G_SCALE for s in segs]
        o_ref[:, sl] = _attend(scores, [s[1][:, ksl] for s in segs]).astype(BF16)
    o0 = GQA_HEADS * HEAD_DIM
    v0 = MLA_HEADS * MLA_NOPE
    for hh in range(MLA_HEADS):
        sl = slice(hh * LANES, (hh + 1) * LANES)
        qn = qn_ref[:, sl]
        qp = qp_ref[:, sl]
        scores = [(_dot_nt(qn, s[2][:, sl]) + _dot_nt(qp, s[3][...])) * M_SCALE for s in segs]
        vals = [s[2][:, v0 + hh * MLA_V:v0 + (hh + 1) * MLA_V] for s in segs]
        o_ref[:, o0 + hh * MLA_V:o0 + (hh + 1) * MLA_V] = _attend(scores, vals).astype(BF16)


def _attn_kernel(q_ref, qn_ref, qp_ref,
                 ck_ref, cv_ref, ckvm_ref, ckpe_ref,
                 dk_ref, dv_ref, dkvm_ref, dkpe_ref,
                 pk_ref, pv_ref, pkvm_ref, pkpe_ref,
                 o_ref):
    j = pl.program_id(0)

    @pl.when(j < N_CTX_TILES)
    def _():
        _attn_heads(q_ref, qn_ref, qp_ref, [(ck_ref, cv_ref, ckvm_ref, ckpe_ref)], o_ref)

    @pl.when(j >= N_CTX_TILES)
    def _():
        _attn_heads(q_ref, qn_ref, qp_ref,
                    [(dk_ref, dv_ref, dkvm_ref, dkpe_ref), (pk_ref, pv_ref, pkvm_ref, pkpe_ref)], o_ref)


def _dec_req(j):
    return jnp.maximum(j - N_CTX_TILES, 0) // DEC_TILES_PER_REQ


def _attn(qg, kg, vg, qmn, qmp, kvm, kpeb, pk, pv, pkvm, pkpe):
    tile = lambda w: pl.BlockSpec((TOK, w), lambda j: (j, 0))
    ctx = lambda w: pl.BlockSpec((TOK, w), lambda j: (jnp.minimum(j, N_CTX_TILES - 1), 0))
    dec = lambda w: pl.BlockSpec((DEC_SEQ, w), lambda j: (CTX_TOK // DEC_SEQ + _dec_req(j), 0))
    past = lambda w: pl.BlockSpec((PAST_LEN, w), lambda j: (_dec_req(j), 0))
    kvw = GQA_KV_HEADS * HEAD_DIM
    mw = MLA_HEADS * (MLA_NOPE + MLA_V)
    return pl.pallas_call(
        _attn_kernel,
        grid=(N_TILES,),
        in_specs=[tile(GQA_HEADS * HEAD_DIM), tile(MLA_HEADS * MLA_NOPE), tile(MLA_HEADS * LANES),
                  ctx(kvw), ctx(kvw), ctx(mw), ctx(LANES),
                  dec(kvw), dec(kvw), dec(mw), dec(LANES),
                  past(kvw), past(kvw), past(mw), past(LANES)],
        out_specs=tile(D_MODEL),
        out_shape=jax.ShapeDtypeStruct((N_TOK, D_MODEL), BF16),
        compiler_params=_cparams(),
        name="attention",
    )(qg, qmn, qmp, kg, vg, kvm, kpeb, kg, vg, kvm, kpeb, pk, pv, pkvm, pkpe)


def _residual_and_router(x, mix_out, m, g2_ref, wr_ref, x1_ref, hff_ref, aff_ref):
    x1 = x + m[2:3] * mix_out
    x1_ref[...] = x1
    hf = _rms(x1, g2_ref[...]) * (1.0 + m[4:5]) + m[3:4]
    hff_ref[...] = hf.astype(BF16)
    logits = jnp.dot(hf, wr_ref[...], preferred_element_type=F32, precision=lax.Precision.HIGHEST)
    lane = lax.broadcasted_iota(jnp.int32, logits.shape, 1)
    logits = jnp.where(lane < N_EXPERTS, logits, -jnp.inf)
    e = jnp.exp(logits - logits.max(axis=-1, keepdims=True))
    aff_ref[...] = e / e.sum(axis=-1, keepdims=True)


def _epilogue_specs():
    tile = lambda w: pl.BlockSpec((TOK, w), lambda j: (j, 0))
    out_specs = [tile(D_MODEL), tile(D_MODEL), tile(LANES)]
    out_shape = [jax.ShapeDtypeStruct((N_TOK, D_MODEL), F32),
                 jax.ShapeDtypeStruct((N_TOK, D_MODEL), BF16),
                 jax.ShapeDtypeStruct((N_TOK, LANES), F32)]
    return out_specs, out_shape


def _attn_out_kernel(o_ref, wout_ref, x_ref, mod_ref, g2_ref, wr_ref, x1_ref, hff_ref, aff_ref):
    out = _dot(o_ref[...], wout_ref[...])
    _residual_and_router(x_ref[...], out, mod_ref[...], g2_ref, wr_ref, x1_ref, hff_ref, aff_ref)


def _attn_out(o, wout, x, modt, g2, wr):
    tile = lambda w: pl.BlockSpec((TOK, w), lambda j: (j, 0))
    out_specs, out_shape = _epilogue_specs()
    return pl.pallas_call(
        _attn_out_kernel,
        grid=(N_TILES,),
        in_specs=[tile(D_MODEL), _const_spec(wout.shape), tile(D_MODEL),
                  pl.BlockSpec((None, 6, D_MODEL), lambda j: (j, 0, 0)),
                  _const_spec((1, D_MODEL)), _const_spec(wr.shape)],
        out_specs=out_specs, out_shape=out_shape,
        compiler_params=_cparams(),
        name="attn_out",
    )(o, wout, x, modt, g2, wr)


def _cs_pre_kernel(x_ref, mod_ref, g1_ref, win_ref, ug_ref, ut_ref):
    m = mod_ref[...]
    h = _rms(x_ref[...], g1_ref[...]) * (1.0 + m[1:2]) + m[0:1]
    proj = _dot(h.astype(BF16), win_ref[...])
    ug_ref[...] = proj[:, :CONV_DIM] * jax.nn.sigmoid(proj[:, CONV_DIM:2 * CONV_DIM])
    ut_ref[...] = proj[:, 2 * CONV_DIM:]


def _cs_pre(x, modt, g1, win):
    return pl.pallas_call(
        _cs_pre_kernel,
        grid=(N_TILES,),
        in_specs=[pl.BlockSpec((TOK, D_MODEL), lambda j: (j, 0)),
                  pl.BlockSpec((None, 6, D_MODEL), lambda j: (j, 0, 0)),
                  _const_spec((1, D_MODEL)), _const_spec(win.shape)],
        out_specs=[pl.BlockSpec((TOK, CONV_DIM), lambda j: (j, 0)),
                   pl.BlockSpec((TOK, S5_DIM), lambda j: (0, j))],
        out_shape=[jax.ShapeDtypeStruct((N_TOK, CONV_DIM), F32),
                   jax.ShapeDtypeStruct((TOK, N_TILES * S5_DIM), F32)],
        compiler_params=_cparams(),
        name="cs_pre",
    )(x, modt, g1, win)


CONV_HALO = 16


def _conv_kernel(prev_ref, cur_ref, next_ref, w_ref, b_ref, lng_ref, lnb_ref, o_ref, pad_ref):
    j = pl.program_id(0)
    q = (j - N_CTX_TILES) % DEC_TILES_PER_REQ
    has_prev = jnp.logical_and(j >= N_CTX_TILES, q != 0)
    has_next = jnp.logical_and(j >= N_CTX_TILES, q != DEC_TILES_PER_REQ - 1)
    pad_ref[0:CONV_HALO, :] = jnp.where(has_prev, prev_ref[TOK - CONV_HALO:, :], 0.0)
    pad_ref[CONV_HALO:CONV_HALO + TOK, :] = cur_ref[...]
    pad_ref[CONV_HALO + TOK:, :] = jnp.where(has_next, next_ref[0:CONV_HALO, :], 0.0)
    base = CONV_HALO - CONV_WIDTH // 2
    acc = w_ref[0:1, :] * pad_ref[base:base + TOK, :]
    for tap in range(1, CONV_WIDTH):
        acc = acc + w_ref[tap:tap + 1, :] * pad_ref[base + tap:base + tap + TOK, :]
    u = acc + b_ref[...]
    mu = jnp.mean(u, axis=-1, keepdims=True)
    uc = u - mu
    y = uc * lax.rsqrt(jnp.mean(uc * uc, axis=-1, keepdims=True) + EPS) * lng_ref[...] + lnb_ref[...]
    o_ref[...] = _silu(y).astype(BF16)


def _conv(ug, w, b, lng, lnb):
    nb = lambda d: pl.BlockSpec((TOK, CONV_DIM), lambda j: (jnp.clip(j + d, 0, N_TILES - 1), 0))
    return pl.pallas_call(
        _conv_kernel,
        grid=(N_TILES,),
        in_specs=[nb(-1), nb(0), nb(1), _const_spec(w.shape),
                  _const_spec((1, CONV_DIM)), _const_spec((1, CONV_DIM)), _const_spec((1, CONV_DIM))],
        out_specs=pl.BlockSpec((TOK, CONV_DIM), lambda j: (j, 0)),
        out_shape=jax.ShapeDtypeStruct((N_TOK, CONV_DIM), BF16),
        scratch_shapes=[pltpu.VMEM((TOK + 2 * CONV_HALO, CONV_DIM), F32)],
        compiler_params=_cparams(),
        name="conv_branch",
    )(ug, ug, ug, w, b, lng, lnb)


def _s5_scan_quarter(lam_ref, st_ref, bu_ref, d, q, rows, reverse):
    half = S5_QS // 2
    for cb in range(2):
        c_re = slice(cb * half, (cb + 1) * half)
        c_im = slice(S5_QS + cb * half, S5_QS + (cb + 1) * half)
        lr = lam_ref[d, q, 0:1, c_re]
        li = lam_ref[d, q, 1:2, c_re]
        s0 = q * 2 * S5_QS
        sr = st_ref[d, :, s0 + cb * half:s0 + (cb + 1) * half]
        si = st_ref[d, :, s0 + S5_QS + cb * half:s0 + S5_QS + (cb + 1) * half]

        def body(k, carry):
            sr, si = carry
            t = (S5_TT - 1 - k) if reverse else k
            r0 = pl.multiple_of(t * rows, 8)
            nr = lr * sr - li * si + bu_ref[pl.ds(r0, rows), c_re]
            ni = lr * si + li * sr + bu_ref[pl.ds(r0, rows), c_im]
            bu_ref[pl.ds(r0, rows), c_re] = nr
            bu_ref[pl.ds(r0, rows), c_im] = ni
            return nr, ni

        sr, si = lax.fori_loop(0, S5_TT, body, (sr, si))
        st_ref[d, :, s0 + cb * half:s0 + (cb + 1) * half] = sr
        st_ref[d, :, s0 + S5_QS + cb * half:s0 + S5_QS + (cb + 1) * half] = si


def _s5_kernel(rows, has_input, *refs):
    if has_input:
        uf_ref, ur_ref, wb_ref, dsk_ref = refs[:4]
        refs = refs[4:]
    wc_ref, lam_ref, s0_ref, yf_ref, yr_ref, fin_ref, st_ref, bu_ref = refs
    i = pl.program_id(0)

    @pl.when(i == 0)
    def _():
        st_ref[...] = s0_ref[...]

    for d, (y_ref, reverse) in enumerate(((yf_ref, False), (yr_ref, True))):
        if has_input:
            u = (ur_ref if reverse else uf_ref)[...]
        for q in range(S5_Q):
            csl = slice(q * S5_QC, (q + 1) * S5_QC)
            if has_input:
                bu_ref[...] = _dot(u[:, csl].astype(BF16), wb_ref[d, q])
            else:
                bu_ref[...] = jnp.zeros_like(bu_ref)
            _s5_scan_quarter(lam_ref, st_ref, bu_ref, d, q, rows, reverse)
            y = _dot(bu_ref[...].astype(BF16), wc_ref[d, q])
            if has_input and not reverse:
                y = y + dsk_ref[:, csl] * u[:, csl]
            y_ref[:, csl] = y

    @pl.when(i == S5_NB - 1)
    def _():
        fin_ref[...] = st_ref[...]


def _s5(rows, ut, wb, wc, lam, s0, dskip):
    blk = S5_TT * rows
    fwd = pl.BlockSpec((blk, S5_DIM), lambda i: (i, 0))
    rev = pl.BlockSpec((blk, S5_DIM), lambda i: (S5_NB - 1 - i, 0))
    has_input = ut is not None
    in_specs, args = [], []
    if has_input:
        in_specs += [fwd, rev, _const_spec(wb.shape), _const_spec((1, S5_DIM))]
        args += [ut, ut, wb, dskip]
    in_specs += [_const_spec(wc.shape), _const_spec(lam.shape), _const_spec(s0.shape)]
    args += [wc, lam, s0]
    return pl.pallas_call(
        functools.partial(_s5_kernel, rows, has_input),
        grid=(S5_NB,),
        in_specs=in_specs,
        out_specs=[fwd, rev, _const_spec(s0.shape)],
        out_shape=[jax.ShapeDtypeStruct((TOK * rows, S5_DIM), F32),
                   jax.ShapeDtypeStruct((TOK * rows, S5_DIM), F32),
                   jax.ShapeDtypeStruct(s0.shape, F32)],
        scratch_shapes=[pltpu.VMEM(s0.shape, F32), pltpu.VMEM((blk, 2 * S5_QS), F32)],
        compiler_params=_cparams(),
        name="s5_scan" if has_input else "s5_carry_fix",
    )(*args)


def _gelu_tanh(x):
    return x * (0.5 * (1.0 + jnp.tanh(math.sqrt(2.0 / math.pi) * (x + 0.044715 * (x * x * x)))))


def _cs_post_kernel(uc_ref, yf_ref, yr_ref, cf_ref, cr_ref, wglu_ref, bglu_ref, wout_ref, x_ref, mod_ref,
                    g2_ref, wr_ref, x1_ref, hff_ref, aff_ref):
    y = _gelu_tanh(yf_ref[...] + yr_ref[...] + cf_ref[...] + cr_ref[...])
    y = y * jax.nn.sigmoid(_dot(y.astype(BF16), wglu_ref[...]) + bglu_ref[...])
    out = _dot(uc_ref[...], wout_ref[0:CONV_DIM, :]) + _dot(y.astype(BF16), wout_ref[CONV_DIM:, :])
    _residual_and_router(x_ref[...], out, mod_ref[...], g2_ref, wr_ref, x1_ref, hff_ref, aff_ref)


def _fix_slot(j, reverse):
    d = j - N_CTX_TILES
    r, q = d // DEC_TILES_PER_REQ, d % DEC_TILES_PER_REQ
    if reverse:
        ok, slot = q != DEC_TILES_PER_REQ - 1, r * (DEC_TILES_PER_REQ - 1) + q
    else:
        ok, slot = q != 0, r * (DEC_TILES_PER_REQ - 1) + q - 1
    return jnp.where(jnp.logical_and(j >= N_CTX_TILES, ok), slot, S5_FIX_ROWS - 1)


def _cs_post(uc, yf, yr, cf, cr, wglu, bglu, wout, x, modt, g2, wr):
    tile = lambda w: pl.BlockSpec((TOK, w), lambda j: (j, 0))
    tmaj = pl.BlockSpec((TOK, S5_DIM), lambda j: (0, j))
    out_specs, out_shape = _epilogue_specs()
    return pl.pallas_call(
        _cs_post_kernel,
        grid=(N_TILES,),
        in_specs=[tile(CONV_DIM), tmaj, tmaj,
                  pl.BlockSpec((TOK, S5_DIM), lambda j: (0, _fix_slot(j, False))),
                  pl.BlockSpec((TOK, S5_DIM), lambda j: (0, _fix_slot(j, True))),
                  _const_spec(wglu.shape), _const_spec((1, S5_DIM)), _const_spec(wout.shape),
                  tile(D_MODEL), pl.BlockSpec((None, 6, D_MODEL), lambda j: (j, 0, 0)),
                  _const_spec((1, D_MODEL)), _const_spec(wr.shape)],
        out_specs=out_specs, out_shape=out_shape,
        compiler_params=_cparams(),
        name="cs_post",
    )(uc, yf, yr, cf, cr, wglu, bglu, wout, x, modt, g2, wr)


RANK_CHUNK = 128
GATHER_ROWS = 512


def _moe_select(cap, aff_ref, hff_ref, xs_ref, gs_ref, g_ref, rank_ref, p_ref):
    n = aff_ref.shape[0]
    a_t = aff_ref[...].T
    tok_lane = lax.broadcasted_iota(jnp.int32, (RANK_CHUNK, n), 1)
    tok_sub = lax.broadcasted_iota(jnp.int32, (RANK_CHUNK, n), 0)
    slot = lax.broadcasted_iota(jnp.int32, (cap, n), 0).astype(F32)
    rank_ref[...] = jnp.zeros_like(rank_ref)
    for e in range(N_EXPERTS):
        row = a_t[e:e + 1, :]

        def chunk(c, rank):
            r0 = pl.multiple_of(c * RANK_CHUNK, RANK_CHUNK)
            col = aff_ref[pl.ds(r0, RANK_CHUNK), e:e + 1]
            beats = (col > row) | ((col == row) & (tok_sub + r0 < tok_lane))
            return rank + jnp.sum(jnp.where(beats, 1.0, 0.0), axis=0, keepdims=True)

        rank = lax.fori_loop(0, n // RANK_CHUNK, chunk, jnp.zeros((1, n), F32))
        rank_ref[e:e + 1, :] = rank
        onehot = slot == rank
        p_ref[e * cap:(e + 1) * cap, :] = jnp.where(onehot, 1.0, 0.0).astype(BF16)
        gs_ref[e] = jnp.sum(jnp.where(onehot, row, 0.0), axis=1, keepdims=True)
    grp = GATHER_ROWS // cap
    for e0 in range(0, N_EXPERTS, grp):
        xs = _dot(p_ref[e0 * cap:(e0 + grp) * cap, :], hff_ref[...])
        for k in range(grp):
            xs_ref[e0 + k] = xs[k * cap:(k + 1) * cap].astype(BF16)
    rank_t = rank_ref[...].T
    per = LANES // cap
    lane = lax.broadcasted_iota(jnp.int32, (n, LANES), 1)
    slot_lane = (lane & (cap - 1)).astype(F32)
    for blk in range(N_EXPERTS // per):
        rc = rank_t[:, blk * per:blk * per + 1]
        for k in range(1, per):
            rc = jnp.where(lane >= k * cap, rank_t[:, blk * per + k:blk * per + k + 1], rc)
        g_ref[:, blk * LANES:(blk + 1) * LANES] = jnp.where(slot_lane == rc, 1.0, 0.0).astype(BF16)


def _moe_gather_kernel(affc_ref, hffc_ref, affd_ref, hffd_ref,
                       xsc_ref, gsc_ref, gc_ref, xsd_ref, gsd_ref, gd_ref,
                       rankc_ref, pc_ref, rankd_ref, pd_ref):
    s = pl.program_id(0)

    @pl.when(s < BATCH)
    def _():
        _moe_select(CAP_CTX, affc_ref, hffc_ref, xsc_ref, gsc_ref, gc_ref, rankc_ref, pc_ref)

    @pl.when(s >= BATCH)
    def _():
        _moe_select(CAP_DEC, affd_ref, hffd_ref, xsd_ref, gsd_ref, gd_ref, rankd_ref, pd_ref)


def _moe_gather(aff, hff):
    cidx = lambda s: jnp.minimum(s, BATCH - 1)
    didx = lambda s: jnp.maximum(s - BATCH, 0)
    dec0 = CTX_TOK // DEC_SEQ
    return pl.pallas_call(
        _moe_gather_kernel,
        grid=(BATCH + DEC_BATCH,),
        in_specs=[pl.BlockSpec((SEQ, LANES), lambda s: (cidx(s), 0)),
                  pl.BlockSpec((SEQ, D_MODEL), lambda s: (cidx(s), 0)),
                  pl.BlockSpec((DEC_SEQ, LANES), lambda s: (dec0 + didx(s), 0)),
                  pl.BlockSpec((DEC_SEQ, D_MODEL), lambda s: (dec0 + didx(s), 0))],
        out_specs=[pl.BlockSpec((N_EXPERTS, CAP_CTX, D_MODEL), lambda s: (0, cidx(s), 0)),
                   pl.BlockSpec((N_EXPERTS, CAP_CTX, 1), lambda s: (0, cidx(s), 0)),
                   pl.BlockSpec((SEQ, N_EXPERTS * CAP_CTX), lambda s: (cidx(s), 0)),
                   pl.BlockSpec((N_EXPERTS, CAP_DEC, D_MODEL), lambda s: (0, didx(s), 0)),
                   pl.BlockSpec((N_EXPERTS, CAP_DEC, 1), lambda s: (0, didx(s), 0)),
                   pl.BlockSpec((DEC_SEQ, N_EXPERTS * CAP_DEC), lambda s: (didx(s), 0))],
        out_shape=[jax.ShapeDtypeStruct((N_EXPERTS, XS_CTX_ROWS, D_MODEL), BF16),
                   jax.ShapeDtypeStruct((N_EXPERTS, XS_CTX_ROWS, 1), F32),
                   jax.ShapeDtypeStruct((CTX_TOK, N_EXPERTS * CAP_CTX), BF16),
                   jax.ShapeDtypeStruct((N_EXPERTS, XS_DEC_ROWS, D_MODEL), BF16),
                   jax.ShapeDtypeStruct((N_EXPERTS, XS_DEC_ROWS, 1), F32),
                   jax.ShapeDtypeStruct((DEC_BATCH * DEC_SEQ, N_EXPERTS * CAP_DEC), BF16)],
        scratch_shapes=[pltpu.VMEM((LANES, SEQ), F32), pltpu.VMEM((N_EXPERTS * CAP_CTX, SEQ), BF16),
                        pltpu.VMEM((LANES, DEC_SEQ), F32), pltpu.VMEM((N_EXPERTS * CAP_DEC, DEC_SEQ), BF16)],
        compiler_params=_cparams(),
        name="moe_gather",
    )(aff, hff, aff, hff)


FF_CHUNK = 256
N_FF_CHUNKS = EXPERT_FF // FF_CHUNK


def _moe_ffn_kernel(xsc_ref, xsd_ref, gsc_ref, gsd_ref, wg_ref, wu_ref, wd_ref, yc_ref, yd_ref, acc_ref):
    f = pl.program_id(1)
    wg = wg_ref[...].astype(BF16)
    wu = wu_ref[...].astype(BF16)
    wd = wd_ref[...].astype(BF16)
    for xs_ref, r0, rows in ((xsc_ref, 0, XS_CTX_ROWS), (xsd_ref, XS_CTX_ROWS, XS_DEC_ROWS)):
        xs = xs_ref[...]
        act = _silu(_dot(xs, wg)) * _dot(xs, wu)
        part = _dot(act.astype(BF16), wd)

        @pl.when(f == 0)
        def _():
            acc_ref[r0:r0 + rows, :] = part

        @pl.when(f != 0)
        def _():
            acc_ref[r0:r0 + rows, :] += part

    @pl.when(f == N_FF_CHUNKS - 1)
    def _():
        yc_ref[...] = (acc_ref[0:XS_CTX_ROWS, :] * gsc_ref[...]).astype(BF16)
        yd_ref[...] = (acc_ref[XS_CTX_ROWS:, :] * gsd_ref[...]).astype(BF16)


def _moe_ffn(layer, xsc, xsd, gsc, gsd, w_gate, w_up, w_down):
    per_e = lambda rows, w: pl.BlockSpec((None, rows, w), lambda e, f: (e, 0, 0))
    return pl.pallas_call(
        _moe_ffn_kernel,
        grid=(N_EXPERTS, N_FF_CHUNKS),
        in_specs=[per_e(XS_CTX_ROWS, D_MODEL), per_e(XS_DEC_ROWS, D_MODEL),
                  per_e(XS_CTX_ROWS, 1), per_e(XS_DEC_ROWS, 1),
                  pl.BlockSpec((None, None, D_MODEL, FF_CHUNK), lambda e, f: (layer, e, 0, f)),
                  pl.BlockSpec((None, None, D_MODEL, FF_CHUNK), lambda e, f: (layer, e, 0, f)),
                  pl.BlockSpec((None, None, FF_CHUNK, D_MODEL), lambda e, f: (layer, e, f, 0))],
        out_specs=[per_e(XS_CTX_ROWS, D_MODEL), per_e(XS_DEC_ROWS, D_MODEL)],
        out_shape=[jax.ShapeDtypeStruct((N_EXPERTS, XS_CTX_ROWS, D_MODEL), BF16),
                   jax.ShapeDtypeStruct((N_EXPERTS, XS_DEC_ROWS, D_MODEL), BF16)],
        scratch_shapes=[pltpu.VMEM((XS_CTX_ROWS + XS_DEC_ROWS, D_MODEL), F32)],
        compiler_params=_cparams(2),
        name="moe_ffn",
    )(xsc, xsd, gsc, gsd, w_gate, w_up, w_down)


def _moe_combine_kernel(final, gc_ref, yc_ref, gd_ref, yd_ref, x1_ref, mod_ref, fg_ref, o_ref):
    j = pl.program_id(0)

    def finish(comb):
        x2 = x1_ref[...] + mod_ref[5:6, :] * comb
        o_ref[...] = _rms(x2, fg_ref[...]) if final else x2

    @pl.when(j < N_CTX_TILES)
    def _():
        ys = jnp.concatenate([yc_ref[e] for e in range(N_EXPERTS)], axis=0)
        finish(_dot(gc_ref[...], ys))

    @pl.when(j >= N_CTX_TILES)
    def _():
        ys = jnp.concatenate([yd_ref[e] for e in range(N_EXPERTS)], axis=0)
        finish(_dot(gd_ref[...], ys))


def _moe_combine(final, gc, yc, gd, yd, x1, modt, fg):
    cidx = lambda j: jnp.minimum(j, N_CTX_TILES - 1)
    didx = lambda j: jnp.maximum(j - N_CTX_TILES, 0)
    return pl.pallas_call(
        functools.partial(_moe_combine_kernel, final),
        grid=(N_TILES,),
        in_specs=[pl.BlockSpec((TOK, N_EXPERTS * CAP_CTX), lambda j: (cidx(j), 0)),
                  pl.BlockSpec((N_EXPERTS, CAP_CTX, D_MODEL), lambda j: (0, cidx(j), 0)),
                  pl.BlockSpec((TOK, N_EXPERTS * CAP_DEC), lambda j: (didx(j), 0)),
                  pl.BlockSpec((N_EXPERTS, CAP_DEC, D_MODEL), lambda j: (0, didx(j) // DEC_TILES_PER_REQ, 0)),
                  pl.BlockSpec((TOK, D_MODEL), lambda j: (j, 0)),
                  pl.BlockSpec((None, 6, D_MODEL), lambda j: (j, 0, 0)),
                  _const_spec((1, D_MODEL))],
        out_specs=pl.BlockSpec((TOK, D_MODEL), lambda j: (j, 0)),
        out_shape=jax.ShapeDtypeStruct((N_TOK, D_MODEL), F32),
        compiler_params=_cparams(),
        name="moe_combine",
    )(gc, yc, gd, yd, x1, modt, fg)


def _moe(layer, final, x1, hff, aff, modt, w_gate, w_up, w_down, fg):
    xsc, gsc, gc, xsd, gsd, gd = _moe_gather(aff, hff)
    yc, yd = _moe_ffn(layer, xsc, xsd, gsc, gsd, w_gate, w_up, w_down)
    return _moe_combine(final, gc, yc, gd, yd, x1, modt, fg)


def _axial_rope(n_tokens, rot_dim):
    rows = n_tokens // GRID_W
    per_axis = rot_dim // 4
    freqs = ROPE_THETA ** (-jnp.arange(per_axis, dtype=F32) / per_axis)
    row = jnp.repeat(jnp.arange(rows, dtype=F32), GRID_W)
    col = jnp.tile(jnp.arange(GRID_W, dtype=F32), rows)
    ang = jnp.concatenate([row[:, None] * freqs, col[:, None] * freqs], axis=-1)
    return jnp.cos(ang), jnp.sin(ang)


def _rope_table():
    cg, sg = _axial_rope(DEC_SEQ, HEAD_DIM)
    cm, sm = _axial_rope(DEC_SEQ, MLA_ROPE)
    z = jnp.zeros_like(cm)
    pos = jnp.concatenate([
        jnp.concatenate([cg, cg], -1), jnp.concatenate([-sg, sg], -1),
        jnp.concatenate([cm, cm, z, z], -1), jnp.concatenate([-sm, z, z, z], -1),
        jnp.concatenate([z, sm, z, z], -1)], axis=-1)
    one, zero = jnp.ones((TOK, LANES), F32), jnp.zeros((TOK, LANES), F32)
    ident = jnp.concatenate([one, zero, one, zero, zero], axis=-1)
    return jnp.concatenate([ident, pos], axis=0)


def _attn_weights(w_in, w_qb, w_kvb, w_out):
    win = jnp.pad(w_in.astype(BF16), ((0, 0), (0, ATTN_IN_PAD - w_in.shape[1])))
    qb = w_qb.astype(BF16).reshape(MLA_Q_RANK, MLA_HEADS, MLA_NOPE + MLA_ROPE)
    qb_pe = jnp.pad(qb[:, :, MLA_NOPE:], ((0, 0), (0, 0), (0, LANES - MLA_ROPE)))
    wqb = jnp.concatenate([qb[:, :, :MLA_NOPE].reshape(MLA_Q_RANK, -1), qb_pe.reshape(MLA_Q_RANK, -1)], axis=1)
    kvb = w_kvb.astype(BF16).reshape(MLA_KV_RANK, MLA_HEADS, MLA_NOPE + MLA_V)
    wkvb = jnp.concatenate([kvb[:, :, :MLA_NOPE].reshape(MLA_KV_RANK, -1),
                            kvb[:, :, MLA_NOPE:].reshape(MLA_KV_RANK, -1)], axis=1)
    return win, wqb, wkvb, w_out.astype(BF16)


def _s5_weights(a_re, a_im, log_step, b_re, b_im, c_re, c_im):
    lam = lax.complex(a_re.astype(F32), a_im.astype(F32))
    step = jnp.exp(log_step.astype(F32))[..., None]
    lam_bar = jnp.exp(lam * step)
    b_bar = ((lam_bar - 1.0) / lam)[..., None] * lax.complex(b_re.astype(F32), b_im.astype(F32))
    gq = S5_GROUPS // S5_Q
    eye = jnp.eye(gq, dtype=F32)

    def blockdiag_in(m):
        m = m.reshape(2, S5_Q, gq, S5_STATE, S5_GROUP)
        return jnp.einsum('ab,dqapc->dqacbp', eye, m).reshape(2, S5_Q, S5_QC, S5_QS)

    def blockdiag_out(m):
        m = m.reshape(2, S5_Q, gq, S5_GROUP, S5_STATE)
        return jnp.einsum('ab,dqacp->dqbpac', eye, m).reshape(2, S5_Q, S5_QS, S5_QC)

    wb = jnp.concatenate([blockdiag_in(jnp.real(b_bar)), blockdiag_in(jnp.imag(b_bar))], axis=-1).astype(BF16)
    wc = jnp.concatenate([blockdiag_out(c_re.astype(F32)), blockdiag_out(-c_im.astype(F32))], axis=-2).astype(BF16)
    lam_q = jnp.stack([jnp.real(lam_bar), jnp.imag(lam_bar)], axis=1)
    lam_q = lam_q.reshape(2, 2, S5_Q, S5_QS).transpose(0, 2, 1, 3)
    lam_chunk = jnp.exp(lam * step * float(TOK)).reshape(2, S5_GROUPS * S5_STATE)
    return wb, wc, lam_q, lam_chunk


def _cols_to_complex(rows):
    r = rows.reshape(rows.shape[:-1] + (S5_Q, 2, S5_QS))
    return lax.complex(r[..., 0, :], r[..., 1, :]).reshape(rows.shape[:-1] + (S5_Q * S5_QS,))


def _complex_to_cols(z):
    r = jnp.stack([jnp.real(z), jnp.imag(z)], axis=-2)
    r = r.reshape(z.shape[:-1] + (2, S5_Q, S5_QS))
    return jnp.swapaxes(r, -3, -2).reshape(z.shape[:-1] + (S5_STATE_COLS,))


def _s5_branch(ut, state_i, a_re, a_im, log_step, b_re, b_im, c_re, c_im, dskip):
    wb, wc, lam_q, lam_chunk = _s5_weights(a_re, a_im, log_step, b_re, b_im, c_re, c_im)
    h0 = _complex_to_cols(lax.complex(state_i[:, :, 0], state_i[:, :, 1]).reshape(DEC_BATCH, 2, -1))
    s0 = jnp.zeros((2, N_TILES, S5_STATE_COLS), F32)
    first = N_CTX_TILES + DEC_TILES_PER_REQ * jnp.arange(DEC_BATCH)
    s0 = s0.at[0, first].set(h0[:, 0]).at[1, first + DEC_TILES_PER_REQ - 1].set(h0[:, 1])
    yf, yr, fin = _s5(N_TILES, ut.reshape(TOK * N_TILES, S5_DIM), wb, wc, lam_q, s0, dskip)
    fz = _cols_to_complex(fin[:, N_CTX_TILES:]).reshape(2, DEC_BATCH, DEC_TILES_PER_REQ, -1)
    lc = lam_chunk[:, None, :]
    f1 = fz[0, :, 0]
    f2 = fz[0, :, 1] + lc[0] * f1
    f3 = fz[0, :, 2] + lc[0] * f2
    r2 = fz[1, :, 3]
    r1 = fz[1, :, 2] + lc[1] * r2
    r0 = fz[1, :, 1] + lc[1] * r1
    pad = jnp.zeros((S5_FIX_ROWS - DEC_BATCH * (DEC_TILES_PER_REQ - 1), S5_STATE_COLS), F32)
    sin_f = jnp.concatenate([_complex_to_cols(jnp.stack([f1, f2, f3], 1)).reshape(-1, S5_STATE_COLS), pad])
    sin_r = jnp.concatenate([_complex_to_cols(jnp.stack([r0, r1, r2], 1)).reshape(-1, S5_STATE_COLS), pad])
    cf, cr, _ = _s5(S5_FIX_ROWS, None, None, wc, lam_q, jnp.stack([sin_f, sin_r]), None)
    tm = lambda a, rows: a.reshape(TOK, rows * S5_DIM)
    new_state = fin[:, :N_CTX_TILES].reshape(2, BATCH, S5_Q, 2, S5_QS).transpose(1, 0, 3, 2, 4)
    new_state = new_state.reshape(BATCH, 2, 2, S5_GROUPS, S5_STATE)
    return tm(yf, N_TILES), tm(yr, N_TILES), tm(cf, S5_FIX_ROWS), tm(cr, S5_FIX_ROWS), new_state


def kernel(x_prompt, x_sample, cache_gqa_k, cache_gqa_v, cache_mla_ckv, cache_mla_kpe, state_s5, c, c_ctx, w_mod, b_mod, norm1_g, norm2_g, attn_w_in, gqa_q_norm, gqa_k_norm, mla_qa_norm, mla_w_qb, mla_kva_norm, mla_w_kvb, attn_w_out, cs_w_in, conv_w, conv_b, conv_ln_g, conv_ln_b, s5_a_re, s5_a_im, s5_log_step, s5_b_re, s5_b_im, s5_c_re, s5_c_im, s5_d, s5_w_glu, s5_b_glu, cs_w_out, moe_router, moe_w_gate, moe_w_up, moe_w_down, final_norm_g):
    x = jnp.concatenate([x_prompt.reshape(CTX_TOK, D_MODEL), x_sample.reshape(DEC_BATCH * DEC_SEQ, D_MODEL)])
    cond8 = jnp.concatenate([c_ctx[None, :], c, jnp.zeros((8 - 1 - DEC_BATCH, D_MODEL), F32)])
    mod = _modulation(cond8, w_mod, b_mod)
    tile_row = np.concatenate([np.zeros(N_CTX_TILES, np.int32),
                               1 + np.repeat(np.arange(DEC_BATCH, dtype=np.int32), DEC_TILES_PER_REQ)])
    modt = mod[:, tile_row].reshape(DEPTH, N_TILES, 6, D_MODEL)
    rope_tab = _rope_table()
    row = lambda v: v.reshape(1, -1)
    fg = row(final_norm_g)
    new_k, new_v, new_ckv, new_kpe, new_s5 = [], [], [], [], []
    for l in range(DEPTH):
        i = l // 2
        wr = jnp.pad(moe_router[l], ((0, 0), (0, LANES - N_EXPERTS)))
        if l % 2 == 0:
            win, wqb, wkvb, wout = _attn_weights(attn_w_in[i], mla_w_qb[i], mla_w_kvb[i], attn_w_out[i])
            (qg, kg, vg, kf, vf, ckvf, kpef, qmn, qmp, kvm, kpeb) = _attn_pre(
                x, modt[l], row(norm1_g[l]), win, row(gqa_q_norm[i]), row(gqa_k_norm[i]),
                row(mla_qa_norm[i]), row(mla_kva_norm[i]), wqb, wkvb, rope_tab)
            n_past = DEC_BATCH * PAST_LEN
            pk = cache_gqa_k[:, i].reshape(n_past, -1).astype(BF16)
            pv = cache_gqa_v[:, i].reshape(n_past, -1).astype(BF16)
            pkvm = _rows_matmul(cache_mla_ckv[:, i].reshape(n_past, MLA_KV_RANK), wkvb, BF16, "mla_cache_kv")
            pkpe = jnp.pad(cache_mla_kpe[:, i].reshape(n_past, MLA_ROPE).astype(BF16),
                           ((0, 0), (0, LANES - MLA_ROPE)))
            o = _attn(qg, kg, vg, qmn, qmp, kvm, kpeb, pk, pv, pkvm, pkpe)
            x1, hff, aff = _attn_out(o, wout, x, modt[l], row(norm2_g[l]), wr)
            new_k.append(kf[:CTX_TOK].reshape(BATCH, SEQ, GQA_KV_HEADS, HEAD_DIM))
            new_v.append(vf[:CTX_TOK].reshape(BATCH, SEQ, GQA_KV_HEADS, HEAD_DIM))
            new_ckv.append(ckvf[:CTX_TOK].reshape(BATCH, SEQ, MLA_KV_RANK))
            new_kpe.append(kpef[:CTX_TOK, :MLA_ROPE].reshape(BATCH, SEQ, MLA_ROPE))
        else:
            ug, ut = _cs_pre(x, modt[l], row(norm1_g[l]), cs_w_in[i].astype(BF16))
            uc = _conv(ug, conv_w[i], row(conv_b[i]), row(conv_ln_g[i]), row(conv_ln_b[i]))
            yf, yr, cf, cr, ns = _s5_branch(ut, state_s5[:, i], s5_a_re[i], s5_a_im[i], s5_log_step[i],
                                            s5_b_re[i], s5_b_im[i], s5_c_re[i], s5_c_im[i], row(s5_d[i]))
            new_s5.append(ns)
            x1, hff, aff = _cs_post(uc, yf, yr, cf, cr, s5_w_glu[i].astype(BF16), row(s5_b_glu[i]),
                                    cs_w_out[i].astype(BF16), x, modt[l], row(norm2_g[l]), wr)
        x = _moe(l, l == DEPTH - 1, x1, hff, aff, modt[l], moe_w_gate, moe_w_up, moe_w_down, fg)
    y_prompt = x[:CTX_TOK].reshape(BATCH, SEQ, D_MODEL)
    y_sample = x[CTX_TOK:].reshape(DEC_BATCH, DEC_SEQ, D_MODEL)
    return (y_prompt, y_sample, jnp.stack(new_k, axis=1), jnp.stack(new_v, axis=1),
            jnp.stack(new_ckv, axis=1), jnp.stack(new_kpe, axis=1), jnp.stack(new_s5, axis=1))
```

```python
import functools
import math

import jax
import jax.numpy as jnp
import numpy as np
from jax import lax
from jax.experimental import pallas as pl
from jax.experimental.pallas import tpu as pltpu

F32 = jnp.float32
BF16 = jnp.bfloat16

D_MODEL = 2048
BATCH = 16
SEQ = 256
DEPTH = 4
DEC_BATCH = 2
DEC_SEQ = 1024
PAST_LEN = 512
GRID_W = 64
HEAD_DIM = 128
ROPE_THETA = 10000.0
EPS = 1e-6
GQA_HEADS = 8
GQA_KV_HEADS = 2
MLA_HEADS = 8
MLA_Q_RANK = 512
MLA_KV_RANK = 256
MLA_NOPE = 128
MLA_ROPE = 64
MLA_V = 128
CONV_DIM = D_MODEL // 2
CONV_WIDTH = 31
S5_DIM = D_MODEL // 2
S5_GROUP = 16
S5_GROUPS = S5_DIM // S5_GROUP
S5_STATE = 64
N_EXPERTS = 16
EXPERT_FF = 1024
EC_CAPACITY = 2

LANES = 128
VMEM_LIMIT = 56 * 1024 * 1024

TOK = 256
N_CTX_TILES = BATCH * SEQ // TOK
DEC_TILES_PER_REQ = DEC_SEQ // TOK
N_DEC_TILES = DEC_BATCH * DEC_TILES_PER_REQ
N_TILES = N_CTX_TILES + N_DEC_TILES
N_TOK = N_TILES * TOK
CTX_TOK = N_CTX_TILES * TOK
CAP_CTX = EC_CAPACITY * SEQ // N_EXPERTS
CAP_DEC = EC_CAPACITY * DEC_SEQ // N_EXPERTS
XS_CTX_ROWS = BATCH * CAP_CTX
XS_DEC_ROWS = DEC_BATCH * CAP_DEC

ATTN_IN_PAD = 2432
G_SCALE = HEAD_DIM ** -0.5
M_SCALE = (MLA_NOPE + MLA_ROPE) ** -0.5

S5_TT = 16
S5_NB = TOK // S5_TT
S5_Q = 4
S5_QC = S5_DIM // S5_Q
S5_QS = S5_GROUPS // S5_Q * S5_STATE
S5_STATE_COLS = S5_Q * 2 * S5_QS
S5_FIX_ROWS = 8

_NT = (((1,), (1,)), ((), ()))


def _cparams(n_grid_dims=1):
    return pltpu.CompilerParams(dimension_semantics=("arbitrary",) * n_grid_dims,
                                vmem_limit_bytes=VMEM_LIMIT)


def _const_spec(shape):
    nd = len(shape)
    return pl.BlockSpec(shape, lambda *_: (0,) * nd)


def _rms(x, g):
    return x * lax.rsqrt(jnp.mean(x * x, axis=-1, keepdims=True) + EPS) * g


def _silu(x):
    return x * jax.nn.sigmoid(x)


def _dot(a, b):
    return jnp.dot(a, b, preferred_element_type=F32)


def _dot_nt(a, b):
    return lax.dot_general(a, b, _NT, preferred_element_type=F32)


def _ctx_tile(j):
    return jnp.minimum(j, N_CTX_TILES - 1)


def _dec_tile(j):
    return jnp.maximum(j - N_CTX_TILES, 0)


def _x_in(x):
    if isinstance(x, tuple):
        return ([pl.BlockSpec((TOK, D_MODEL), lambda j: (_ctx_tile(j), 0)),
                 pl.BlockSpec((TOK, D_MODEL), lambda j: (_dec_tile(j), 0))], list(x))
    return [pl.BlockSpec((TOK, D_MODEL), lambda j: (j, 0))], [x]


def _x_tile(x_refs):
    if len(x_refs) == 2:
        return jnp.where(pl.program_id(0) < N_CTX_TILES, x_refs[0][...], x_refs[1][...])
    return x_refs[0][...]


MOD_TN = 1024


def _mod_kernel(cond_ref, w_ref, b_ref, o_ref):
    a = _silu(cond_ref[...]).astype(BF16)
    o_ref[...] = _dot(a, w_ref[...].astype(BF16)) + b_ref[...]


def _modulation(cond8, w_mod, b_mod):
    n_out = 6 * D_MODEL
    return pl.pallas_call(
        _mod_kernel,
        grid=(DEPTH, n_out // MOD_TN),
        in_specs=[
            pl.BlockSpec((8, D_MODEL), lambda l, n: (0, 0)),
            pl.BlockSpec((None, D_MODEL, MOD_TN), lambda l, n: (l, 0, n)),
            pl.BlockSpec((None, 1, MOD_TN), lambda l, n: (l, 0, n)),
        ],
        out_specs=pl.BlockSpec((None, 8, MOD_TN), lambda l, n: (l, 0, n)),
        out_shape=jax.ShapeDtypeStruct((DEPTH, 8, n_out), F32),
        compiler_params=_cparams(2),
        name="modulation",
    )(cond8, w_mod, b_mod.reshape(DEPTH, 1, n_out))


def _attn_pre_kernel(nx, *refs):
    (mod_ref, g1_ref, win_ref, qn_ref, kn_ref, qan_ref, kvan_ref, wqb_ref, wkvb_ref, rope_ref,
     qg_ref, kg_ref, vg_ref, kf_ref, vf_ref, ckvf_ref, kpef_ref, qmn_ref, qmp_ref, kvm_ref, kpeb_ref) = refs[nx:]
    m = mod_ref[...]
    h = _rms(_x_tile(refs[:nx]), g1_ref[...]) * (1.0 + m[1:2]) + m[0:1]
    proj = _dot(h.astype(BF16), win_ref[...])
    rope = rope_ref[...]
    ga, gb, ma, mb, md = [rope[:, i * LANES:(i + 1) * LANES] for i in range(5)]

    def rope_g(xh):
        return xh * ga + pltpu.roll(xh, 64, 1) * gb

    def rope_m(xh):
        return xh * ma + pltpu.roll(xh, 96, 1) * mb + pltpu.roll(xh, 32, 1) * md

    for hh in range(GQA_HEADS):
        sl = slice(hh * HEAD_DIM, (hh + 1) * HEAD_DIM)
        qg_ref[:, sl] = rope_g(_rms(proj[:, sl], qn_ref[...])).astype(BF16)
    k0 = GQA_HEADS * HEAD_DIM
    for hh in range(GQA_KV_HEADS):
        sl = slice(hh * HEAD_DIM, (hh + 1) * HEAD_DIM)
        kh = rope_g(_rms(proj[:, k0 + hh * HEAD_DIM:k0 + (hh + 1) * HEAD_DIM], kn_ref[...]))
        kf_ref[:, sl] = kh
        kg_ref[:, sl] = kh.astype(BF16)
    v0 = k0 + GQA_KV_HEADS * HEAD_DIM
    v = proj[:, v0:v0 + GQA_KV_HEADS * HEAD_DIM]
    vf_ref[...] = v
    vg_ref[...] = v.astype(BF16)
    c0 = v0 + GQA_KV_HEADS * HEAD_DIM
    cq = _rms(proj[:, c0:c0 + MLA_Q_RANK], qan_ref[...])
    qm = _dot(cq.astype(BF16), wqb_ref[...])
    n_nope = MLA_HEADS * MLA_NOPE
    qmn_ref[...] = qm[:, :n_nope].astype(BF16)
    for hh in range(MLA_HEADS):
        sl = slice(hh * LANES, (hh + 1) * LANES)
        qmp_ref[:, sl] = rope_m(qm[:, n_nope + hh * LANES:n_nope + (hh + 1) * LANES]).astype(BF16)
    kv0 = c0 + MLA_Q_RANK
    ckv = _rms(proj[:, kv0:kv0 + MLA_KV_RANK], kvan_ref[...])
    ckvf_ref[...] = ckv
    kvm_ref[...] = _dot(ckv.astype(BF16), wkvb_ref[...]).astype(BF16)
    kpe = rope_m(proj[:, kv0 + MLA_KV_RANK:kv0 + MLA_KV_RANK + LANES])
    kpef_ref[...] = kpe
    kpeb_ref[...] = kpe.astype(BF16)


def _rope_tile(j):
    return jnp.where(j < N_CTX_TILES, 0, 1 + (j - N_CTX_TILES) % DEC_TILES_PER_REQ)


def _attn_pre(x, modt, g1, win, qn, kn, qan, kvan, wqb, wkvb, rope_tab):
    tile = lambda w: pl.BlockSpec((TOK, w), lambda j: (j, 0))
    kvw = GQA_KV_HEADS * HEAD_DIM
    outs = [
        (GQA_HEADS * HEAD_DIM, BF16), (kvw, BF16), (kvw, BF16),
        (kvw, F32), (kvw, F32), (MLA_KV_RANK, F32), (LANES, F32),
        (MLA_HEADS * MLA_NOPE, BF16), (MLA_HEADS * LANES, BF16),
        (MLA_HEADS * (MLA_NOPE + MLA_V), BF16), (LANES, BF16),
    ]
    x_specs, x_args = _x_in(x)
    return pl.pallas_call(
        functools.partial(_attn_pre_kernel, len(x_args)),
        grid=(N_TILES,),
        in_specs=x_specs + [
            pl.BlockSpec((None, 6, D_MODEL), lambda j: (j, 0, 0)),
            _const_spec((1, D_MODEL)),
            _const_spec(win.shape),
            _const_spec((1, HEAD_DIM)), _const_spec((1, HEAD_DIM)),
            _const_spec((1, MLA_Q_RANK)), _const_spec((1, MLA_KV_RANK)),
            _const_spec(wqb.shape), _const_spec(wkvb.shape),
            pl.BlockSpec((TOK, 5 * LANES), lambda j: (_rope_tile(j), 0)),
        ],
        out_specs=[tile(w) for w, _ in outs],
        out_shape=[jax.ShapeDtypeStruct((N_TOK, w), dt) for w, dt in outs],
        compiler_params=_cparams(),
        name="attn_pre",
    )(*x_args, modt, g1, win, qn, kn, qan, kvan, wqb, wkvb, rope_tab)


def _rows_matmul_kernel(a_ref, w_ref, o_ref):
    o_ref[...] = _dot(a_ref[...].astype(BF16), w_ref[...]).astype(o_ref.dtype)


def _rows_matmul(a, w, out_dtype, name):
    rows, k = a.shape
    n = w.shape[1]
    return pl.pallas_call(
        _rows_matmul_kernel,
        grid=(rows // TOK,),
        in_specs=[pl.BlockSpec((TOK, k), lambda i: (i, 0)), _const_spec(w.shape)],
        out_specs=pl.BlockSpec((TOK, n), lambda i: (i, 0)),
        out_shape=jax.ShapeDtypeStruct((rows, n), out_dtype),
        compiler_params=_cparams(),
        name=name,
    )(a, w)


def _attend(scores, values):
    m = scores[0].max(axis=-1, keepdims=True)
    for s in scores[1:]:
        m = jnp.maximum(m, s.max(axis=-1, keepdims=True))
    ps = [jnp.exp(s - m) for s in scores]
    l = ps[0].sum(axis=-1, keepdims=True)
    for p in ps[1:]:
        l = l + p.sum(axis=-1, keepdims=True)
    o = _dot(ps[0].astype(BF16), values[0])
    for p, v in zip(ps[1:], values[1:]):
        o = o + _dot(p.astype(BF16), v)
    return o / l


def _attn_heads(q_ref, qn_ref, qp_ref, segs, o_ref):
    for hh in range(GQA_HEADS):
        sl = slice(hh * HEAD_DIM, (hh + 1) * HEAD_DIM)
        kh = hh // (GQA_HEADS // GQA_KV_HEADS)
        ksl = slice(kh * HEAD_DIM, (kh + 1) * HEAD_DIM)
        q = q_ref[:, sl]
        scores = [_dot_nt(q, s[0][:, ksl]) * G_SCALE for s in segs]
        o_ref[:, sl] = _attend(scores, [s[1][:, ksl] for s in segs]).astype(BF16)
    o0 = GQA_HEADS * HEAD_DIM
    v0 = MLA_HEADS * MLA_NOPE
    for hh in range(MLA_HEADS):
        sl = slice(hh * LANES, (hh + 1) * LANES)
        qn = qn_ref[:, sl]
        qp = qp_ref[:, sl]
        scores = [(_dot_nt(qn, s[2][:, sl]) + _dot_nt(qp, s[3][...])) * M_SCALE for s in segs]
        vals = [s[2][:, v0 + hh * MLA_V:v0 + (hh + 1) * MLA_V] for s in segs]
        o_ref[:, o0 + hh * MLA_V:o0 + (hh + 1) * MLA_V] = _attend(scores, vals).astype(BF16)


def _attn_kernel(q_ref, qn_ref, qp_ref,
                 ck_ref, cv_ref, ckvm_ref, ckpe_ref,
                 dk_ref, dv_ref, dkvm_ref, dkpe_ref,
                 pk_ref, pv_ref, pkvm_ref, pkpe_ref,
                 o_ref):
    j = pl.program_id(0)

    @pl.when(j < N_CTX_TILES)
    def _():
        _attn_heads(q_ref, qn_ref, qp_ref, [(ck_ref, cv_ref, ckvm_ref, ckpe_ref)], o_ref)

    @pl.when(j >= N_CTX_TILES)
    def _():
        _attn_heads(q_ref, qn_ref, qp_ref,
                    [(dk_ref, dv_ref, dkvm_ref, dkpe_ref), (pk_ref, pv_ref, pkvm_ref, pkpe_ref)], o_ref)


def _dec_req(j):
    return jnp.maximum(j - N_CTX_TILES, 0) // DEC_TILES_PER_REQ


def _attn(qg, kg, vg, qmn, qmp, kvm, kpeb, pk, pv, pkvm, pkpe):
    tile = lambda w: pl.BlockSpec((TOK, w), lambda j: (j, 0))
    ctx = lambda w: pl.BlockSpec((TOK, w), lambda j: (jnp.minimum(j, N_CTX_TILES - 1), 0))
    dec = lambda w: pl.BlockSpec((DEC_SEQ, w), lambda j: (CTX_TOK // DEC_SEQ + _dec_req(j), 0))
    past = lambda w: pl.BlockSpec((PAST_LEN, w), lambda j: (_dec_req(j), 0))
    kvw = GQA_KV_HEADS * HEAD_DIM
    mw = MLA_HEADS * (MLA_NOPE + MLA_V)
    return pl.pallas_call(
        _attn_kernel,
        grid=(N_TILES,),
        in_specs=[tile(GQA_HEADS * HEAD_DIM), tile(MLA_HEADS * MLA_NOPE), tile(MLA_HEADS * LANES),
                  ctx(kvw), ctx(kvw), ctx(mw), ctx(LANES),
                  dec(kvw), dec(kvw), dec(mw), dec(LANES),
                  past(kvw), past(kvw), past(mw), past(LANES)],
        out_specs=tile(D_MODEL),
        out_shape=jax.ShapeDtypeStruct((N_TOK, D_MODEL), BF16),
        compiler_params=_cparams(),
        name="attention",
    )(qg, qmn, qmp, kg, vg, kvm, kpeb, kg, vg, kvm, kpeb, pk, pv, pkvm, pkpe)


def _residual_and_router(x, mix_out, m, g2_ref, wr_ref, x1_ref, hff_ref, aff_ref):
    x1 = x + m[2:3] * mix_out
    x1_ref[...] = x1
    hf = _rms(x1, g2_ref[...]) * (1.0 + m[4:5]) + m[3:4]
    hi = hf.astype(BF16)
    hff_ref[...] = hi
    lo = (hf - hi.astype(F32)).astype(BF16)
    r = _dot(hi, wr_ref[0]) + _dot(lo, wr_ref[1])
    logits = r + pltpu.roll(r, LANES - N_EXPERTS, 1)
    lane = lax.broadcasted_iota(jnp.int32, logits.shape, 1)
    logits = jnp.where(lane < N_EXPERTS, logits, -jnp.inf)
    e = jnp.exp(logits - logits.max(axis=-1, keepdims=True))
    aff_ref[...] = e / e.sum(axis=-1, keepdims=True)


def _epilogue_specs():
    tile = lambda w: pl.BlockSpec((TOK, w), lambda j: (j, 0))
    out_specs = [tile(D_MODEL), tile(D_MODEL), tile(LANES)]
    out_shape = [jax.ShapeDtypeStruct((N_TOK, D_MODEL), F32),
                 jax.ShapeDtypeStruct((N_TOK, D_MODEL), BF16),
                 jax.ShapeDtypeStruct((N_TOK, LANES), F32)]
    return out_specs, out_shape


def _attn_out_kernel(nx, *refs):
    o_ref, wout_ref, mod_ref, g2_ref, wr_ref, x1_ref, hff_ref, aff_ref = refs[nx:]
    out = _dot(o_ref[...], wout_ref[...])
    _residual_and_router(_x_tile(refs[:nx]), out, mod_ref[...], g2_ref, wr_ref, x1_ref, hff_ref, aff_ref)


def _attn_out(o, wout, x, modt, g2, wr):
    tile = lambda w: pl.BlockSpec((TOK, w), lambda j: (j, 0))
    out_specs, out_shape = _epilogue_specs()
    x_specs, x_args = _x_in(x)
    return pl.pallas_call(
        functools.partial(_attn_out_kernel, len(x_args)),
        grid=(N_TILES,),
        in_specs=x_specs + [tile(D_MODEL), _const_spec(wout.shape),
                            pl.BlockSpec((None, 6, D_MODEL), lambda j: (j, 0, 0)),
                            _const_spec((1, D_MODEL)), _const_spec(wr.shape)],
        out_specs=out_specs, out_shape=out_shape,
        compiler_params=_cparams(),
        name="attn_out",
    )(*x_args, o, wout, modt, g2, wr)


def _cs_pre_kernel(x_ref, mod_ref, g1_ref, win_ref, ug_ref, ut_ref):
    m = mod_ref[...]
    h = _rms(x_ref[...], g1_ref[...]) * (1.0 + m[1:2]) + m[0:1]
    proj = _dot(h.astype(BF16), win_ref[...])
    ug_ref[...] = proj[:, :CONV_DIM] * jax.nn.sigmoid(proj[:, CONV_DIM:2 * CONV_DIM])
    ut_ref[...] = proj[:, 2 * CONV_DIM:]


def _cs_pre(x, modt, g1, win):
    return pl.pallas_call(
        _cs_pre_kernel,
        grid=(N_TILES,),
        in_specs=[pl.BlockSpec((TOK, D_MODEL), lambda j: (j, 0)),
                  pl.BlockSpec((None, 6, D_MODEL), lambda j: (j, 0, 0)),
                  _const_spec((1, D_MODEL)), _const_spec(win.shape)],
        out_specs=[pl.BlockSpec((TOK, CONV_DIM), lambda j: (j, 0)),
                   pl.BlockSpec((TOK, S5_DIM), lambda j: (0, j))],
        out_shape=[jax.ShapeDtypeStruct((N_TOK, CONV_DIM), F32),
                   jax.ShapeDtypeStruct((TOK, N_TILES * S5_DIM), F32)],
        compiler_params=_cparams(),
        name="cs_pre",
    )(x, modt, g1, win)


CONV_HALO = 16
SUBLANES = 8
CONV_BASE = CONV_HALO - CONV_WIDTH // 2
CONV_SH_ROWS = (CONV_BASE + CONV_WIDTH - 1) // SUBLANES * SUBLANES + TOK


def _conv_kernel(prev_ref, cur_ref, next_ref, w_ref, b_ref, lng_ref, lnb_ref, o_ref, pad_ref, sh_ref, u_ref):
    j = pl.program_id(0)
    q = (j - N_CTX_TILES) % DEC_TILES_PER_REQ
    has_prev = jnp.logical_and(j >= N_CTX_TILES, q != 0)
    has_next = jnp.logical_and(j >= N_CTX_TILES, q != DEC_TILES_PER_REQ - 1)
    pad_ref[0:CONV_HALO, :] = jnp.where(has_prev, prev_ref[TOK - CONV_HALO:, :], 0.0)
    pad_ref[CONV_HALO:CONV_HALO + TOK, :] = cur_ref[...]
    pad_ref[CONV_HALO + TOK:, :] = jnp.where(has_next, next_ref[0:CONV_HALO, :], 0.0)
    for s in range(1, SUBLANES):
        sh_ref[s - 1] = pad_ref[s:s + CONV_SH_ROWS, :]
    for lt in range(CONV_DIM // LANES):
        ls = slice(lt * LANES, (lt + 1) * LANES)
        acc = None
        for tap in range(CONV_WIDTH):
            a8, s = (CONV_BASE + tap) // SUBLANES * SUBLANES, (CONV_BASE + tap) % SUBLANES
            win = pad_ref[a8:a8 + TOK, ls] if s == 0 else sh_ref[s - 1, a8:a8 + TOK, ls]
            term = w_ref[tap:tap + 1, ls] * win
            acc = term if acc is None else acc + term
        u_ref[:, ls] = acc + b_ref[:, ls]
    u = u_ref[...]
    mu = jnp.mean(u, axis=-1, keepdims=True)
    uc = u - mu
    y = uc * lax.rsqrt(jnp.mean(uc * uc, axis=-1, keepdims=True) + EPS) * lng_ref[...] + lnb_ref[...]
    o_ref[...] = _silu(y).astype(BF16)


def _conv(ug, w, b, lng, lnb):
    nb = lambda d: pl.BlockSpec((TOK, CONV_DIM), lambda j: (jnp.clip(j + d, 0, N_TILES - 1), 0))
    return pl.pallas_call(
        _conv_kernel,
        grid=(N_TILES,),
        in_specs=[nb(-1), nb(0), nb(1), _const_spec(w.shape),
                  _const_spec((1, CONV_DIM)), _const_spec((1, CONV_DIM)), _const_spec((1, CONV_DIM))],
        out_specs=pl.BlockSpec((TOK, CONV_DIM), lambda j: (j, 0)),
        out_shape=jax.ShapeDtypeStruct((N_TOK, CONV_DIM), BF16),
        scratch_shapes=[pltpu.VMEM((TOK + 2 * CONV_HALO, CONV_DIM), F32),
                        pltpu.VMEM((SUBLANES - 1, CONV_SH_ROWS, CONV_DIM), F32),
                        pltpu.VMEM((TOK, CONV_DIM), F32)],
        compiler_params=_cparams(),
        name="conv_branch",
    )(ug, ug, ug, w, b, lng, lnb)


def _s5_scan_quarter(lam_ref, st_ref, bu_ref, d, q, rows, reverse):
    half = S5_QS // 2
    for cb in range(2):
        c_re = slice(cb * half, (cb + 1) * half)
        c_im = slice(S5_QS + cb * half, S5_QS + (cb + 1) * half)
        lr = lam_ref[d, q, 0:1, c_re]
        li = lam_ref[d, q, 1:2, c_re]
        s0 = q * 2 * S5_QS
        sr = st_ref[d, :, s0 + cb * half:s0 + (cb + 1) * half]
        si = st_ref[d, :, s0 + S5_QS + cb * half:s0 + S5_QS + (cb + 1) * half]

        def body(k, carry):
            sr, si = carry
            t = (S5_TT - 1 - k) if reverse else k
            r0 = pl.multiple_of(t * rows, 8)
            nr = lr * sr - li * si + bu_ref[pl.ds(r0, rows), c_re]
            ni = lr * si + li * sr + bu_ref[pl.ds(r0, rows), c_im]
            bu_ref[pl.ds(r0, rows), c_re] = nr
            bu_ref[pl.ds(r0, rows), c_im] = ni
            return nr, ni

        sr, si = lax.fori_loop(0, S5_TT, body, (sr, si))
        st_ref[d, :, s0 + cb * half:s0 + (cb + 1) * half] = sr
        st_ref[d, :, s0 + S5_QS + cb * half:s0 + S5_QS + (cb + 1) * half] = si


def _s5_kernel(rows, has_input, *refs):
    if has_input:
        uf_ref, ur_ref, wb_ref, dsk_ref = refs[:4]
        refs = refs[4:]
    wc_ref, lam_ref, s0_ref, yf_ref, yr_ref, fin_ref, st_ref, bu_ref = refs
    i = pl.program_id(0)

    @pl.when(i == 0)
    def _():
        st_ref[...] = s0_ref[...]

    for d, (y_ref, reverse) in enumerate(((yf_ref, False), (yr_ref, True))):
        if has_input:
            u = (ur_ref if reverse else uf_ref)[...]
        for q in range(S5_Q):
            csl = slice(q * S5_QC, (q + 1) * S5_QC)
            if has_input:
                bu_ref[...] = _dot(u[:, csl].astype(BF16), wb_ref[d, q])
            else:
                bu_ref[...] = jnp.zeros_like(bu_ref)
            _s5_scan_quarter(lam_ref, st_ref, bu_ref, d, q, rows, reverse)
            y = _dot(bu_ref[...].astype(BF16), wc_ref[d, q])
            if has_input and not reverse:
                y = y + dsk_ref[:, csl] * u[:, csl]
            y_ref[:, csl] = y

    @pl.when(i == S5_NB - 1)
    def _():
        fin_ref[...] = st_ref[...]


def _s5(rows, ut, wb, wc, lam, s0, dskip):
    blk = S5_TT * rows
    fwd = pl.BlockSpec((blk, S5_DIM), lambda i: (i, 0))
    rev = pl.BlockSpec((blk, S5_DIM), lambda i: (S5_NB - 1 - i, 0))
    has_input = ut is not None
    in_specs, args = [], []
    if has_input:
        in_specs += [fwd, rev, _const_spec(wb.shape), _const_spec((1, S5_DIM))]
        args += [ut, ut, wb, dskip]
    in_specs += [_const_spec(wc.shape), _const_spec(lam.shape), _const_spec(s0.shape)]
    args += [wc, lam, s0]
    return pl.pallas_call(
        functools.partial(_s5_kernel, rows, has_input),
        grid=(S5_NB,),
        in_specs=in_specs,
        out_specs=[fwd, rev, _const_spec(s0.shape)],
        out_shape=[jax.ShapeDtypeStruct((TOK * rows, S5_DIM), F32),
                   jax.ShapeDtypeStruct((TOK * rows, S5_DIM), F32),
                   jax.ShapeDtypeStruct(s0.shape, F32)],
        scratch_shapes=[pltpu.VMEM(s0.shape, F32), pltpu.VMEM((blk, 2 * S5_QS), F32)],
        compiler_params=_cparams(),
        name="s5_scan" if has_input else "s5_carry_fix",
    )(*args)


def _gelu_tanh(x):
    return x * (0.5 * (1.0 + jnp.tanh(math.sqrt(2.0 / math.pi) * (x + 0.044715 * (x * x * x)))))


def _cs_post_kernel(uc_ref, yf_ref, yr_ref, cf_ref, cr_ref, wglu_ref, bglu_ref, wout_ref, x_ref, mod_ref,
                    g2_ref, wr_ref, x1_ref, hff_ref, aff_ref):
    y = _gelu_tanh(yf_ref[...] + yr_ref[...] + cf_ref[...] + cr_ref[...])
    y = y * jax.nn.sigmoid(_dot(y.astype(BF16), wglu_ref[...]) + bglu_ref[...])
    out = _dot(uc_ref[...], wout_ref[0:CONV_DIM, :]) + _dot(y.astype(BF16), wout_ref[CONV_DIM:, :])
    _residual_and_router(x_ref[...], out, mod_ref[...], g2_ref, wr_ref, x1_ref, hff_ref, aff_ref)


def _fix_slot(j, reverse):
    d = j - N_CTX_TILES
    r, q = d // DEC_TILES_PER_REQ, d % DEC_TILES_PER_REQ
    if reverse:
        ok, slot = q != DEC_TILES_PER_REQ - 1, r * (DEC_TILES_PER_REQ - 1) + q
    else:
        ok, slot = q != 0, r * (DEC_TILES_PER_REQ - 1) + q - 1
    return jnp.where(jnp.logical_and(j >= N_CTX_TILES, ok), slot, S5_FIX_ROWS - 1)


def _cs_post(uc, yf, yr, cf, cr, wglu, bglu, wout, x, modt, g2, wr):
    tile = lambda w: pl.BlockSpec((TOK, w), lambda j: (j, 0))
    tmaj = pl.BlockSpec((TOK, S5_DIM), lambda j: (0, j))
    out_specs, out_shape = _epilogue_specs()
    return pl.pallas_call(
        _cs_post_kernel,
        grid=(N_TILES,),
        in_specs=[tile(CONV_DIM), tmaj, tmaj,
                  pl.BlockSpec((TOK, S5_DIM), lambda j: (0, _fix_slot(j, False))),
                  pl.BlockSpec((TOK, S5_DIM), lambda j: (0, _fix_slot(j, True))),
                  _const_spec(wglu.shape), _const_spec((1, S5_DIM)), _const_spec(wout.shape),
                  tile(D_MODEL), pl.BlockSpec((None, 6, D_MODEL), lambda j: (j, 0, 0)),
                  _const_spec((1, D_MODEL)), _const_spec(wr.shape)],
        out_specs=out_specs, out_shape=out_shape,
        compiler_params=_cparams(),
        name="cs_post",
    )(uc, yf, yr, cf, cr, wglu, bglu, wout, x, modt, g2, wr)


RANK_CHUNK = 128
GATHER_ROWS = 512


def _moe_select(cap, aff_ref, hff_ref, xs_ref, gs_ref, g_ref, rank_ref, p_ref):
    n = aff_ref.shape[0]
    nblk = n // RANK_CHUNK
    a_t = aff_ref[...].T
    sub = lax.broadcasted_iota(jnp.int32, (RANK_CHUNK, RANK_CHUNK), 0)
    lan = lax.broadcasted_iota(jnp.int32, (RANK_CHUNK, RANK_CHUNK), 1)
    earlier = sub < lan
    slot = lax.broadcasted_iota(jnp.int32, (cap, n), 0).astype(F32)
    rank_ref[...] = jnp.zeros_like(rank_ref)
    for e in range(N_EXPERTS):
        row = a_t[e:e + 1, :]
        cols = [aff_ref[c * RANK_CHUNK:(c + 1) * RANK_CHUNK, e:e + 1] for c in range(nblk)]
        for b in range(nblk):
            rb = row[:, b * RANK_CHUNK:(b + 1) * RANK_CHUNK]
            cnt = None
            for c in range(nblk):
                if c < b:
                    part = jnp.where(cols[c] >= rb, 1.0, 0.0)
                elif c > b:
                    part = jnp.where(cols[c] > rb, 1.0, 0.0)
                else:
                    tie = jnp.where(earlier, cols[c], -1.0) == rb
                    part = jnp.where(cols[c] > rb, 1.0, 0.0) + jnp.where(tie, 1.0, 0.0)
                cnt = part if cnt is None else cnt + part
            rank_ref[e:e + 1, b * RANK_CHUNK:(b + 1) * RANK_CHUNK] = jnp.sum(cnt, axis=0, keepdims=True)
        rank = rank_ref[e:e + 1, :]
        onehot = slot == rank
        p_ref[e * cap:(e + 1) * cap, :] = jnp.where(onehot, 1.0, 0.0).astype(BF16)
        gs_ref[e] = jnp.sum(jnp.where(onehot, row, 0.0), axis=1, keepdims=True)
    grp = GATHER_ROWS // cap
    for e0 in range(0, N_EXPERTS, grp):
        xs = _dot(p_ref[e0 * cap:(e0 + grp) * cap, :], hff_ref[...])
        for k in range(grp):
            xs_ref[e0 + k] = xs[k * cap:(k + 1) * cap].astype(BF16)
    rank_t = rank_ref[...].T
    per = LANES // cap
    lane = lax.broadcasted_iota(jnp.int32, (n, LANES), 1)
    slot_lane = (lane & (cap - 1)).astype(F32)
    for blk in range(N_EXPERTS // per):
        rc = rank_t[:, blk * per:blk * per + 1]
        for k in range(1, per):
            rc = jnp.where(lane >= k * cap, rank_t[:, blk * per + k:blk * per + k + 1], rc)
        g_ref[:, blk * LANES:(blk + 1) * LANES] = jnp.where(slot_lane == rc, 1.0, 0.0).astype(BF16)


def _moe_gather_kernel(affc_ref, hffc_ref, affd_ref, hffd_ref,
                       xsc_ref, gsc_ref, gc_ref, xsd_ref, gsd_ref, gd_ref,
                       rankc_ref, pc_ref, rankd_ref, pd_ref):
    s = pl.program_id(0)

    @pl.when(s < BATCH)
    def _():
        _moe_select(CAP_CTX, affc_ref, hffc_ref, xsc_ref, gsc_ref, gc_ref, rankc_ref, pc_ref)

    @pl.when(s >= BATCH)
    def _():
        _moe_select(CAP_DEC, affd_ref, hffd_ref, xsd_ref, gsd_ref, gd_ref, rankd_ref, pd_ref)


def _moe_gather(aff, hff):
    cidx = lambda s: jnp.minimum(s, BATCH - 1)
    didx = lambda s: jnp.maximum(s - BATCH, 0)
    dec0 = CTX_TOK // DEC_SEQ
    return pl.pallas_call(
        _moe_gather_kernel,
        grid=(BATCH + DEC_BATCH,),
        in_specs=[pl.BlockSpec((SEQ, LANES), lambda s: (cidx(s), 0)),
                  pl.BlockSpec((SEQ, D_MODEL), lambda s: (cidx(s), 0)),
                  pl.BlockSpec((DEC_SEQ, LANES), lambda s: (dec0 + didx(s), 0)),
                  pl.BlockSpec((DEC_SEQ, D_MODEL), lambda s: (dec0 + didx(s), 0))],
        out_specs=[pl.BlockSpec((N_EXPERTS, CAP_CTX, D_MODEL), lambda s: (0, cidx(s), 0)),
                   pl.BlockSpec((N_EXPERTS, CAP_CTX, 1), lambda s: (0, cidx(s), 0)),
                   pl.BlockSpec((SEQ, N_EXPERTS * CAP_CTX), lambda s: (cidx(s), 0)),
                   pl.BlockSpec((N_EXPERTS, CAP_DEC, D_MODEL), lambda s: (0, didx(s), 0)),
                   pl.BlockSpec((N_EXPERTS, CAP_DEC, 1), lambda s: (0, didx(s), 0)),
                   pl.BlockSpec((DEC_SEQ, N_EXPERTS * CAP_DEC), lambda s: (didx(s), 0))],
        out_shape=[jax.ShapeDtypeStruct((N_EXPERTS, XS_CTX_ROWS, D_MODEL), BF16),
                   jax.ShapeDtypeStruct((N_EXPERTS, XS_CTX_ROWS, 1), F32),
                   jax.ShapeDtypeStruct((CTX_TOK, N_EXPERTS * CAP_CTX), BF16),
                   jax.ShapeDtypeStruct((N_EXPERTS, XS_DEC_ROWS, D_MODEL), BF16),
                   jax.ShapeDtypeStruct((N_EXPERTS, XS_DEC_ROWS, 1), F32),
                   jax.ShapeDtypeStruct((DEC_BATCH * DEC_SEQ, N_EXPERTS * CAP_DEC), BF16)],
        scratch_shapes=[pltpu.VMEM((LANES, SEQ), F32), pltpu.VMEM((N_EXPERTS * CAP_CTX, SEQ), BF16),
                        pltpu.VMEM((LANES, DEC_SEQ), F32), pltpu.VMEM((N_EXPERTS * CAP_DEC, DEC_SEQ), BF16)],
        compiler_params=_cparams(),
        name="moe_gather",
    )(aff, hff, aff, hff)


FF_CHUNK = 256
N_FF_CHUNKS = EXPERT_FF // FF_CHUNK


def _moe_ffn_kernel(xsc_ref, xsd_ref, gsc_ref, gsd_ref, wg_ref, wu_ref, wd_ref, yc_ref, yd_ref, acc_ref):
    f = pl.program_id(1)

    @pl.when(f == 0)
    def _():
        acc_ref[...] = jnp.zeros_like(acc_ref)

    wg = wg_ref[...].astype(BF16)
    wu = wu_ref[...].astype(BF16)
    wd = wd_ref[...].astype(BF16)
    for xs_ref, r0, rows in ((xsc_ref, 0, XS_CTX_ROWS), (xsd_ref, XS_CTX_ROWS, XS_DEC_ROWS)):
        xs = xs_ref[...]
        act = _silu(_dot(xs, wg)) * _dot(xs, wu)
        acc_ref[r0:r0 + rows, :] += _dot(act.astype(BF16), wd)

    @pl.when(f == N_FF_CHUNKS - 1)
    def _():
        yc_ref[...] = (acc_ref[0:XS_CTX_ROWS, :] * gsc_ref[...]).astype(BF16)
        yd_ref[...] = (acc_ref[XS_CTX_ROWS:, :] * gsd_ref[...]).astype(BF16)


def _moe_ffn(layer, xsc, xsd, gsc, gsd, w_gate, w_up, w_down):
    per_e = lambda rows, w: pl.BlockSpec((None, rows, w), lambda e, f: (e, 0, 0))
    return pl.pallas_call(
        _moe_ffn_kernel,
        grid=(N_EXPERTS, N_FF_CHUNKS),
        in_specs=[per_e(XS_CTX_ROWS, D_MODEL), per_e(XS_DEC_ROWS, D_MODEL),
                  per_e(XS_CTX_ROWS, 1), per_e(XS_DEC_ROWS, 1),
                  pl.BlockSpec((None, None, D_MODEL, FF_CHUNK), lambda e, f: (layer, e, 0, f)),
                  pl.BlockSpec((None, None, D_MODEL, FF_CHUNK), lambda e, f: (layer, e, 0, f)),
                  pl.BlockSpec((None, None, FF_CHUNK, D_MODEL), lambda e, f: (layer, e, f, 0))],
        out_specs=[per_e(XS_CTX_ROWS, D_MODEL), per_e(XS_DEC_ROWS, D_MODEL)],
        out_shape=[jax.ShapeDtypeStruct((N_EXPERTS, XS_CTX_ROWS, D_MODEL), BF16),
                   jax.ShapeDtypeStruct((N_EXPERTS, XS_DEC_ROWS, D_MODEL), BF16)],
        scratch_shapes=[pltpu.VMEM((XS_CTX_ROWS + XS_DEC_ROWS, D_MODEL), F32)],
        compiler_params=_cparams(2),
        name="moe_ffn",
    )(xsc, xsd, gsc, gsd, w_gate, w_up, w_down)


def _moe_combine_kernel(final, gc_ref, yc_ref, gd_ref, yd_ref, x1_ref, mod_ref, fg_ref, *o_refs):
    j = pl.program_id(0)

    def finish(comb, o_ref):
        x2 = x1_ref[...] + mod_ref[5:6, :] * comb
        o_ref[...] = _rms(x2, fg_ref[...]) if final else x2

    @pl.when(j < N_CTX_TILES)
    def _():
        ys = jnp.concatenate([yc_ref[e] for e in range(N_EXPERTS)], axis=0)
        finish(_dot(gc_ref[...], ys), o_refs[0])

    @pl.when(j >= N_CTX_TILES)
    def _():
        ys = jnp.concatenate([yd_ref[e] for e in range(N_EXPERTS)], axis=0)
        finish(_dot(gd_ref[...], ys), o_refs[-1])


def _moe_combine(final, gc, yc, gd, yd, x1, modt, fg):
    if final:
        out_specs = [pl.BlockSpec((TOK, D_MODEL), lambda j: (_ctx_tile(j), 0)),
                     pl.BlockSpec((TOK, D_MODEL), lambda j: (_dec_tile(j), 0))]
        out_shape = [jax.ShapeDtypeStruct((CTX_TOK, D_MODEL), F32),
                     jax.ShapeDtypeStruct((N_TOK - CTX_TOK, D_MODEL), F32)]
    else:
        out_specs = pl.BlockSpec((TOK, D_MODEL), lambda j: (j, 0))
        out_shape = jax.ShapeDtypeStruct((N_TOK, D_MODEL), F32)
    return pl.pallas_call(
        functools.partial(_moe_combine_kernel, final),
        grid=(N_TILES,),
        in_specs=[pl.BlockSpec((TOK, N_EXPERTS * CAP_CTX), lambda j: (_ctx_tile(j), 0)),
                  pl.BlockSpec((N_EXPERTS, CAP_CTX, D_MODEL), lambda j: (0, _ctx_tile(j), 0)),
                  pl.BlockSpec((TOK, N_EXPERTS * CAP_DEC), lambda j: (_dec_tile(j), 0)),
                  pl.BlockSpec((N_EXPERTS, CAP_DEC, D_MODEL), lambda j: (0, _dec_tile(j) // DEC_TILES_PER_REQ, 0)),
                  pl.BlockSpec((TOK, D_MODEL), lambda j: (j, 0)),
                  pl.BlockSpec((None, 6, D_MODEL), lambda j: (j, 0, 0)),
                  _const_spec((1, D_MODEL))],
        out_specs=out_specs, out_shape=out_shape,
        compiler_params=_cparams(),
        name="moe_combine",
    )(gc, yc, gd, yd, x1, modt, fg)


def _moe(layer, final, x1, hff, aff, modt, w_gate, w_up, w_down, fg):
    xsc, gsc, gc, xsd, gsd, gd = _moe_gather(aff, hff)
    yc, yd = _moe_ffn(layer, xsc, xsd, gsc, gsd, w_gate, w_up, w_down)
    return _moe_combine(final, gc, yc, gd, yd, x1, modt, fg)


def _axial_rope(n_tokens, rot_dim):
    rows = n_tokens // GRID_W
    per_axis = rot_dim // 4
    freqs = ROPE_THETA ** (-jnp.arange(per_axis, dtype=F32) / per_axis)
    row = jnp.repeat(jnp.arange(rows, dtype=F32), GRID_W)
    col = jnp.tile(jnp.arange(GRID_W, dtype=F32), rows)
    ang = jnp.concatenate([row[:, None] * freqs, col[:, None] * freqs], axis=-1)
    return jnp.cos(ang), jnp.sin(ang)


def _rope_table():
    cg, sg = _axial_rope(DEC_SEQ, HEAD_DIM)
    cm, sm = _axial_rope(DEC_SEQ, MLA_ROPE)
    z = jnp.zeros_like(cm)
    pos = jnp.concatenate([
        jnp.concatenate([cg, cg], -1), jnp.concatenate([-sg, sg], -1),
        jnp.concatenate([cm, cm, z, z], -1), jnp.concatenate([-sm, z, z, z], -1),
        jnp.concatenate([z, sm, z, z], -1)], axis=-1)
    one, zero = jnp.ones((TOK, LANES), F32), jnp.zeros((TOK, LANES), F32)
    ident = jnp.concatenate([one, zero, one, zero, zero], axis=-1)
    return jnp.concatenate([ident, pos], axis=0)


def _router_weights(w):
    hi = w.astype(BF16)
    lo = (w - hi.astype(F32)).astype(BF16)
    z = jnp.zeros((w.shape[0], LANES - 2 * N_EXPERTS), BF16)
    return jnp.stack([jnp.concatenate([hi, lo, z], axis=1),
                      jnp.concatenate([hi, jnp.zeros_like(lo), z], axis=1)])


def _attn_weights(w_in, w_qb, w_kvb, w_out):
    win = jnp.pad(w_in.astype(BF16), ((0, 0), (0, ATTN_IN_PAD - w_in.shape[1])))
    qb = w_qb.astype(BF16).reshape(MLA_Q_RANK, MLA_HEADS, MLA_NOPE + MLA_ROPE)
    qb_pe = jnp.pad(qb[:, :, MLA_NOPE:], ((0, 0), (0, 0), (0, LANES - MLA_ROPE)))
    wqb = jnp.concatenate([qb[:, :, :MLA_NOPE].reshape(MLA_Q_RANK, -1), qb_pe.reshape(MLA_Q_RANK, -1)], axis=1)
    kvb = w_kvb.astype(BF16).reshape(MLA_KV_RANK, MLA_HEADS, MLA_NOPE + MLA_V)
    wkvb = jnp.concatenate([kvb[:, :, :MLA_NOPE].reshape(MLA_KV_RANK, -1),
                            kvb[:, :, MLA_NOPE:].reshape(MLA_KV_RANK, -1)], axis=1)
    return win, wqb, wkvb, w_out.astype(BF16)


def _s5_weights(a_re, a_im, log_step, b_re, b_im, c_re, c_im):
    lam = lax.complex(a_re.astype(F32), a_im.astype(F32))
    step = jnp.exp(log_step.astype(F32))[..., None]
    lam_bar = jnp.exp(lam * step)
    b_bar = ((lam_bar - 1.0) / lam)[..., None] * lax.complex(b_re.astype(F32), b_im.astype(F32))
    gq = S5_GROUPS // S5_Q
    eye = jnp.eye(gq, dtype=F32)

    def blockdiag_in(m):
        m = m.reshape(2, S5_Q, gq, S5_STATE, S5_GROUP)
        return jnp.einsum('ab,dqapc->dqacbp', eye, m).reshape(2, S5_Q, S5_QC, S5_QS)

    def blockdiag_out(m):
        m = m.reshape(2, S5_Q, gq, S5_GROUP, S5_STATE)
        return jnp.einsum('ab,dqacp->dqbpac', eye, m).reshape(2, S5_Q, S5_QS, S5_QC)

    wb = jnp.concatenate([blockdiag_in(jnp.real(b_bar)), blockdiag_in(jnp.imag(b_bar))], axis=-1).astype(BF16)
    wc = jnp.concatenate([blockdiag_out(c_re.astype(F32)), blockdiag_out(-c_im.astype(F32))], axis=-2).astype(BF16)
    lam_q = jnp.stack([jnp.real(lam_bar), jnp.imag(lam_bar)], axis=1)
    lam_q = lam_q.reshape(2, 2, S5_Q, S5_QS).transpose(0, 2, 1, 3)
    lam_chunk = jnp.exp(lam * step * float(TOK)).reshape(2, S5_GROUPS * S5_STATE)
    return wb, wc, lam_q, lam_chunk


def _cols_to_complex(rows):
    r = rows.reshape(rows.shape[:-1] + (S5_Q, 2, S5_QS))
    return lax.complex(r[..., 0, :], r[..., 1, :]).reshape(rows.shape[:-1] + (S5_Q * S5_QS,))


def _complex_to_cols(z):
    r = jnp.stack([jnp.real(z), jnp.imag(z)], axis=-2)
    r = r.reshape(z.shape[:-1] + (2, S5_Q, S5_QS))
    return jnp.swapaxes(r, -3, -2).reshape(z.shape[:-1] + (S5_STATE_COLS,))


def _s5_branch(ut, state_i, a_re, a_im, log_step, b_re, b_im, c_re, c_im, dskip):
    wb, wc, lam_q, lam_chunk = _s5_weights(a_re, a_im, log_step, b_re, b_im, c_re, c_im)
    h0 = _complex_to_cols(lax.complex(state_i[:, :, 0], state_i[:, :, 1]).reshape(DEC_BATCH, 2, -1))
    s0 = jnp.zeros((2, N_TILES, S5_STATE_COLS), F32)
    first = N_CTX_TILES + DEC_TILES_PER_REQ * jnp.arange(DEC_BATCH)
    s0 = s0.at[0, first].set(h0[:, 0]).at[1, first + DEC_TILES_PER_REQ - 1].set(h0[:, 1])
    yf, yr, fin = _s5(N_TILES, ut.reshape(TOK * N_TILES, S5_DIM), wb, wc, lam_q, s0, dskip)
    fz = _cols_to_complex(fin[:, N_CTX_TILES:]).reshape(2, DEC_BATCH, DEC_TILES_PER_REQ, -1)
    lc = lam_chunk[:, None, :]
    f1 = fz[0, :, 0]
    f2 = fz[0, :, 1] + lc[0] * f1
    f3 = fz[0, :, 2] + lc[0] * f2
    r2 = fz[1, :, 3]
    r1 = fz[1, :, 2] + lc[1] * r2
    r0 = fz[1, :, 1] + lc[1] * r1
    pad = jnp.zeros((S5_FIX_ROWS - DEC_BATCH * (DEC_TILES_PER_REQ - 1), S5_STATE_COLS), F32)
    sin_f = jnp.concatenate([_complex_to_cols(jnp.stack([f1, f2, f3], 1)).reshape(-1, S5_STATE_COLS), pad])
    sin_r = jnp.concatenate([_complex_to_cols(jnp.stack([r0, r1, r2], 1)).reshape(-1, S5_STATE_COLS), pad])
    cf, cr, _ = _s5(S5_FIX_ROWS, None, None, wc, lam_q, jnp.stack([sin_f, sin_r]), None)
    tm = lambda a, rows: a.reshape(TOK, rows * S5_DIM)
    new_state = fin[:, :N_CTX_TILES].reshape(2, BATCH, S5_Q, 2, S5_QS).transpose(1, 0, 3, 2, 4)
    new_state = new_state.reshape(BATCH, 2, 2, S5_GROUPS, S5_STATE)
    return tm(yf, N_TILES), tm(yr, N_TILES), tm(cf, S5_FIX_ROWS), tm(cr, S5_FIX_ROWS), new_state


def kernel(x_prompt, x_sample, cache_gqa_k, cache_gqa_v, cache_mla_ckv, cache_mla_kpe, state_s5, c, c_ctx, w_mod, b_mod, norm1_g, norm2_g, attn_w_in, gqa_q_norm, gqa_k_norm, mla_qa_norm, mla_w_qb, mla_kva_norm, mla_w_kvb, attn_w_out, cs_w_in, conv_w, conv_b, conv_ln_g, conv_ln_b, s5_a_re, s5_a_im, s5_log_step, s5_b_re, s5_b_im, s5_c_re, s5_c_im, s5_d, s5_w_glu, s5_b_glu, cs_w_out, moe_router, moe_w_gate, moe_w_up, moe_w_down, final_norm_g):
    x = (x_prompt.reshape(CTX_TOK, D_MODEL), x_sample.reshape(DEC_BATCH * DEC_SEQ, D_MODEL))
    cond8 = jnp.concatenate([c_ctx[None, :], c, jnp.zeros((8 - 1 - DEC_BATCH, D_MODEL), F32)])
    mod = _modulation(cond8, w_mod, b_mod)
    tile_row = np.concatenate([np.zeros(N_CTX_TILES, np.int32),
                               1 + np.repeat(np.arange(DEC_BATCH, dtype=np.int32), DEC_TILES_PER_REQ)])
    modt = mod[:, tile_row].reshape(DEPTH, N_TILES, 6, D_MODEL)
    rope_tab = _rope_table()
    row = lambda v: v.reshape(1, -1)
    fg = row(final_norm_g)
    new_k, new_v, new_ckv, new_kpe, new_s5 = [], [], [], [], []
    for l in range(DEPTH):
        i = l // 2
        wr = _router_weights(moe_router[l])
        if l % 2 == 0:
            win, wqb, wkvb, wout = _attn_weights(attn_w_in[i], mla_w_qb[i], mla_w_kvb[i], attn_w_out[i])
            (qg, kg, vg, kf, vf, ckvf, kpef, qmn, qmp, kvm, kpeb) = _attn_pre(
                x, modt[l], row(norm1_g[l]), win, row(gqa_q_norm[i]), row(gqa_k_norm[i]),
                row(mla_qa_norm[i]), row(mla_kva_norm[i]), wqb, wkvb, rope_tab)
            n_past = DEC_BATCH * PAST_LEN
            pk = cache_gqa_k[:, i].reshape(n_past, -1).astype(BF16)
            pv = cache_gqa_v[:, i].reshape(n_past, -1).astype(BF16)
            pkvm = _rows_matmul(cache_mla_ckv[:, i].reshape(n_past, MLA_KV_RANK), wkvb, BF16, "mla_cache_kv")
            pkpe = jnp.pad(cache_mla_kpe[:, i].reshape(n_past, MLA_ROPE).astype(BF16),
                           ((0, 0), (0, LANES - MLA_ROPE)))
            o = _attn(qg, kg, vg, qmn, qmp, kvm, kpeb, pk, pv, pkvm, pkpe)
            x1, hff, aff = _attn_out(o, wout, x, modt[l], row(norm2_g[l]), wr)
            new_k.append(kf[:CTX_TOK].reshape(BATCH, SEQ, GQA_KV_HEADS, HEAD_DIM))
            new_v.append(vf[:CTX_TOK].reshape(BATCH, SEQ, GQA_KV_HEADS, HEAD_DIM))
            new_ckv.append(ckvf[:CTX_TOK].reshape(BATCH, SEQ, MLA_KV_RANK))
            new_kpe.append(kpef[:CTX_TOK, :MLA_ROPE].reshape(BATCH, SEQ, MLA_ROPE))
        else:
            ug, ut = _cs_pre(x, modt[l], row(norm1_g[l]), cs_w_in[i].astype(BF16))
            uc = _conv(ug, conv_w[i], row(conv_b[i]), row(conv_ln_g[i]), row(conv_ln_b[i]))
            yf, yr, cf, cr, ns = _s5_branch(ut, state_s5[:, i], s5_a_re[i], s5_a_im[i], s5_log_step[i],
                                            s5_b_re[i], s5_b_im[i], s5_c_re[i], s5_c_im[i], row(s5_d[i]))
            new_s5.append(ns)
            x1, hff, aff = _cs_post(uc, yf, yr, cf, cr, s5_w_glu[i].astype(BF16), row(s5_b_glu[i]),
                                    cs_w_out[i].astype(BF16), x, modt[l], row(norm2_g[l]), wr)
        x = _moe(l, l == DEPTH - 1, x1, hff, aff, modt[l], moe_w_gate, moe_w_up, moe_w_down, fg)
    y_prompt = x[0].reshape(BATCH, SEQ, D_MODEL)
    y_sample = x[1].reshape(DEC_BATCH, DEC_SEQ, D_MODEL)
    return (y_prompt, y_sample, jnp.stack(new_k, axis=1), jnp.stack(new_v, axis=1),
            jnp.stack(new_ckv, axis=1), jnp.stack(new_kpe, axis=1), jnp.stack(new_s5, axis=1))
```

```python
import functools
import math

import jax
import jax.numpy as jnp
import numpy as np
from jax import lax
from jax.experimental import pallas as pl
from jax.experimental.pallas import tpu as pltpu

F32 = jnp.float32
BF16 = jnp.bfloat16

D_MODEL = 2048
BATCH = 16
SEQ = 256
DEPTH = 4
DEC_BATCH = 2
DEC_SEQ = 1024
PAST_LEN = 512
GRID_W = 64
HEAD_DIM = 128
ROPE_THETA = 10000.0
EPS = 1e-6
GQA_HEADS = 8
GQA_KV_HEADS = 2
MLA_HEADS = 8
MLA_Q_RANK = 512
MLA_KV_RANK = 256
MLA_NOPE = 128
MLA_ROPE = 64
MLA_V = 128
CONV_DIM = D_MODEL // 2
CONV_WIDTH = 31
S5_DIM = D_MODEL // 2
S5_GROUP = 16
S5_GROUPS = S5_DIM // S5_GROUP
S5_STATE = 64
N_EXPERTS = 16
EXPERT_FF = 1024
EC_CAPACITY = 2

LANES = 128
VMEM_LIMIT = 56 * 1024 * 1024

TOK = 256
N_CTX_TILES = BATCH * SEQ // TOK
DEC_TILES_PER_REQ = DEC_SEQ // TOK
N_DEC_TILES = DEC_BATCH * DEC_TILES_PER_REQ
N_TILES = N_CTX_TILES + N_DEC_TILES
N_TOK = N_TILES * TOK
CTX_TOK = N_CTX_TILES * TOK
CAP_CTX = EC_CAPACITY * SEQ // N_EXPERTS
CAP_DEC = EC_CAPACITY * DEC_SEQ // N_EXPERTS
XS_CTX_ROWS = BATCH * CAP_CTX
XS_DEC_ROWS = DEC_BATCH * CAP_DEC

G_SCALE = HEAD_DIM ** -0.5
M_SCALE = (MLA_NOPE + MLA_ROPE) ** -0.5

S5_TT = 16
S5_NB = TOK // S5_TT
S5_Q = 4
S5_QC = S5_DIM // S5_Q
S5_QS = S5_GROUPS // S5_Q * S5_STATE
S5_STATE_COLS = S5_Q * 2 * S5_QS
S5_FIX_ROWS = 8

_NT = (((1,), (1,)), ((), ()))


def _cparams(n_grid_dims=1):
    return pltpu.CompilerParams(dimension_semantics=("arbitrary",) * n_grid_dims,
                                vmem_limit_bytes=VMEM_LIMIT)


def _const_spec(shape):
    nd = len(shape)
    return pl.BlockSpec(shape, lambda *_: (0,) * nd)


def _layer_spec(stacked, i):
    return pl.BlockSpec((None,) + stacked.shape[1:], lambda *_: (i, 0, 0))


def _rms(x, g):
    return x * lax.rsqrt(jnp.mean(x * x, axis=-1, keepdims=True) + EPS) * g


def _silu(x):
    return x * jax.nn.sigmoid(x)


def _dot(a, b):
    return jnp.dot(a, b, preferred_element_type=F32)


def _dot_nt(a, b):
    return lax.dot_general(a, b, _NT, preferred_element_type=F32)


def _ctx_tile(j):
    return jnp.minimum(j, N_CTX_TILES - 1)


def _dec_tile(j):
    return jnp.maximum(j - N_CTX_TILES, 0)


def _x_in(x):
    if isinstance(x, tuple):
        return ([pl.BlockSpec((TOK, D_MODEL), lambda j: (_ctx_tile(j), 0)),
                 pl.BlockSpec((TOK, D_MODEL), lambda j: (_dec_tile(j), 0))], list(x))
    return [pl.BlockSpec((TOK, D_MODEL), lambda j: (j, 0))], [x]


def _x_tile(x_refs):
    if len(x_refs) == 2:
        return jnp.where(pl.program_id(0) < N_CTX_TILES, x_refs[0][...], x_refs[1][...])
    return x_refs[0][...]


MOD_TN = 1024


def _mod_kernel(cond_ref, w_ref, b_ref, o_ref):
    a = _silu(cond_ref[...]).astype(BF16)
    o_ref[...] = _dot(a, w_ref[...].astype(BF16)) + b_ref[...]


def _modulation(cond8, w_mod, b_mod):
    n_out = 6 * D_MODEL
    return pl.pallas_call(
        _mod_kernel,
        grid=(DEPTH, n_out // MOD_TN),
        in_specs=[
            pl.BlockSpec((8, D_MODEL), lambda l, n: (0, 0)),
            pl.BlockSpec((None, D_MODEL, MOD_TN), lambda l, n: (l, 0, n)),
            pl.BlockSpec((None, 1, MOD_TN), lambda l, n: (l, 0, n)),
        ],
        out_specs=pl.BlockSpec((None, 8, MOD_TN), lambda l, n: (l, 0, n)),
        out_shape=jax.ShapeDtypeStruct((DEPTH, 8, n_out), F32),
        compiler_params=_cparams(2),
        name="modulation",
    )(cond8, w_mod, b_mod.reshape(DEPTH, 1, n_out))


def _attn_pre_kernel(nx, *refs):
    (mod_ref, g1_ref, win_ref, qn_ref, kn_ref, qan_ref, kvan_ref, wqb_ref, wkvb_ref, rope_ref,
     qg_ref, kg_ref, vg_ref, kf_ref, vf_ref, ckvf_ref, kpef_ref, qmn_ref, qmp_ref, kvm_ref, kpeb_ref) = refs[nx:]
    m = mod_ref[...]
    h = _rms(_x_tile(refs[:nx]), g1_ref[...]) * (1.0 + m[1:2]) + m[0:1]
    proj = _dot(h.astype(BF16), win_ref[...])
    rope = rope_ref[...]
    ga, gb, ma, mb, md = [rope[:, i * LANES:(i + 1) * LANES] for i in range(5)]

    def rope_g(xh):
        return xh * ga + pltpu.roll(xh, 64, 1) * gb

    def rope_m(xh):
        return xh * ma + pltpu.roll(xh, 96, 1) * mb + pltpu.roll(xh, 32, 1) * md

    for hh in range(GQA_HEADS):
        sl = slice(hh * HEAD_DIM, (hh + 1) * HEAD_DIM)
        qg_ref[:, sl] = rope_g(_rms(proj[:, sl], qn_ref[...])).astype(BF16)
    k0 = GQA_HEADS * HEAD_DIM
    for hh in range(GQA_KV_HEADS):
        sl = slice(hh * HEAD_DIM, (hh + 1) * HEAD_DIM)
        kh = rope_g(_rms(proj[:, k0 + hh * HEAD_DIM:k0 + (hh + 1) * HEAD_DIM], kn_ref[...]))
        kf_ref[:, sl] = kh
        kg_ref[:, sl] = kh.astype(BF16)
    v0 = k0 + GQA_KV_HEADS * HEAD_DIM
    v = proj[:, v0:v0 + GQA_KV_HEADS * HEAD_DIM]
    vf_ref[...] = v
    vg_ref[...] = v.astype(BF16)
    c0 = v0 + GQA_KV_HEADS * HEAD_DIM
    cq = _rms(proj[:, c0:c0 + MLA_Q_RANK], qan_ref[...])
    qm = _dot(cq.astype(BF16), wqb_ref[...])
    n_nope = MLA_HEADS * MLA_NOPE
    qmn_ref[...] = qm[:, :n_nope].astype(BF16)
    for hh in range(MLA_HEADS):
        sl = slice(hh * LANES, (hh + 1) * LANES)
        qmp_ref[:, sl] = rope_m(qm[:, n_nope + hh * LANES:n_nope + (hh + 1) * LANES]).astype(BF16)
    kv0 = c0 + MLA_Q_RANK
    ckv = _rms(proj[:, kv0:kv0 + MLA_KV_RANK], kvan_ref[...])
    ckvf_ref[...] = ckv
    kvm_ref[...] = _dot(ckv.astype(BF16), wkvb_ref[...]).astype(BF16)
    kpe = proj[:, kv0 + MLA_KV_RANK:kv0 + MLA_KV_RANK + MLA_ROPE]
    kpe = rope_m(jnp.concatenate([kpe, jnp.zeros((TOK, LANES - MLA_ROPE), F32)], axis=1))
    kpef_ref[...] = kpe
    kpeb_ref[...] = kpe.astype(BF16)


def _rope_tile(j):
    return jnp.where(j < N_CTX_TILES, 0, 1 + (j - N_CTX_TILES) % DEC_TILES_PER_REQ)


def _attn_pre(layer_i, x, modt, g1, win_all, qn, kn, qan, kvan, wqb, wkvb, rope_tab):
    tile = lambda w: pl.BlockSpec((TOK, w), lambda j: (j, 0))
    kvw = GQA_KV_HEADS * HEAD_DIM
    outs = [
        (GQA_HEADS * HEAD_DIM, BF16), (kvw, BF16), (kvw, BF16),
        (kvw, F32), (kvw, F32), (MLA_KV_RANK, F32), (LANES, F32),
        (MLA_HEADS * MLA_NOPE, BF16), (MLA_HEADS * LANES, BF16),
        (MLA_HEADS * (MLA_NOPE + MLA_V), BF16), (LANES, BF16),
    ]
    x_specs, x_args = _x_in(x)
    return pl.pallas_call(
        functools.partial(_attn_pre_kernel, len(x_args)),
        grid=(N_TILES,),
        in_specs=x_specs + [
            pl.BlockSpec((None, 6, D_MODEL), lambda j: (j, 0, 0)),
            _const_spec((1, D_MODEL)),
            _layer_spec(win_all, layer_i),
            _const_spec((1, HEAD_DIM)), _const_spec((1, HEAD_DIM)),
            _const_spec((1, MLA_Q_RANK)), _const_spec((1, MLA_KV_RANK)),
            _const_spec(wqb.shape), _const_spec(wkvb.shape),
            pl.BlockSpec((TOK, 5 * LANES), lambda j: (_rope_tile(j), 0)),
        ],
        out_specs=[tile(w) for w, _ in outs],
        out_shape=[jax.ShapeDtypeStruct((N_TOK, w), dt) for w, dt in outs],
        compiler_params=_cparams(),
        name="attn_pre",
    )(*x_args, modt, g1, win_all, qn, kn, qan, kvan, wqb, wkvb, rope_tab)


def _rows_matmul_kernel(a_ref, w_ref, o_ref):
    o_ref[...] = _dot(a_ref[...].astype(BF16), w_ref[...]).astype(o_ref.dtype)


def _rows_matmul(a, w, out_dtype, name):
    rows, k = a.shape
    n = w.shape[1]
    return pl.pallas_call(
        _rows_matmul_kernel,
        grid=(rows // TOK,),
        in_specs=[pl.BlockSpec((TOK, k), lambda i: (i, 0)), _const_spec(w.shape)],
        out_specs=pl.BlockSpec((TOK, n), lambda i: (i, 0)),
        out_shape=jax.ShapeDtypeStruct((rows, n), out_dtype),
        compiler_params=_cparams(),
        name=name,
    )(a, w)


def _attend(scores, values):
    m = scores[0].max(axis=-1, keepdims=True)
    for s in scores[1:]:
        m = jnp.maximum(m, s.max(axis=-1, keepdims=True))
    ps = [jnp.exp(s - m) for s in scores]
    l = ps[0].sum(axis=-1, keepdims=True)
    for p in ps[1:]:
        l = l + p.sum(axis=-1, keepdims=True)
    o = _dot(ps[0].astype(BF16), values[0])
    for p, v in zip(ps[1:], values[1:]):
        o = o + _dot(p.astype(BF16), v)
    return o / l


def _attn_heads(q_ref, qn_ref, qp_ref, segs, o_ref):
    for hh in range(GQA_HEADS):
        sl = slice(hh * HEAD_DIM, (hh + 1) * HEAD_DIM)
        kh = hh // (GQA_HEADS // GQA_KV_HEADS)
        ksl = slice(kh * HEAD_DIM, (kh + 1) * HEAD_DIM)
        q = q_ref[:, sl]
        scores = [_dot_nt(q, s[0][:, ksl]) * G_SCALE for s in segs]
        o_ref[:, sl] = _attend(scores, [s[1][:, ksl] for s in segs]).astype(BF16)
    o0 = GQA_HEADS * HEAD_DIM
    v0 = MLA_HEADS * MLA_NOPE
    for hh in range(MLA_HEADS):
        sl = slice(hh * LANES, (hh + 1) * LANES)
        qn = qn_ref[:, sl]
        qp = qp_ref[:, sl]
        scores = [(_dot_nt(qn, s[2][:, sl]) + _dot_nt(qp, s[3][...])) * M_SCALE for s in segs]
        vals = [s[2][:, v0 + hh * MLA_V:v0 + (hh + 1) * MLA_V] for s in segs]
        o_ref[:, o0 + hh * MLA_V:o0 + (hh + 1) * MLA_V] = _attend(scores, vals).astype(BF16)


def _attn_kernel(q_ref, qn_ref, qp_ref,
                 ck_ref, cv_ref, ckvm_ref, ckpe_ref,
                 dk_ref, dv_ref, dkvm_ref, dkpe_ref,
                 pk_ref, pv_ref, pkvm_ref, pkpe_ref,
                 o_ref):
    j = pl.program_id(0)

    @pl.when(j < N_CTX_TILES)
    def _():
        _attn_heads(q_ref, qn_ref, qp_ref, [(ck_ref, cv_ref, ckvm_ref, ckpe_ref)], o_ref)

    @pl.when(j >= N_CTX_TILES)
    def _():
        _attn_heads(q_ref, qn_ref, qp_ref,
                    [(dk_ref, dv_ref, dkvm_ref, dkpe_ref), (pk_ref, pv_ref, pkvm_ref, pkpe_ref)], o_ref)


def _dec_req(j):
    return jnp.maximum(j - N_CTX_TILES, 0) // DEC_TILES_PER_REQ


def _attn(qg, kg, vg, qmn, qmp, kvm, kpeb, pk, pv, pkvm, pkpe):
    tile = lambda w: pl.BlockSpec((TOK, w), lambda j: (j, 0))
    ctx = lambda w: pl.BlockSpec((TOK, w), lambda j: (jnp.minimum(j, N_CTX_TILES - 1), 0))
    dec = lambda w: pl.BlockSpec((DEC_SEQ, w), lambda j: (CTX_TOK // DEC_SEQ + _dec_req(j), 0))
    past = lambda w: pl.BlockSpec((PAST_LEN, w), lambda j: (_dec_req(j), 0))
    kvw = GQA_KV_HEADS * HEAD_DIM
    mw = MLA_HEADS * (MLA_NOPE + MLA_V)
    return pl.pallas_call(
        _attn_kernel,
        grid=(N_TILES,),
        in_specs=[tile(GQA_HEADS * HEAD_DIM), tile(MLA_HEADS * MLA_NOPE), tile(MLA_HEADS * LANES),
                  ctx(kvw), ctx(kvw), ctx(mw), ctx(LANES),
                  dec(kvw), dec(kvw), dec(mw), dec(LANES),
                  past(kvw), past(kvw), past(mw), past(LANES)],
        out_specs=tile(D_MODEL),
        out_shape=jax.ShapeDtypeStruct((N_TOK, D_MODEL), BF16),
        compiler_params=_cparams(),
        name="attention",
    )(qg, qmn, qmp, kg, vg, kvm, kpeb, kg, vg, kvm, kpeb, pk, pv, pkvm, pkpe)


def _residual_and_router(x, mix_out, m, g2_ref, wr_ref, x1_ref, hff_ref, aff_ref):
    x1 = x + m[2:3] * mix_out
    x1_ref[...] = x1
    hf = _rms(x1, g2_ref[...]) * (1.0 + m[4:5]) + m[3:4]
    hi = hf.astype(BF16)
    hff_ref[...] = hi
    lo = (hf - hi.astype(F32)).astype(BF16)
    r = _dot(hi, wr_ref[0]) + _dot(lo, wr_ref[1])
    logits = r + pltpu.roll(r, LANES - N_EXPERTS, 1)
    lane = lax.broadcasted_iota(jnp.int32, logits.shape, 1)
    logits = jnp.where(lane < N_EXPERTS, logits, -jnp.inf)
    e = jnp.exp(logits - logits.max(axis=-1, keepdims=True))
    aff_ref[...] = e / e.sum(axis=-1, keepdims=True)


def _epilogue_specs():
    tile = lambda w: pl.BlockSpec((TOK, w), lambda j: (j, 0))
    out_specs = [tile(D_MODEL), tile(D_MODEL), tile(LANES)]
    out_shape = [jax.ShapeDtypeStruct((N_TOK, D_MODEL), F32),
                 jax.ShapeDtypeStruct((N_TOK, D_MODEL), BF16),
                 jax.ShapeDtypeStruct((N_TOK, LANES), F32)]
    return out_specs, out_shape


def _attn_out_kernel(nx, *refs):
    o_ref, wout_ref, mod_ref, g2_ref, wr_ref, x1_ref, hff_ref, aff_ref = refs[nx:]
    out = _dot(o_ref[...], wout_ref[...])
    _residual_and_router(_x_tile(refs[:nx]), out, mod_ref[...], g2_ref, wr_ref, x1_ref, hff_ref, aff_ref)


def _attn_out(layer_i, o, wout, x, modt, g2, wr):
    tile = lambda w: pl.BlockSpec((TOK, w), lambda j: (j, 0))
    out_specs, out_shape = _epilogue_specs()
    x_specs, x_args = _x_in(x)
    return pl.pallas_call(
        functools.partial(_attn_out_kernel, len(x_args)),
        grid=(N_TILES,),
        in_specs=x_specs + [tile(D_MODEL), _layer_spec(wout, layer_i),
                            pl.BlockSpec((None, 6, D_MODEL), lambda j: (j, 0, 0)),
                            _const_spec((1, D_MODEL)), _const_spec(wr.shape)],
        out_specs=out_specs, out_shape=out_shape,
        compiler_params=_cparams(),
        name="attn_out",
    )(*x_args, o, wout, modt, g2, wr)


def _cs_pre_kernel(x_ref, mod_ref, g1_ref, win_ref, ug_ref, us_ref):
    m = mod_ref[...]
    h = _rms(x_ref[...], g1_ref[...]) * (1.0 + m[1:2]) + m[0:1]
    proj = _dot(h.astype(BF16), win_ref[...])
    ug_ref[...] = proj[:, :CONV_DIM] * jax.nn.sigmoid(proj[:, CONV_DIM:2 * CONV_DIM])
    us_ref[...] = proj[:, 2 * CONV_DIM:]


def _cs_pre(layer_i, x, modt, g1, win):
    return pl.pallas_call(
        _cs_pre_kernel,
        grid=(N_TILES,),
        in_specs=[pl.BlockSpec((TOK, D_MODEL), lambda j: (j, 0)),
                  pl.BlockSpec((None, 6, D_MODEL), lambda j: (j, 0, 0)),
                  _const_spec((1, D_MODEL)), _layer_spec(win, layer_i)],
        out_specs=[pl.BlockSpec((TOK, CONV_DIM), lambda j: (j, 0)),
                   pl.BlockSpec((TOK, S5_DIM), lambda j: (j, 0))],
        out_shape=[jax.ShapeDtypeStruct((N_TOK, CONV_DIM), F32),
                   jax.ShapeDtypeStruct((N_TOK, S5_DIM), F32)],
        compiler_params=_cparams(),
        name="cs_pre",
    )(x, modt, g1, win)


CONV_HALO = 16
SUBLANES = 8
CONV_BASE = CONV_HALO - CONV_WIDTH // 2
CONV_SH_ROWS = (CONV_BASE + CONV_WIDTH - 1) // SUBLANES * SUBLANES + TOK


def _conv_kernel(prev_ref, cur_ref, next_ref, w_ref, b_ref, lng_ref, lnb_ref, o_ref, pad_ref, sh_ref, u_ref):
    j = pl.program_id(0)
    q = (j - N_CTX_TILES) % DEC_TILES_PER_REQ
    has_prev = jnp.logical_and(j >= N_CTX_TILES, q != 0)
    has_next = jnp.logical_and(j >= N_CTX_TILES, q != DEC_TILES_PER_REQ - 1)
    pad_ref[0:CONV_HALO, :] = jnp.where(has_prev, prev_ref[TOK - CONV_HALO:, :], 0.0)
    pad_ref[CONV_HALO:CONV_HALO + TOK, :] = cur_ref[...]
    pad_ref[CONV_HALO + TOK:, :] = jnp.where(has_next, next_ref[0:CONV_HALO, :], 0.0)
    for s in range(1, SUBLANES):
        sh_ref[s - 1] = pad_ref[s:s + CONV_SH_ROWS, :]
    for lt in range(CONV_DIM // LANES):
        ls = slice(lt * LANES, (lt + 1) * LANES)
        acc = None
        for tap in range(CONV_WIDTH):
            a8, s = (CONV_BASE + tap) // SUBLANES * SUBLANES, (CONV_BASE + tap) % SUBLANES
            win = pad_ref[a8:a8 + TOK, ls] if s == 0 else sh_ref[s - 1, a8:a8 + TOK, ls]
            term = w_ref[tap:tap + 1, ls] * win
            acc = term if acc is None else acc + term
        u_ref[:, ls] = acc + b_ref[:, ls]
    u = u_ref[...]
    mu = jnp.mean(u, axis=-1, keepdims=True)
    uc = u - mu
    y = uc * lax.rsqrt(jnp.mean(uc * uc, axis=-1, keepdims=True) + EPS) * lng_ref[...] + lnb_ref[...]
    o_ref[...] = _silu(y).astype(BF16)


def _conv(ug, w, b, lng, lnb):
    nb = lambda d: pl.BlockSpec((TOK, CONV_DIM), lambda j: (jnp.clip(j + d, 0, N_TILES - 1), 0))
    return pl.pallas_call(
        _conv_kernel,
        grid=(N_TILES,),
        in_specs=[nb(-1), nb(0), nb(1), _const_spec(w.shape),
                  _const_spec((1, CONV_DIM)), _const_spec((1, CONV_DIM)), _const_spec((1, CONV_DIM))],
        out_specs=pl.BlockSpec((TOK, CONV_DIM), lambda j: (j, 0)),
        out_shape=jax.ShapeDtypeStruct((N_TOK, CONV_DIM), BF16),
        scratch_shapes=[pltpu.VMEM((TOK + 2 * CONV_HALO, CONV_DIM), F32),
                        pltpu.VMEM((SUBLANES - 1, CONV_SH_ROWS, CONV_DIM), F32),
                        pltpu.VMEM((TOK, CONV_DIM), F32)],
        compiler_params=_cparams(),
        name="conv_branch",
    )(ug, ug, ug, w, b, lng, lnb)


S5_PITCH = 24
S5_SLABS = S5_DIM // LANES
S5_SCAN_COLS = 256


def _to_time_major(blk_ref, slab_ref, rows):
    for s in range(S5_SLABS):
        for j in range(rows):
            slab_ref[s, j * S5_PITCH:j * S5_PITCH + S5_TT, :] = blk_ref[j, :, s * LANES:(s + 1) * LANES]
    steps = [jnp.concatenate([slab_ref[s, pl.ds(t, rows, stride=S5_PITCH), :] for s in range(S5_SLABS)], axis=1)
             for t in range(S5_TT)]
    return jnp.concatenate(steps, axis=0)


def _from_time_major(y, slab_ref, out_ref, rows, c0):
    n_slabs = y.shape[1] // LANES
    for t in range(S5_TT):
        for s in range(n_slabs):
            slab_ref[s, pl.ds(t, rows, stride=S5_PITCH), :] = y[t * rows:(t + 1) * rows, s * LANES:(s + 1) * LANES]
    for s in range(n_slabs):
        for j in range(rows):
            out_ref[j, :, c0 + s * LANES:c0 + (s + 1) * LANES] = slab_ref[s, j * S5_PITCH:j * S5_PITCH + S5_TT, :]


def _s5_scan_quarter(lam_ref, st_ref, bu, d, q, rows, reverse):
    ncb = S5_QS // S5_SCAN_COLS
    states = [[None] * (2 * ncb) for _ in range(S5_TT)]
    s0 = q * 2 * S5_QS
    for cb in range(ncb):
        c_re = slice(cb * S5_SCAN_COLS, (cb + 1) * S5_SCAN_COLS)
        c_im = slice(S5_QS + cb * S5_SCAN_COLS, S5_QS + (cb + 1) * S5_SCAN_COLS)
        st_re = slice(s0 + c_re.start, s0 + c_re.stop)
        st_im = slice(s0 + c_im.start, s0 + c_im.stop)
        lr = lam_ref[d, q, 0:1, c_re]
        li = lam_ref[d, q, 1:2, c_re]
        sr = st_ref[d, :, st_re]
        si = st_ref[d, :, st_im]
        for k in range(S5_TT):
            t = (S5_TT - 1 - k) if reverse else k
            nr = lr * sr - li * si
            ni = lr * si + li * sr
            if bu is not None:
                nr = nr + bu[t * rows:(t + 1) * rows, c_re]
                ni = ni + bu[t * rows:(t + 1) * rows, c_im]
            sr, si = nr, ni
            states[t][cb] = sr
            states[t][ncb + cb] = si
        st_ref[d, :, st_re] = sr
        st_ref[d, :, st_im] = si
    return jnp.concatenate([jnp.concatenate(row, axis=1) for row in states], axis=0)


def _s5_kernel(rows, has_input, *refs):
    if has_input:
        uf_ref, ur_ref, wb_ref, dsk_ref = refs[:4]
        refs = refs[4:]
    wc_ref, lam_ref, s0_ref, yf_ref, yr_ref, fin_ref, st_ref, uslab_ref, yslab_ref = refs
    i = pl.program_id(0)

    @pl.when(i == 0)
    def _():
        st_ref[...] = s0_ref[...]

    for d, (y_ref, reverse) in enumerate(((yf_ref, False), (yr_ref, True))):
        if has_input:
            u = _to_time_major(ur_ref if reverse else uf_ref, uslab_ref, rows)
            ub = u.astype(BF16)
        for q in range(S5_Q):
            csl = slice(q * S5_QC, (q + 1) * S5_QC)
            bu = _dot(ub[:, csl], wb_ref[d, q]) if has_input else None
            s_all = _s5_scan_quarter(lam_ref, st_ref, bu, d, q, rows, reverse)
            y = _dot(s_all.astype(BF16), wc_ref[d, q])
            if has_input and not reverse:
                y = y + dsk_ref[:, csl] * u[:, csl]
            _from_time_major(y, yslab_ref.at[q % 2], y_ref, rows, q * S5_QC)

    @pl.when(i == S5_NB - 1)
    def _():
        fin_ref[...] = st_ref[...]


def _s5(rows, u3, wb, wc, lam, s0, dskip):
    fwd = pl.BlockSpec((rows, S5_TT, S5_DIM), lambda i: (0, i, 0))
    rev = pl.BlockSpec((rows, S5_TT, S5_DIM), lambda i: (0, S5_NB - 1 - i, 0))
    has_input = u3 is not None
    in_specs, args = [], []
    if has_input:
        in_specs += [fwd, rev, _const_spec(wb.shape), _const_spec((1, S5_DIM))]
        args += [u3, u3, wb, dskip]
    in_specs += [_const_spec(wc.shape), _const_spec(lam.shape), _const_spec(s0.shape)]
    args += [wc, lam, s0]
    y_shape = jax.ShapeDtypeStruct((rows, TOK, S5_DIM), F32)
    return pl.pallas_call(
        functools.partial(_s5_kernel, rows, has_input),
        grid=(S5_NB,),
        in_specs=in_specs,
        out_specs=[fwd, rev, _const_spec(s0.shape)],
        out_shape=[y_shape, y_shape, jax.ShapeDtypeStruct(s0.shape, F32)],
        scratch_shapes=[pltpu.VMEM(s0.shape, F32),
                        pltpu.VMEM((S5_SLABS, rows * S5_PITCH, LANES), F32),
                        pltpu.VMEM((2, S5_QC // LANES, rows * S5_PITCH, LANES), F32)],
        compiler_params=_cparams(),
        name="s5_scan" if has_input else "s5_carry_fix",
    )(*args)


def _gelu_tanh(x):
    return x * (0.5 * (1.0 + jnp.tanh(math.sqrt(2.0 / math.pi) * (x + 0.044715 * (x * x * x)))))


def _cs_post_kernel(uc_ref, yf_ref, yr_ref, cf_ref, cr_ref, wglu_ref, bglu_ref, wout_ref, x_ref, mod_ref,
                    g2_ref, wr_ref, x1_ref, hff_ref, aff_ref):
    y = _gelu_tanh(yf_ref[...] + yr_ref[...] + cf_ref[...] + cr_ref[...])
    y = y * jax.nn.sigmoid(_dot(y.astype(BF16), wglu_ref[...]) + bglu_ref[...])
    out = _dot(uc_ref[...], wout_ref[0:CONV_DIM, :]) + _dot(y.astype(BF16), wout_ref[CONV_DIM:, :])
    _residual_and_router(x_ref[...], out, mod_ref[...], g2_ref, wr_ref, x1_ref, hff_ref, aff_ref)


def _fix_slot(j, reverse):
    d = j - N_CTX_TILES
    r, q = d // DEC_TILES_PER_REQ, d % DEC_TILES_PER_REQ
    if reverse:
        ok, slot = q != DEC_TILES_PER_REQ - 1, r * (DEC_TILES_PER_REQ - 1) + q
    else:
        ok, slot = q != 0, r * (DEC_TILES_PER_REQ - 1) + q - 1
    return jnp.where(jnp.logical_and(j >= N_CTX_TILES, ok), slot, S5_FIX_ROWS - 1)


def _cs_post(layer_i, uc, yf, yr, cf, cr, wglu, bglu, wout, x, modt, g2, wr):
    tile = lambda w: pl.BlockSpec((TOK, w), lambda j: (j, 0))
    out_specs, out_shape = _epilogue_specs()
    return pl.pallas_call(
        _cs_post_kernel,
        grid=(N_TILES,),
        in_specs=[tile(CONV_DIM), tile(S5_DIM), tile(S5_DIM),
                  pl.BlockSpec((TOK, S5_DIM), lambda j: (_fix_slot(j, False), 0)),
                  pl.BlockSpec((TOK, S5_DIM), lambda j: (_fix_slot(j, True), 0)),
                  _layer_spec(wglu, layer_i), _const_spec((1, S5_DIM)), _layer_spec(wout, layer_i),
                  tile(D_MODEL), pl.BlockSpec((None, 6, D_MODEL), lambda j: (j, 0, 0)),
                  _const_spec((1, D_MODEL)), _const_spec(wr.shape)],
        out_specs=out_specs, out_shape=out_shape,
        compiler_params=_cparams(),
        name="cs_post",
    )(uc, yf, yr, cf, cr, wglu, bglu, wout, x, modt, g2, wr)


RANK_CHUNK = 128
GATHER_ROWS = 512


def _moe_select(cap, aff_ref, hff_ref, xs_ref, gs_ref, g_ref, rank_ref, p_ref):
    n = aff_ref.shape[0]
    nblk = n // RANK_CHUNK
    a_t = aff_ref[...].T
    sub = lax.broadcasted_iota(jnp.int32, (RANK_CHUNK, RANK_CHUNK), 0)
    lan = lax.broadcasted_iota(jnp.int32, (RANK_CHUNK, RANK_CHUNK), 1)
    earlier = sub < lan
    slot = lax.broadcasted_iota(jnp.int32, (cap, n), 0).astype(F32)
    rank_ref[...] = jnp.zeros_like(rank_ref)
    for e in range(N_EXPERTS):
        row = a_t[e:e + 1, :]
        cols = [aff_ref[c * RANK_CHUNK:(c + 1) * RANK_CHUNK, e:e + 1] for c in range(nblk)]
        for b in range(nblk):
            rb = row[:, b * RANK_CHUNK:(b + 1) * RANK_CHUNK]
            cnt = None
            for c in range(nblk):
                if c < b:
                    part = jnp.where(cols[c] >= rb, 1.0, 0.0)
                elif c > b:
                    part = jnp.where(cols[c] > rb, 1.0, 0.0)
                else:
                    tie = jnp.where(earlier, cols[c], -1.0) == rb
                    part = jnp.where(cols[c] > rb, 1.0, 0.0) + jnp.where(tie, 1.0, 0.0)
                cnt = part if cnt is None else cnt + part
            rank_ref[e:e + 1, b * RANK_CHUNK:(b + 1) * RANK_CHUNK] = jnp.sum(cnt, axis=0, keepdims=True)
        rank = rank_ref[e:e + 1, :]
        onehot = slot == rank
        p_ref[e * cap:(e + 1) * cap, :] = jnp.where(onehot, 1.0, 0.0).astype(BF16)
        gs_ref[e] = jnp.sum(jnp.where(onehot, row, 0.0), axis=1, keepdims=True)
    grp = GATHER_ROWS // cap
    for e0 in range(0, N_EXPERTS, grp):
        xs = _dot(p_ref[e0 * cap:(e0 + grp) * cap, :], hff_ref[...])
        for k in range(grp):
            xs_ref[e0 + k] = xs[k * cap:(k + 1) * cap].astype(BF16)
    rank_t = rank_ref[...].T
    per = LANES // cap
    lane = lax.broadcasted_iota(jnp.int32, (n, LANES), 1)
    slot_lane = (lane & (cap - 1)).astype(F32)
    for blk in range(N_EXPERTS // per):
        rc = rank_t[:, blk * per:blk * per + 1]
        for k in range(1, per):
            rc = jnp.where(lane >= k * cap, rank_t[:, blk * per + k:blk * per + k + 1], rc)
        g_ref[:, blk * LANES:(blk + 1) * LANES] = jnp.where(slot_lane == rc, 1.0, 0.0).astype(BF16)


def _moe_gather_kernel(affc_ref, hffc_ref, affd_ref, hffd_ref,
                       xsc_ref, gsc_ref, gc_ref, xsd_ref, gsd_ref, gd_ref,
                       rankc_ref, pc_ref, rankd_ref, pd_ref):
    s = pl.program_id(0)

    @pl.when(s < BATCH)
    def _():
        _moe_select(CAP_CTX, affc_ref, hffc_ref, xsc_ref, gsc_ref, gc_ref, rankc_ref, pc_ref)

    @pl.when(s >= BATCH)
    def _():
        _moe_select(CAP_DEC, affd_ref, hffd_ref, xsd_ref, gsd_ref, gd_ref, rankd_ref, pd_ref)


def _moe_gather(aff, hff):
    cidx = lambda s: jnp.minimum(s, BATCH - 1)
    didx = lambda s: jnp.maximum(s - BATCH, 0)
    dec0 = CTX_TOK // DEC_SEQ
    return pl.pallas_call(
        _moe_gather_kernel,
        grid=(BATCH + DEC_BATCH,),
        in_specs=[pl.BlockSpec((SEQ, LANES), lambda s: (cidx(s), 0)),
                  pl.BlockSpec((SEQ, D_MODEL), lambda s: (cidx(s), 0)),
                  pl.BlockSpec((DEC_SEQ, LANES), lambda s: (dec0 + didx(s), 0)),
                  pl.BlockSpec((DEC_SEQ, D_MODEL), lambda s: (dec0 + didx(s), 0))],
        out_specs=[pl.BlockSpec((N_EXPERTS, CAP_CTX, D_MODEL), lambda s: (0, cidx(s), 0)),
                   pl.BlockSpec((N_EXPERTS, CAP_CTX, 1), lambda s: (0, cidx(s), 0)),
                   pl.BlockSpec((SEQ, N_EXPERTS * CAP_CTX), lambda s: (cidx(s), 0)),
                   pl.BlockSpec((N_EXPERTS, CAP_DEC, D_MODEL), lambda s: (0, didx(s), 0)),
                   pl.BlockSpec((N_EXPERTS, CAP_DEC, 1), lambda s: (0, didx(s), 0)),
                   pl.BlockSpec((DEC_SEQ, N_EXPERTS * CAP_DEC), lambda s: (didx(s), 0))],
        out_shape=[jax.ShapeDtypeStruct((N_EXPERTS, XS_CTX_ROWS, D_MODEL), BF16),
                   jax.ShapeDtypeStruct((N_EXPERTS, XS_CTX_ROWS, 1), F32),
                   jax.ShapeDtypeStruct((CTX_TOK, N_EXPERTS * CAP_CTX), BF16),
                   jax.ShapeDtypeStruct((N_EXPERTS, XS_DEC_ROWS, D_MODEL), BF16),
                   jax.ShapeDtypeStruct((N_EXPERTS, XS_DEC_ROWS, 1), F32),
                   jax.ShapeDtypeStruct((DEC_BATCH * DEC_SEQ, N_EXPERTS * CAP_DEC), BF16)],
        scratch_shapes=[pltpu.VMEM((LANES, SEQ), F32), pltpu.VMEM((N_EXPERTS * CAP_CTX, SEQ), BF16),
                        pltpu.VMEM((LANES, DEC_SEQ), F32), pltpu.VMEM((N_EXPERTS * CAP_DEC, DEC_SEQ), BF16)],
        compiler_params=_cparams(),
        name="moe_gather",
    )(aff, hff, aff, hff)


FF_CHUNK = 512
N_FF_CHUNKS = EXPERT_FF // FF_CHUNK


def _moe_ffn_kernel(xsc_ref, xsd_ref, gsc_ref, gsd_ref, wg_ref, wu_ref, wd_ref, yc_ref, yd_ref, acc_ref):
    f = pl.program_id(1)

    @pl.when(f == 0)
    def _():
        acc_ref[...] = jnp.zeros_like(acc_ref)

    wg = wg_ref[...].astype(BF16)
    wu = wu_ref[...].astype(BF16)
    wd = wd_ref[...].astype(BF16)
    for xs_ref, r0, rows in ((xsc_ref, 0, XS_CTX_ROWS), (xsd_ref, XS_CTX_ROWS, XS_DEC_ROWS)):
        xs = xs_ref[...]
        act = _silu(_dot(xs, wg)) * _dot(xs, wu)
        acc_ref[r0:r0 + rows, :] += _dot(act.astype(BF16), wd)

    @pl.when(f == N_FF_CHUNKS - 1)
    def _():
        yc_ref[...] = (acc_ref[0:XS_CTX_ROWS, :] * gsc_ref[...]).astype(BF16)
        yd_ref[...] = (acc_ref[XS_CTX_ROWS:, :] * gsd_ref[...]).astype(BF16)


def _moe_ffn(layer, xsc, xsd, gsc, gsd, w_gate, w_up, w_down):
    per_e = lambda rows, w: pl.BlockSpec((None, rows, w), lambda e, f: (e, 0, 0))
    return pl.pallas_call(
        _moe_ffn_kernel,
        grid=(N_EXPERTS, N_FF_CHUNKS),
        in_specs=[per_e(XS_CTX_ROWS, D_MODEL), per_e(XS_DEC_ROWS, D_MODEL),
                  per_e(XS_CTX_ROWS, 1), per_e(XS_DEC_ROWS, 1),
                  pl.BlockSpec((None, None, D_MODEL, FF_CHUNK), lambda e, f: (layer, e, 0, f)),
                  pl.BlockSpec((None, None, D_MODEL, FF_CHUNK), lambda e, f: (layer, e, 0, f)),
                  pl.BlockSpec((None, None, FF_CHUNK, D_MODEL), lambda e, f: (layer, e, f, 0))],
        out_specs=[per_e(XS_CTX_ROWS, D_MODEL), per_e(XS_DEC_ROWS, D_MODEL)],
        out_shape=[jax.ShapeDtypeStruct((N_EXPERTS, XS_CTX_ROWS, D_MODEL), BF16),
                   jax.ShapeDtypeStruct((N_EXPERTS, XS_DEC_ROWS, D_MODEL), BF16)],
        scratch_shapes=[pltpu.VMEM((XS_CTX_ROWS + XS_DEC_ROWS, D_MODEL), F32)],
        compiler_params=_cparams(2),
        name="moe_ffn",
    )(xsc, xsd, gsc, gsd, w_gate, w_up, w_down)


def _moe_combine_kernel(final, gc_ref, yc_ref, gd_ref, yd_ref, x1_ref, mod_ref, fg_ref, *o_refs):
    j = pl.program_id(0)

    def finish(comb, o_ref):
        x2 = x1_ref[...] + mod_ref[5:6, :] * comb
        o_ref[...] = _rms(x2, fg_ref[...]) if final else x2

    @pl.when(j < N_CTX_TILES)
    def _():
        ys = jnp.concatenate([yc_ref[e] for e in range(N_EXPERTS)], axis=0)
        finish(_dot(gc_ref[...], ys), o_refs[0])

    @pl.when(j >= N_CTX_TILES)
    def _():
        ys = jnp.concatenate([yd_ref[e] for e in range(N_EXPERTS)], axis=0)
        finish(_dot(gd_ref[...], ys), o_refs[-1])


def _moe_combine(final, gc, yc, gd, yd, x1, modt, fg):
    if final:
        out_specs = [pl.BlockSpec((TOK, D_MODEL), lambda j: (_ctx_tile(j), 0)),
                     pl.BlockSpec((TOK, D_MODEL), lambda j: (_dec_tile(j), 0))]
        out_shape = [jax.ShapeDtypeStruct((CTX_TOK, D_MODEL), F32),
                     jax.ShapeDtypeStruct((N_TOK - CTX_TOK, D_MODEL), F32)]
    else:
        out_specs = pl.BlockSpec((TOK, D_MODEL), lambda j: (j, 0))
        out_shape = jax.ShapeDtypeStruct((N_TOK, D_MODEL), F32)
    return pl.pallas_call(
        functools.partial(_moe_combine_kernel, final),
        grid=(N_TILES,),
        in_specs=[pl.BlockSpec((TOK, N_EXPERTS * CAP_CTX), lambda j: (_ctx_tile(j), 0)),
                  pl.BlockSpec((N_EXPERTS, CAP_CTX, D_MODEL), lambda j: (0, _ctx_tile(j), 0)),
                  pl.BlockSpec((TOK, N_EXPERTS * CAP_DEC), lambda j: (_dec_tile(j), 0)),
                  pl.BlockSpec((N_EXPERTS, CAP_DEC, D_MODEL), lambda j: (0, _dec_tile(j) // DEC_TILES_PER_REQ, 0)),
                  pl.BlockSpec((TOK, D_MODEL), lambda j: (j, 0)),
                  pl.BlockSpec((None, 6, D_MODEL), lambda j: (j, 0, 0)),
                  _const_spec((1, D_MODEL))],
        out_specs=out_specs, out_shape=out_shape,
        compiler_params=_cparams(),
        name="moe_combine",
    )(gc, yc, gd, yd, x1, modt, fg)


def _moe(layer, final, x1, hff, aff, modt, w_gate, w_up, w_down, fg):
    xsc, gsc, gc, xsd, gsd, gd = _moe_gather(aff, hff)
    yc, yd = _moe_ffn(layer, xsc, xsd, gsc, gsd, w_gate, w_up, w_down)
    return _moe_combine(final, gc, yc, gd, yd, x1, modt, fg)


def _axial_rope(n_tokens, rot_dim):
    rows = n_tokens // GRID_W
    per_axis = rot_dim // 4
    freqs = ROPE_THETA ** (-jnp.arange(per_axis, dtype=F32) / per_axis)
    row = jnp.repeat(jnp.arange(rows, dtype=F32), GRID_W)
    col = jnp.tile(jnp.arange(GRID_W, dtype=F32), rows)
    ang = jnp.concatenate([row[:, None] * freqs, col[:, None] * freqs], axis=-1)
    return jnp.cos(ang), jnp.sin(ang)


def _rope_table():
    cg, sg = _axial_rope(DEC_SEQ, HEAD_DIM)
    cm, sm = _axial_rope(DEC_SEQ, MLA_ROPE)
    z = jnp.zeros_like(cm)
    pos = jnp.concatenate([
        jnp.concatenate([cg, cg], -1), jnp.concatenate([-sg, sg], -1),
        jnp.concatenate([cm, cm, z, z], -1), jnp.concatenate([-sm, z, z, z], -1),
        jnp.concatenate([z, sm, z, z], -1)], axis=-1)
    one, zero = jnp.ones((TOK, LANES), F32), jnp.zeros((TOK, LANES), F32)
    ident = jnp.concatenate([one, zero, one, zero, zero], axis=-1)
    return jnp.concatenate([ident, pos], axis=0)


def _router_weights(w):
    hi = w.astype(BF16)
    lo = (w - hi.astype(F32)).astype(BF16)
    z = jnp.zeros((w.shape[0], LANES - 2 * N_EXPERTS), BF16)
    return jnp.stack([jnp.concatenate([hi, lo, z], axis=1),
                      jnp.concatenate([hi, jnp.zeros_like(lo), z], axis=1)])


def _mla_weights(w_qb, w_kvb):
    qb = w_qb.astype(BF16).reshape(MLA_Q_RANK, MLA_HEADS, MLA_NOPE + MLA_ROPE)
    qb_pe = jnp.pad(qb[:, :, MLA_NOPE:], ((0, 0), (0, 0), (0, LANES - MLA_ROPE)))
    wqb = jnp.concatenate([qb[:, :, :MLA_NOPE].reshape(MLA_Q_RANK, -1), qb_pe.reshape(MLA_Q_RANK, -1)], axis=1)
    kvb = w_kvb.astype(BF16).reshape(MLA_KV_RANK, MLA_HEADS, MLA_NOPE + MLA_V)
    wkvb = jnp.concatenate([kvb[:, :, :MLA_NOPE].reshape(MLA_KV_RANK, -1),
                            kvb[:, :, MLA_NOPE:].reshape(MLA_KV_RANK, -1)], axis=1)
    return wqb, wkvb


def _s5_weights(a_re, a_im, log_step, b_re, b_im, c_re, c_im):
    ar, ai = a_re.astype(F32), a_im.astype(F32)
    step = jnp.exp(log_step.astype(F32))[..., None]

    def cexp(k):
        mag = jnp.exp(k * ar * step)
        return mag * jnp.cos(k * ai * step), mag * jnp.sin(k * ai * step)

    lr, li = cexp(1.0)
    den = ar * ar + ai * ai
    gr = ((lr - 1.0) * ar + li * ai) / den
    gi = (li * ar - (lr - 1.0) * ai) / den
    br, bi = b_re.astype(F32), b_im.astype(F32)
    bbr = gr[..., None] * br - gi[..., None] * bi
    bbi = gr[..., None] * bi + gi[..., None] * br
    gq = S5_GROUPS // S5_Q
    group_of_col = (np.arange(2 * S5_QS) % S5_QS) // S5_STATE
    own = jnp.asarray(np.arange(gq)[:, None] == group_of_col[None, :])

    def rows_in(m):
        return m.reshape(2, S5_Q, gq, S5_STATE, S5_GROUP).transpose(0, 1, 4, 2, 3).reshape(2, S5_Q, S5_GROUP, S5_QS)

    def cols_out(m):
        return m.reshape(2, S5_Q, gq, S5_GROUP, S5_STATE).transpose(0, 1, 2, 4, 3).reshape(2, S5_Q, S5_QS, S5_GROUP)

    b_rows = jnp.concatenate([rows_in(bbr), rows_in(bbi)], axis=-1).astype(BF16)
    wb = jnp.where(own[None, None, :, None, :], b_rows[:, :, None, :, :], 0).reshape(2, S5_Q, S5_QC, 2 * S5_QS)
    c_cols = jnp.concatenate([cols_out(c_re.astype(F32)), cols_out(-c_im.astype(F32))], axis=-2).astype(BF16)
    wc = jnp.where(own.T[None, None, :, :, None], c_cols[:, :, :, None, :], 0).reshape(2, S5_Q, 2 * S5_QS, S5_QC)
    to_q = lambda v: v.reshape(2, S5_Q, S5_QS)
    lam_q = jnp.stack([to_q(lr), to_q(li)], axis=2)
    cr_, ci_ = cexp(float(TOK))
    lam_chunk = jnp.stack([to_q(cr_), to_q(ci_)], axis=2).reshape(2, S5_STATE_COLS)
    return wb, wc, lam_q, lam_chunk


def _cmul_cols(a, b):
    a4 = a.reshape(a.shape[:-1] + (S5_Q, 2, S5_QS))
    b4 = b.reshape(b.shape[:-1] + (S5_Q, 2, S5_QS))
    re = a4[..., 0, :] * b4[..., 0, :] - a4[..., 1, :] * b4[..., 1, :]
    im = a4[..., 0, :] * b4[..., 1, :] + a4[..., 1, :] * b4[..., 0, :]
    return jnp.stack([re, im], axis=-2).reshape(re.shape[:-2] + (S5_STATE_COLS,))


def _s5_branch(us, state_i, a_re, a_im, log_step, b_re, b_im, c_re, c_im, dskip):
    wb, wc, lam_q, lam_chunk = _s5_weights(a_re, a_im, log_step, b_re, b_im, c_re, c_im)
    h0 = state_i.reshape(DEC_BATCH, 2, 2, S5_Q, S5_QS).transpose(0, 1, 3, 2, 4).reshape(DEC_BATCH, 2, S5_STATE_COLS)
    s0 = jnp.zeros((2, N_TILES, S5_STATE_COLS), F32)
    first = N_CTX_TILES + DEC_TILES_PER_REQ * jnp.arange(DEC_BATCH)
    s0 = s0.at[0, first].set(h0[:, 0]).at[1, first + DEC_TILES_PER_REQ - 1].set(h0[:, 1])
    yf, yr, fin = _s5(N_TILES, us.reshape(N_TILES, TOK, S5_DIM), wb, wc, lam_q, s0, dskip)
    fz = fin[:, N_CTX_TILES:].reshape(2, DEC_BATCH, DEC_TILES_PER_REQ, S5_STATE_COLS)
    f1 = fz[0, :, 0]
    f2 = fz[0, :, 1] + _cmul_cols(lam_chunk[0], f1)
    f3 = fz[0, :, 2] + _cmul_cols(lam_chunk[0], f2)
    r2 = fz[1, :, 3]
    r1 = fz[1, :, 2] + _cmul_cols(lam_chunk[1], r2)
    r0 = fz[1, :, 1] + _cmul_cols(lam_chunk[1], r1)
    pad = jnp.zeros((S5_FIX_ROWS - DEC_BATCH * (DEC_TILES_PER_REQ - 1), S5_STATE_COLS), F32)
    sin_f = jnp.concatenate([jnp.stack([f1, f2, f3], 1).reshape(-1, S5_STATE_COLS), pad])
    sin_r = jnp.concatenate([jnp.stack([r0, r1, r2], 1).reshape(-1, S5_STATE_COLS), pad])
    cf, cr, _ = _s5(S5_FIX_ROWS, None, None, wc, lam_q, jnp.stack([sin_f, sin_r]), None)
    rows2d = lambda a: a.reshape(-1, S5_DIM)
    new_state = fin[:, :N_CTX_TILES].reshape(2, BATCH, S5_Q, 2, S5_QS).transpose(1, 0, 3, 2, 4)
    new_state = new_state.reshape(BATCH, 2, 2, S5_GROUPS, S5_STATE)
    return rows2d(yf), rows2d(yr), rows2d(cf), rows2d(cr), new_state


def kernel(x_prompt, x_sample, cache_gqa_k, cache_gqa_v, cache_mla_ckv, cache_mla_kpe, state_s5, c, c_ctx, w_mod, b_mod, norm1_g, norm2_g, attn_w_in, gqa_q_norm, gqa_k_norm, mla_qa_norm, mla_w_qb, mla_kva_norm, mla_w_kvb, attn_w_out, cs_w_in, conv_w, conv_b, conv_ln_g, conv_ln_b, s5_a_re, s5_a_im, s5_log_step, s5_b_re, s5_b_im, s5_c_re, s5_c_im, s5_d, s5_w_glu, s5_b_glu, cs_w_out, moe_router, moe_w_gate, moe_w_up, moe_w_down, final_norm_g):
    x = (x_prompt.reshape(CTX_TOK, D_MODEL), x_sample.reshape(DEC_BATCH * DEC_SEQ, D_MODEL))
    cond8 = jnp.concatenate([c_ctx[None, :], c, jnp.zeros((8 - 1 - DEC_BATCH, D_MODEL), F32)])
    mod = _modulation(cond8, w_mod, b_mod)
    tile_row = np.concatenate([np.zeros(N_CTX_TILES, np.int32),
                               1 + np.repeat(np.arange(DEC_BATCH, dtype=np.int32), DEC_TILES_PER_REQ)])
    modt = mod[:, tile_row].reshape(DEPTH, N_TILES, 6, D_MODEL)
    rope_tab = _rope_table()
    row = lambda v: v.reshape(1, -1)
    fg = row(final_norm_g)
    attn_w_in_b, attn_w_out_b = attn_w_in.astype(BF16), attn_w_out.astype(BF16)
    cs_w_in_b, cs_w_out_b, w_glu_b = cs_w_in.astype(BF16), cs_w_out.astype(BF16), s5_w_glu.astype(BF16)
    new_k, new_v, new_ckv, new_kpe, new_s5 = [], [], [], [], []
    for l in range(DEPTH):
        i = l // 2
        wr = _router_weights(moe_router[l])
        if l % 2 == 0:
            wqb, wkvb = _mla_weights(mla_w_qb[i], mla_w_kvb[i])
            (qg, kg, vg, kf, vf, ckvf, kpef, qmn, qmp, kvm, kpeb) = _attn_pre(
                i, x, modt[l], row(norm1_g[l]), attn_w_in_b, row(gqa_q_norm[i]), row(gqa_k_norm[i]),
                row(mla_qa_norm[i]), row(mla_kva_norm[i]), wqb, wkvb, rope_tab)
            n_past = DEC_BATCH * PAST_LEN
            pk = cache_gqa_k[:, i].reshape(n_past, -1).astype(BF16)
            pv = cache_gqa_v[:, i].reshape(n_past, -1).astype(BF16)
            pkvm = _rows_matmul(cache_mla_ckv[:, i].reshape(n_past, MLA_KV_RANK), wkvb, BF16, "mla_cache_kv")
            pkpe = jnp.pad(cache_mla_kpe[:, i].reshape(n_past, MLA_ROPE).astype(BF16),
                           ((0, 0), (0, LANES - MLA_ROPE)))
            o = _attn(qg, kg, vg, qmn, qmp, kvm, kpeb, pk, pv, pkvm, pkpe)
            x1, hff, aff = _attn_out(i, o, attn_w_out_b, x, modt[l], row(norm2_g[l]), wr)
            new_k.append(kf[:CTX_TOK].reshape(BATCH, SEQ, GQA_KV_HEADS, HEAD_DIM))
            new_v.append(vf[:CTX_TOK].reshape(BATCH, SEQ, GQA_KV_HEADS, HEAD_DIM))
            new_ckv.append(ckvf[:CTX_TOK].reshape(BATCH, SEQ, MLA_KV_RANK))
            new_kpe.append(kpef[:CTX_TOK, :MLA_ROPE].reshape(BATCH, SEQ, MLA_ROPE))
        else:
            ug, us = _cs_pre(i, x, modt[l], row(norm1_g[l]), cs_w_in_b)
            uc = _conv(ug, conv_w[i], row(conv_b[i]), row(conv_ln_g[i]), row(conv_ln_b[i]))
            yf, yr, cf, cr, ns = _s5_branch(us, state_s5[:, i], s5_a_re[i], s5_a_im[i], s5_log_step[i],
                                            s5_b_re[i], s5_b_im[i], s5_c_re[i], s5_c_im[i], row(s5_d[i]))
            new_s5.append(ns)
            x1, hff, aff = _cs_post(i, uc, yf, yr, cf, cr, w_glu_b, row(s5_b_glu[i]),
                                    cs_w_out_b, x, modt[l], row(norm2_g[l]), wr)
        x = _moe(l, l == DEPTH - 1, x1, hff, aff, modt[l], moe_w_gate, moe_w_up, moe_w_down, fg)
    y_prompt = x[0].reshape(BATCH, SEQ, D_MODEL)
    y_sample = x[1].reshape(DEC_BATCH, DEC_SEQ, D_MODEL)
    return (y_prompt, y_sample, jnp.stack(new_k, axis=1), jnp.stack(new_v, axis=1),
            jnp.stack(new_ckv, axis=1), jnp.stack(new_kpe, axis=1), jnp.stack(new_s5, axis=1))
```

```python
import functools
import math

import jax
import jax.numpy as jnp
import numpy as np
from jax import lax
from jax.experimental import pallas as pl
from jax.experimental.pallas import tpu as pltpu

F32 = jnp.float32
BF16 = jnp.bfloat16

D_MODEL = 2048
BATCH = 16
SEQ = 256
DEPTH = 4
DEC_BATCH = 2
DEC_SEQ = 1024
PAST_LEN = 512
GRID_W = 64
HEAD_DIM = 128
ROPE_THETA = 10000.0
EPS = 1e-6
GQA_HEADS = 8
GQA_KV_HEADS = 2
MLA_HEADS = 8
MLA_Q_RANK = 512
MLA_KV_RANK = 256
MLA_NOPE = 128
MLA_ROPE = 64
MLA_V = 128
CONV_DIM = D_MODEL // 2
CONV_WIDTH = 31
S5_DIM = D_MODEL // 2
S5_GROUP = 16
S5_GROUPS = S5_DIM // S5_GROUP
S5_STATE = 64
N_EXPERTS = 16
EXPERT_FF = 1024
EC_CAPACITY = 2

LANES = 128
VMEM_LIMIT = 56 * 1024 * 1024

TOK = 256
N_CTX_TILES = BATCH * SEQ // TOK
DEC_TILES_PER_REQ = DEC_SEQ // TOK
N_DEC_TILES = DEC_BATCH * DEC_TILES_PER_REQ
N_TILES = N_CTX_TILES + N_DEC_TILES
N_TOK = N_TILES * TOK
CTX_TOK = N_CTX_TILES * TOK
CAP_CTX = EC_CAPACITY * SEQ // N_EXPERTS
CAP_DEC = EC_CAPACITY * DEC_SEQ // N_EXPERTS
XS_CTX_ROWS = BATCH * CAP_CTX
XS_DEC_ROWS = DEC_BATCH * CAP_DEC

G_SCALE = HEAD_DIM ** -0.5
M_SCALE = (MLA_NOPE + MLA_ROPE) ** -0.5
LOG2_E = math.log2(math.e)

S5_TT = 16
S5_NB = TOK // S5_TT
S5_Q = 4
S5_QC = S5_DIM // S5_Q
S5_QS = S5_GROUPS // S5_Q * S5_STATE
S5_STATE_COLS = S5_Q * 2 * S5_QS
S5_FIX_ROWS = 8

_NT = (((1,), (1,)), ((), ()))


def _cparams(n_grid_dims=1):
    return pltpu.CompilerParams(dimension_semantics=("arbitrary",) * n_grid_dims,
                                vmem_limit_bytes=VMEM_LIMIT)


def _const_spec(shape):
    nd = len(shape)
    return pl.BlockSpec(shape, lambda *_: (0,) * nd)


def _layer_spec(stacked, i):
    return pl.BlockSpec((None,) + stacked.shape[1:], lambda *_: (i, 0, 0))


def _rms(x, g):
    return x * lax.rsqrt(jnp.mean(x * x, axis=-1, keepdims=True) + EPS) * g


def _silu(x):
    return x * jax.nn.sigmoid(x)


def _dot(a, b):
    return jnp.dot(a, b, preferred_element_type=F32)


def _dot_nt(a, b):
    return lax.dot_general(a, b, _NT, preferred_element_type=F32)


def _ctx_tile(j):
    return jnp.minimum(j, N_CTX_TILES - 1)


def _dec_tile(j):
    return jnp.maximum(j - N_CTX_TILES, 0)


def _x_in(x):
    if isinstance(x, tuple):
        return ([pl.BlockSpec((TOK, D_MODEL), lambda j: (_ctx_tile(j), 0)),
                 pl.BlockSpec((TOK, D_MODEL), lambda j: (_dec_tile(j), 0))], list(x))
    return [pl.BlockSpec((TOK, D_MODEL), lambda j: (j, 0))], [x]


def _x_tile(x_refs):
    if len(x_refs) == 2:
        return jnp.where(pl.program_id(0) < N_CTX_TILES, x_refs[0][...], x_refs[1][...])
    return x_refs[0][...]


MOD_TN = 1024


def _mod_kernel(cond_ref, w_ref, b_ref, o_ref):
    a = _silu(cond_ref[...]).astype(BF16)
    o_ref[...] = _dot(a, w_ref[...].astype(BF16)) + b_ref[...]


def _modulation(cond8, w_mod, b_mod):
    n_out = 6 * D_MODEL
    return pl.pallas_call(
        _mod_kernel,
        grid=(DEPTH, n_out // MOD_TN),
        in_specs=[
            pl.BlockSpec((8, D_MODEL), lambda l, n: (0, 0)),
            pl.BlockSpec((None, D_MODEL, MOD_TN), lambda l, n: (l, 0, n)),
            pl.BlockSpec((None, 1, MOD_TN), lambda l, n: (l, 0, n)),
        ],
        out_specs=pl.BlockSpec((None, 8, MOD_TN), lambda l, n: (l, 0, n)),
        out_shape=jax.ShapeDtypeStruct((DEPTH, 8, n_out), F32),
        compiler_params=_cparams(2),
        name="modulation",
    )(cond8, w_mod, b_mod.reshape(DEPTH, 1, n_out))


def _attn_pre_kernel(nx, *refs):
    (mod_ref, g1_ref, win_ref, qn_ref, kn_ref, qan_ref, kvan_ref, wqb_ref, wkvb_ref, rope_ref,
     qg_ref, kg_ref, vg_ref, kf_ref, vf_ref, ckvf_ref, kpef_ref, qmn_ref, qmp_ref, kvm_ref, kpeb_ref) = refs[nx:]
    m = mod_ref[...]
    h = _rms(_x_tile(refs[:nx]), g1_ref[...]) * (1.0 + m[1:2]) + m[0:1]
    proj = _dot(h.astype(BF16), win_ref[...])
    rope = rope_ref[...]
    ga, gb, ma, mb, md = [rope[:, i * LANES:(i + 1) * LANES] for i in range(5)]

    def rope_g(xh):
        return xh * ga + pltpu.roll(xh, 64, 1) * gb

    def rope_m(xh):
        return xh * ma + pltpu.roll(xh, 96, 1) * mb + pltpu.roll(xh, 32, 1) * md

    for hh in range(GQA_HEADS):
        sl = slice(hh * HEAD_DIM, (hh + 1) * HEAD_DIM)
        qg_ref[:, sl] = rope_g(_rms(proj[:, sl], qn_ref[...])).astype(BF16)
    k0 = GQA_HEADS * HEAD_DIM
    k = jnp.concatenate([rope_g(_rms(proj[:, k0 + hh * HEAD_DIM:k0 + (hh + 1) * HEAD_DIM], kn_ref[...]))
                         for hh in range(GQA_KV_HEADS)], axis=1)
    kg_ref[...] = k.astype(BF16)
    v0 = k0 + GQA_KV_HEADS * HEAD_DIM
    v = proj[:, v0:v0 + GQA_KV_HEADS * HEAD_DIM]
    vg_ref[...] = v.astype(BF16)
    c0 = v0 + GQA_KV_HEADS * HEAD_DIM
    cq = _rms(proj[:, c0:c0 + MLA_Q_RANK], qan_ref[...])
    qm = _dot(cq.astype(BF16), wqb_ref[...])
    n_nope = MLA_HEADS * MLA_NOPE
    qmn_ref[...] = qm[:, :n_nope].astype(BF16)
    for hh in range(MLA_HEADS):
        sl = slice(hh * LANES, (hh + 1) * LANES)
        qmp_ref[:, sl] = rope_m(qm[:, n_nope + hh * LANES:n_nope + (hh + 1) * LANES]).astype(BF16)
    kv0 = c0 + MLA_Q_RANK
    ckv = _rms(proj[:, kv0:kv0 + MLA_KV_RANK], kvan_ref[...])
    kvm_ref[...] = _dot(ckv.astype(BF16), wkvb_ref[...]).astype(BF16)
    kpe_raw = proj[:, kv0 + MLA_KV_RANK:kv0 + MLA_KV_RANK + MLA_ROPE]
    kpe = rope_m(jnp.concatenate([kpe_raw, jnp.zeros((TOK, LANES - MLA_ROPE), F32)], axis=1))
    kpeb_ref[...] = kpe.astype(BF16)

    @pl.when(pl.program_id(0) < N_CTX_TILES)
    def _():
        kf_ref[...] = k
        vf_ref[...] = v
        ckvf_ref[...] = ckv
        kpef_ref[...] = kpe_raw


def _rope_tile(j):
    return jnp.where(j < N_CTX_TILES, 0, 1 + (j - N_CTX_TILES) % DEC_TILES_PER_REQ)


def _attn_pre(layer_i, x, modt, g1, win_all, qn, kn, qan, kvan, wqb, wkvb, rope_tab):
    kvw = GQA_KV_HEADS * HEAD_DIM
    outs = [
        (GQA_HEADS * HEAD_DIM, BF16), (kvw, BF16), (kvw, BF16),
        (kvw, F32), (kvw, F32), (MLA_KV_RANK, F32), (MLA_ROPE, F32),
        (MLA_HEADS * MLA_NOPE, BF16), (MLA_HEADS * LANES, BF16),
        (MLA_HEADS * (MLA_NOPE + MLA_V), BF16), (LANES, BF16),
    ]
    out_specs = [pl.BlockSpec((TOK, w), (lambda j: (_ctx_tile(j), 0)) if dt == F32 else (lambda j: (j, 0)))
                 for w, dt in outs]
    out_shape = [jax.ShapeDtypeStruct((CTX_TOK if dt == F32 else N_TOK, w), dt) for w, dt in outs]
    x_specs, x_args = _x_in(x)
    return pl.pallas_call(
        functools.partial(_attn_pre_kernel, len(x_args)),
        grid=(N_TILES,),
        in_specs=x_specs + [
            pl.BlockSpec((None, 6, D_MODEL), lambda j: (j, 0, 0)),
            _const_spec((1, D_MODEL)),
            _layer_spec(win_all, layer_i),
            _const_spec((1, HEAD_DIM)), _const_spec((1, HEAD_DIM)),
            _const_spec((1, MLA_Q_RANK)), _const_spec((1, MLA_KV_RANK)),
            _const_spec(wqb.shape), _const_spec(wkvb.shape),
            pl.BlockSpec((TOK, 5 * LANES), lambda j: (_rope_tile(j), 0)),
        ],
        out_specs=out_specs, out_shape=out_shape,
        compiler_params=_cparams(),
        name="attn_pre",
    )(*x_args, modt, g1, win_all, qn, kn, qan, kvan, wqb, wkvb, rope_tab)


def _rows_matmul_kernel(a_ref, w_ref, o_ref):
    o_ref[...] = _dot(a_ref[...].astype(BF16), w_ref[...]).astype(o_ref.dtype)


def _rows_matmul(a, w, out_dtype, name):
    rows, k = a.shape
    n = w.shape[1]
    return pl.pallas_call(
        _rows_matmul_kernel,
        grid=(rows // TOK,),
        in_specs=[pl.BlockSpec((TOK, k), lambda i: (i, 0)), _const_spec(w.shape)],
        out_specs=pl.BlockSpec((TOK, n), lambda i: (i, 0)),
        out_shape=jax.ShapeDtypeStruct((rows, n), out_dtype),
        compiler_params=_cparams(),
        name=name,
    )(a, w)


def _attend(scores, values, scale):
    m = scores[0].max(axis=-1, keepdims=True)
    for s in scores[1:]:
        m = jnp.maximum(m, s.max(axis=-1, keepdims=True))
    ps = [jnp.exp2((s - m) * (scale * LOG2_E)) for s in scores]
    l = ps[0].sum(axis=-1, keepdims=True)
    for p in ps[1:]:
        l = l + p.sum(axis=-1, keepdims=True)
    o = _dot(ps[0].astype(BF16), values[0])
    for p, v in zip(ps[1:], values[1:]):
        o = o + _dot(p.astype(BF16), v)
    return o / l


def _attn_heads(q_ref, qn_ref, qp_ref, segs, o_ref):
    for hh in range(GQA_HEADS):
        sl = slice(hh * HEAD_DIM, (hh + 1) * HEAD_DIM)
        kh = hh // (GQA_HEADS // GQA_KV_HEADS)
        ksl = slice(kh * HEAD_DIM, (kh + 1) * HEAD_DIM)
        q = q_ref[:, sl]
        scores = [_dot_nt(q, s[0][:, ksl]) for s in segs]
        o_ref[:, sl] = _attend(scores, [s[1][:, ksl] for s in segs], G_SCALE).astype(BF16)
    o0 = GQA_HEADS * HEAD_DIM
    v0 = MLA_HEADS * MLA_NOPE
    for hh in range(MLA_HEADS):
        sl = slice(hh * LANES, (hh + 1) * LANES)
        q = jnp.concatenate([qn_ref[:, sl], qp_ref[:, sl]], axis=1)
        scores = [_dot_nt(q, jnp.concatenate([s[2][:, sl], s[3][...]], axis=1)) for s in segs]
        vals = [s[2][:, v0 + hh * MLA_V:v0 + (hh + 1) * MLA_V] for s in segs]
        o_ref[:, o0 + hh * MLA_V:o0 + (hh + 1) * MLA_V] = _attend(scores, vals, M_SCALE).astype(BF16)


def _attn_kernel(q_ref, qn_ref, qp_ref,
                 ck_ref, cv_ref, ckvm_ref, ckpe_ref,
                 dk_ref, dv_ref, dkvm_ref, dkpe_ref,
                 pk_ref, pv_ref, pkvm_ref, pkpe_ref,
                 o_ref):
    j = pl.program_id(0)

    @pl.when(j < N_CTX_TILES)
    def _():
        _attn_heads(q_ref, qn_ref, qp_ref, [(ck_ref, cv_ref, ckvm_ref, ckpe_ref)], o_ref)

    @pl.when(j >= N_CTX_TILES)
    def _():
        _attn_heads(q_ref, qn_ref, qp_ref,
                    [(dk_ref, dv_ref, dkvm_ref, dkpe_ref), (pk_ref, pv_ref, pkvm_ref, pkpe_ref)], o_ref)


def _dec_req(j):
    return jnp.maximum(j - N_CTX_TILES, 0) // DEC_TILES_PER_REQ


def _attn(qg, kg, vg, qmn, qmp, kvm, kpeb, pk, pv, pkvm, pkpe):
    tile = lambda w: pl.BlockSpec((TOK, w), lambda j: (j, 0))
    ctx = lambda w: pl.BlockSpec((TOK, w), lambda j: (jnp.minimum(j, N_CTX_TILES - 1), 0))
    dec = lambda w: pl.BlockSpec((DEC_SEQ, w), lambda j: (CTX_TOK // DEC_SEQ + _dec_req(j), 0))
    past = lambda w: pl.BlockSpec((PAST_LEN, w), lambda j: (_dec_req(j), 0))
    kvw = GQA_KV_HEADS * HEAD_DIM
    mw = MLA_HEADS * (MLA_NOPE + MLA_V)
    return pl.pallas_call(
        _attn_kernel,
        grid=(N_TILES,),
        in_specs=[tile(GQA_HEADS * HEAD_DIM), tile(MLA_HEADS * MLA_NOPE), tile(MLA_HEADS * LANES),
                  ctx(kvw), ctx(kvw), ctx(mw), ctx(LANES),
                  dec(kvw), dec(kvw), dec(mw), dec(LANES),
                  past(kvw), past(kvw), past(mw), past(LANES)],
        out_specs=tile(D_MODEL),
        out_shape=jax.ShapeDtypeStruct((N_TOK, D_MODEL), BF16),
        compiler_params=_cparams(),
        name="attention",
    )(qg, qmn, qmp, kg, vg, kvm, kpeb, kg, vg, kvm, kpeb, pk, pv, pkvm, pkpe)


def _residual_and_router(x, mix_out, m, g2_ref, wr_ref, x1_ref, hff_ref, aff_ref):
    x1 = x + m[2:3] * mix_out
    x1_ref[...] = x1
    hf = _rms(x1, g2_ref[...]) * (1.0 + m[4:5]) + m[3:4]
    hi = hf.astype(BF16)
    hff_ref[...] = hi
    lo = (hf - hi.astype(F32)).astype(BF16)
    r = _dot(hi, wr_ref[0]) + _dot(lo, wr_ref[1])
    logits = r + pltpu.roll(r, LANES - N_EXPERTS, 1)
    lane = lax.broadcasted_iota(jnp.int32, logits.shape, 1)
    logits = jnp.where(lane < N_EXPERTS, logits, -jnp.inf)
    e = jnp.exp(logits - logits.max(axis=-1, keepdims=True))
    aff_ref[...] = e / e.sum(axis=-1, keepdims=True)


def _epilogue_specs():
    tile = lambda w: pl.BlockSpec((TOK, w), lambda j: (j, 0))
    out_specs = [tile(D_MODEL), tile(D_MODEL), tile(LANES)]
    out_shape = [jax.ShapeDtypeStruct((N_TOK, D_MODEL), F32),
                 jax.ShapeDtypeStruct((N_TOK, D_MODEL), BF16),
                 jax.ShapeDtypeStruct((N_TOK, LANES), F32)]
    return out_specs, out_shape


def _attn_out_kernel(nx, *refs):
    o_ref, wout_ref, mod_ref, g2_ref, wr_ref, x1_ref, hff_ref, aff_ref = refs[nx:]
    out = _dot(o_ref[...], wout_ref[...])
    _residual_and_router(_x_tile(refs[:nx]), out, mod_ref[...], g2_ref, wr_ref, x1_ref, hff_ref, aff_ref)


def _attn_out(layer_i, o, wout, x, modt, g2, wr):
    tile = lambda w: pl.BlockSpec((TOK, w), lambda j: (j, 0))
    out_specs, out_shape = _epilogue_specs()
    x_specs, x_args = _x_in(x)
    return pl.pallas_call(
        functools.partial(_attn_out_kernel, len(x_args)),
        grid=(N_TILES,),
        in_specs=x_specs + [tile(D_MODEL), _layer_spec(wout, layer_i),
                            pl.BlockSpec((None, 6, D_MODEL), lambda j: (j, 0, 0)),
                            _const_spec((1, D_MODEL)), _const_spec(wr.shape)],
        out_specs=out_specs, out_shape=out_shape,
        compiler_params=_cparams(),
        name="attn_out",
    )(*x_args, o, wout, modt, g2, wr)


CONV_HALO = 16
SUBLANES = 8
CONV_BASE = CONV_HALO - CONV_WIDTH // 2
CONV_SH_ROWS = (CONV_BASE + CONV_WIDTH - 1) // SUBLANES * SUBLANES + TOK
CONV_RING = 4


def _cs_pre_kernel(x_ref, mod_ref, g1_ref, win_ref, w_ref, b_ref, lng_ref, lnb_ref, us_ref, o_ref,
                   ring_ref, pad_ref, sh_ref, u_ref):
    j = pl.program_id(0)
    slot = lambda d: (j + d) & (CONV_RING - 1)

    @pl.when(j == 0)
    def _():
        ring_ref[...] = jnp.zeros_like(ring_ref)

    @pl.when(j < N_TILES)
    def _():
        m = mod_ref[...]
        h = _rms(x_ref[...], g1_ref[...]) * (1.0 + m[1:2]) + m[0:1]
        proj = _dot(h.astype(BF16), win_ref[...])
        ring_ref[slot(0)] = proj[:, :CONV_DIM] * jax.nn.sigmoid(proj[:, CONV_DIM:2 * CONV_DIM])
        us_ref[...] = proj[:, 2 * CONV_DIM:]

    @pl.when(j >= 1)
    def _():
        c = j - 1
        q = (c - N_CTX_TILES) % DEC_TILES_PER_REQ
        has_prev = jnp.logical_and(c >= N_CTX_TILES, q != 0)
        has_next = jnp.logical_and(c >= N_CTX_TILES, q != DEC_TILES_PER_REQ - 1)
        pad_ref[0:CONV_HALO, :] = jnp.where(has_prev, ring_ref[slot(-2), TOK - CONV_HALO:, :], 0.0)
        pad_ref[CONV_HALO:CONV_HALO + TOK, :] = ring_ref[slot(-1)]
        pad_ref[CONV_HALO + TOK:, :] = jnp.where(has_next, ring_ref[slot(0), 0:CONV_HALO, :], 0.0)
        for s in range(1, SUBLANES):
            sh_ref[s - 1] = pad_ref[s:s + CONV_SH_ROWS, :]
        for lt in range(CONV_DIM // LANES):
            ls = slice(lt * LANES, (lt + 1) * LANES)
            acc = None
            for tap in range(CONV_WIDTH):
                a8, s = (CONV_BASE + tap) // SUBLANES * SUBLANES, (CONV_BASE + tap) % SUBLANES
                win = pad_ref[a8:a8 + TOK, ls] if s == 0 else sh_ref[s - 1, a8:a8 + TOK, ls]
                term = w_ref[tap:tap + 1, ls] * win
                acc = term if acc is None else acc + term
            u_ref[:, ls] = acc + b_ref[:, ls]
        u = u_ref[...]
        mu = jnp.mean(u, axis=-1, keepdims=True)
        uc = u - mu
        y = uc * lax.rsqrt(jnp.mean(uc * uc, axis=-1, keepdims=True) + EPS) * lng_ref[...] + lnb_ref[...]
        o_ref[...] = _silu(y).astype(BF16)


def _cs_pre(layer_i, x, modt, g1, win, w, b, lng, lnb):
    last = lambda j: jnp.minimum(j, N_TILES - 1)
    vec = _const_spec((1, CONV_DIM))
    return pl.pallas_call(
        _cs_pre_kernel,
        grid=(N_TILES + 1,),
        in_specs=[pl.BlockSpec((TOK, D_MODEL), lambda j: (last(j), 0)),
                  pl.BlockSpec((None, 6, D_MODEL), lambda j: (last(j), 0, 0)),
                  _const_spec((1, D_MODEL)), _layer_spec(win, layer_i),
                  _const_spec(w.shape), vec, vec, vec],
        out_specs=[pl.BlockSpec((TOK, S5_DIM), lambda j: (last(j), 0)),
                   pl.BlockSpec((TOK, CONV_DIM), lambda j: (jnp.maximum(j - 1, 0), 0))],
        out_shape=[jax.ShapeDtypeStruct((N_TOK, S5_DIM), F32),
                   jax.ShapeDtypeStruct((N_TOK, CONV_DIM), BF16)],
        scratch_shapes=[pltpu.VMEM((CONV_RING, TOK, CONV_DIM), F32),
                        pltpu.VMEM((TOK + 2 * CONV_HALO, CONV_DIM), F32),
                        pltpu.VMEM((SUBLANES - 1, CONV_SH_ROWS, CONV_DIM), F32),
                        pltpu.VMEM((TOK, CONV_DIM), F32)],
        compiler_params=_cparams(),
        name="cs_pre",
    )(x, modt, g1, win, w, b, lng, lnb)


S5_PITCH = 24
S5_SLABS = S5_DIM // LANES
S5_SCAN_COLS = 256


def _to_time_major(blk_ref, slab_ref, rows):
    for s in range(S5_SLABS):
        for j in range(rows):
            slab_ref[s, j * S5_PITCH:j * S5_PITCH + S5_TT, :] = blk_ref[j, :, s * LANES:(s + 1) * LANES]
    steps = [jnp.concatenate([slab_ref[s, pl.ds(t, rows, stride=S5_PITCH), :] for s in range(S5_SLABS)], axis=1)
             for t in range(S5_TT)]
    return jnp.concatenate(steps, axis=0)


def _from_time_major(y, slab_ref, out_ref, rows, c0):
    n_slabs = y.shape[1] // LANES
    for t in range(S5_TT):
        for s in range(n_slabs):
            slab_ref[s, pl.ds(t, rows, stride=S5_PITCH), :] = y[t * rows:(t + 1) * rows, s * LANES:(s + 1) * LANES]
    for s in range(n_slabs):
        for j in range(rows):
            out_ref[j, :, c0 + s * LANES:c0 + (s + 1) * LANES] = slab_ref[s, j * S5_PITCH:j * S5_PITCH + S5_TT, :]


def _s5_scan_quarter(lam_ref, st_ref, bu, d, q, rows, reverse):
    ncb = S5_QS // S5_SCAN_COLS
    states = [[None] * (2 * ncb) for _ in range(S5_TT)]
    s0 = q * 2 * S5_QS
    for cb in range(ncb):
        c_re = slice(cb * S5_SCAN_COLS, (cb + 1) * S5_SCAN_COLS)
        c_im = slice(S5_QS + cb * S5_SCAN_COLS, S5_QS + (cb + 1) * S5_SCAN_COLS)
        st_re = slice(s0 + c_re.start, s0 + c_re.stop)
        st_im = slice(s0 + c_im.start, s0 + c_im.stop)
        lr = lam_ref[d, q, 0:1, c_re]
        li = lam_ref[d, q, 1:2, c_re]
        sr = st_ref[d, :, st_re]
        si = st_ref[d, :, st_im]
        for k in range(S5_TT):
            t = (S5_TT - 1 - k) if reverse else k
            nr = lr * sr - li * si
            ni = lr * si + li * sr
            if bu is not None:
                nr = nr + bu[t * rows:(t + 1) * rows, c_re]
                ni = ni + bu[t * rows:(t + 1) * rows, c_im]
            sr, si = nr, ni
            states[t][cb] = sr
            states[t][ncb + cb] = si
        st_ref[d, :, st_re] = sr
        st_ref[d, :, st_im] = si
    return jnp.concatenate([jnp.concatenate(row, axis=1) for row in states], axis=0)


def _s5_kernel(rows, has_input, *refs):
    if has_input:
        uf_ref, ur_ref, wb_ref, dsk_ref = refs[:4]
        refs = refs[4:]
    wc_ref, lam_ref, s0_ref, yf_ref, yr_ref, fin_ref, st_ref, uslab_ref, yslab_ref = refs
    i = pl.program_id(0)

    @pl.when(i == 0)
    def _():
        st_ref[...] = s0_ref[...]

    for d, (y_ref, reverse) in enumerate(((yf_ref, False), (yr_ref, True))):
        if has_input:
            u = _to_time_major(ur_ref if reverse else uf_ref, uslab_ref, rows)
            ub = u.astype(BF16)
        for q in range(S5_Q):
            csl = slice(q * S5_QC, (q + 1) * S5_QC)
            bu = _dot(ub[:, csl], wb_ref[d, q]) if has_input else None
            s_all = _s5_scan_quarter(lam_ref, st_ref, bu, d, q, rows, reverse)
            y = _dot(s_all.astype(BF16), wc_ref[d, q])
            if has_input and not reverse:
                y = y + dsk_ref[:, csl] * u[:, csl]
            _from_time_major(y, yslab_ref.at[q % 2], y_ref, rows, q * S5_QC)

    @pl.when(i == S5_NB - 1)
    def _():
        fin_ref[...] = st_ref[...]


def _s5(rows, u3, wb, wc, lam, s0, dskip):
    fwd = pl.BlockSpec((rows, S5_TT, S5_DIM), lambda i: (0, i, 0))
    rev = pl.BlockSpec((rows, S5_TT, S5_DIM), lambda i: (0, S5_NB - 1 - i, 0))
    has_input = u3 is not None
    in_specs, args = [], []
    if has_input:
        in_specs += [fwd, rev, _const_spec(wb.shape), _const_spec((1, S5_DIM))]
        args += [u3, u3, wb, dskip]
    in_specs += [_const_spec(wc.shape), _const_spec(lam.shape), _const_spec(s0.shape)]
    args += [wc, lam, s0]
    y_shape = jax.ShapeDtypeStruct((rows, TOK, S5_DIM), F32)
    return pl.pallas_call(
        functools.partial(_s5_kernel, rows, has_input),
        grid=(S5_NB,),
        in_specs=in_specs,
        out_specs=[fwd, rev, _const_spec(s0.shape)],
        out_shape=[y_shape, y_shape, jax.ShapeDtypeStruct(s0.shape, F32)],
        scratch_shapes=[pltpu.VMEM(s0.shape, F32),
                        pltpu.VMEM((S5_SLABS, rows * S5_PITCH, LANES), F32),
                        pltpu.VMEM((2, S5_QC // LANES, rows * S5_PITCH, LANES), F32)],
        compiler_params=_cparams(),
        name="s5_scan" if has_input else "s5_carry_fix",
    )(*args)


def _gelu_tanh(x):
    return x * (0.5 * (1.0 + jnp.tanh(math.sqrt(2.0 / math.pi) * (x + 0.044715 * (x * x * x)))))


def _cs_post_kernel(uc_ref, yf_ref, yr_ref, cf_ref, cr_ref, wglu_ref, bglu_ref, wout_ref, x_ref, mod_ref,
                    g2_ref, wr_ref, x1_ref, hff_ref, aff_ref):
    y = _gelu_tanh(yf_ref[...] + yr_ref[...] + cf_ref[...] + cr_ref[...])
    y = y * jax.nn.sigmoid(_dot(y.astype(BF16), wglu_ref[...]) + bglu_ref[...])
    out = _dot(uc_ref[...], wout_ref[0:CONV_DIM, :]) + _dot(y.astype(BF16), wout_ref[CONV_DIM:, :])
    _residual_and_router(x_ref[...], out, mod_ref[...], g2_ref, wr_ref, x1_ref, hff_ref, aff_ref)


def _fix_slot(j, reverse):
    d = j - N_CTX_TILES
    r, q = d // DEC_TILES_PER_REQ, d % DEC_TILES_PER_REQ
    if reverse:
        ok, slot = q != DEC_TILES_PER_REQ - 1, r * (DEC_TILES_PER_REQ - 1) + q
    else:
        ok, slot = q != 0, r * (DEC_TILES_PER_REQ - 1) + q - 1
    return jnp.where(jnp.logical_and(j >= N_CTX_TILES, ok), slot, S5_FIX_ROWS - 1)


def _cs_post(layer_i, uc, yf, yr, cf, cr, wglu, bglu, wout, x, modt, g2, wr):
    tile = lambda w: pl.BlockSpec((TOK, w), lambda j: (j, 0))
    out_specs, out_shape = _epilogue_specs()
    return pl.pallas_call(
        _cs_post_kernel,
        grid=(N_TILES,),
        in_specs=[tile(CONV_DIM), tile(S5_DIM), tile(S5_DIM),
                  pl.BlockSpec((TOK, S5_DIM), lambda j: (_fix_slot(j, False), 0)),
                  pl.BlockSpec((TOK, S5_DIM), lambda j: (_fix_slot(j, True), 0)),
                  _layer_spec(wglu, layer_i), _const_spec((1, S5_DIM)), _layer_spec(wout, layer_i),
                  tile(D_MODEL), pl.BlockSpec((None, 6, D_MODEL), lambda j: (j, 0, 0)),
                  _const_spec((1, D_MODEL)), _const_spec(wr.shape)],
        out_specs=out_specs, out_shape=out_shape,
        compiler_params=_cparams(),
        name="cs_post",
    )(uc, yf, yr, cf, cr, wglu, bglu, wout, x, modt, g2, wr)


RANK_CHUNK = 128
GATHER_ROWS = 512


def _moe_select(cap, aff_ref, hff_ref, xs_ref, gs_ref, g_ref, rank_ref, p_ref):
    n = aff_ref.shape[0]
    nblk = n // RANK_CHUNK
    a_t = aff_ref[...].T
    sub = lax.broadcasted_iota(jnp.int32, (RANK_CHUNK, RANK_CHUNK), 0)
    lan = lax.broadcasted_iota(jnp.int32, (RANK_CHUNK, RANK_CHUNK), 1)
    earlier = sub < lan
    slot = lax.broadcasted_iota(jnp.int32, (cap, n), 0).astype(F32)
    rank_ref[...] = jnp.zeros_like(rank_ref)
    for e in range(N_EXPERTS):
        row = a_t[e:e + 1, :]
        cols = [aff_ref[c * RANK_CHUNK:(c + 1) * RANK_CHUNK, e:e + 1] for c in range(nblk)]
        for b in range(nblk):
            rb = row[:, b * RANK_CHUNK:(b + 1) * RANK_CHUNK]
            cnt = None
            for c in range(nblk):
                if c < b:
                    part = jnp.where(cols[c] >= rb, 1.0, 0.0)
                elif c > b:
                    part = jnp.where(cols[c] > rb, 1.0, 0.0)
                else:
                    tie = jnp.where(earlier, cols[c], -1.0) == rb
                    part = jnp.where(cols[c] > rb, 1.0, 0.0) + jnp.where(tie, 1.0, 0.0)
                cnt = part if cnt is None else cnt + part
            rank_ref[e:e + 1, b * RANK_CHUNK:(b + 1) * RANK_CHUNK] = jnp.sum(cnt, axis=0, keepdims=True)
        rank = rank_ref[e:e + 1, :]
        onehot = slot == rank
        p_ref[e * cap:(e + 1) * cap, :] = jnp.where(onehot, 1.0, 0.0).astype(BF16)
        gs_ref[e] = jnp.sum(jnp.where(onehot, row, 0.0), axis=1, keepdims=True)
    grp = GATHER_ROWS // cap
    for e0 in range(0, N_EXPERTS, grp):
        xs = _dot(p_ref[e0 * cap:(e0 + grp) * cap, :], hff_ref[...])
        for k in range(grp):
            xs_ref[e0 + k] = xs[k * cap:(k + 1) * cap].astype(BF16)
    rank_t = rank_ref[...].T
    per = LANES // cap
    lane = lax.broadcasted_iota(jnp.int32, (n, LANES), 1)
    slot_lane = (lane & (cap - 1)).astype(F32)
    for blk in range(N_EXPERTS // per):
        rc = rank_t[:, blk * per:blk * per + 1]
        for k in range(1, per):
            rc = jnp.where(lane >= k * cap, rank_t[:, blk * per + k:blk * per + k + 1], rc)
        g_ref[:, blk * LANES:(blk + 1) * LANES] = jnp.where(slot_lane == rc, 1.0, 0.0).astype(BF16)


def _moe_gather_kernel(affc_ref, hffc_ref, affd_ref, hffd_ref,
                       xsc_ref, gsc_ref, gc_ref, xsd_ref, gsd_ref, gd_ref,
                       rankc_ref, pc_ref, rankd_ref, pd_ref):
    s = pl.program_id(0)

    @pl.when(s < BATCH)
    def _():
        _moe_select(CAP_CTX, affc_ref, hffc_ref, xsc_ref, gsc_ref, gc_ref, rankc_ref, pc_ref)

    @pl.when(s >= BATCH)
    def _():
        _moe_select(CAP_DEC, affd_ref, hffd_ref, xsd_ref, gsd_ref, gd_ref, rankd_ref, pd_ref)


def _moe_gather(aff, hff):
    cidx = lambda s: jnp.minimum(s, BATCH - 1)
    didx = lambda s: jnp.maximum(s - BATCH, 0)
    dec0 = CTX_TOK // DEC_SEQ
    return pl.pallas_call(
        _moe_gather_kernel,
        grid=(BATCH + DEC_BATCH,),
        in_specs=[pl.BlockSpec((SEQ, LANES), lambda s: (cidx(s), 0)),
                  pl.BlockSpec((SEQ, D_MODEL), lambda s: (cidx(s), 0)),
                  pl.BlockSpec((DEC_SEQ, LANES), lambda s: (dec0 + didx(s), 0)),
                  pl.BlockSpec((DEC_SEQ, D_MODEL), lambda s: (dec0 + didx(s), 0))],
        out_specs=[pl.BlockSpec((N_EXPERTS, CAP_CTX, D_MODEL), lambda s: (0, cidx(s), 0)),
                   pl.BlockSpec((N_EXPERTS, CAP_CTX, 1), lambda s: (0, cidx(s), 0)),
                   pl.BlockSpec((SEQ, N_EXPERTS * CAP_CTX), lambda s: (cidx(s), 0)),
                   pl.BlockSpec((N_EXPERTS, CAP_DEC, D_MODEL), lambda s: (0, didx(s), 0)),
                   pl.BlockSpec((N_EXPERTS, CAP_DEC, 1), lambda s: (0, didx(s), 0)),
                   pl.BlockSpec((DEC_SEQ, N_EXPERTS * CAP_DEC), lambda s: (didx(s), 0))],
        out_shape=[jax.ShapeDtypeStruct((N_EXPERTS, XS_CTX_ROWS, D_MODEL), BF16),
                   jax.ShapeDtypeStruct((N_EXPERTS, XS_CTX_ROWS, 1), F32),
                   jax.ShapeDtypeStruct((CTX_TOK, N_EXPERTS * CAP_CTX), BF16),
                   jax.ShapeDtypeStruct((N_EXPERTS, XS_DEC_ROWS, D_MODEL), BF16),
                   jax.ShapeDtypeStruct((N_EXPERTS, XS_DEC_ROWS, 1), F32),
                   jax.ShapeDtypeStruct((DEC_BATCH * DEC_SEQ, N_EXPERTS * CAP_DEC), BF16)],
        scratch_shapes=[pltpu.VMEM((LANES, SEQ), F32), pltpu.VMEM((N_EXPERTS * CAP_CTX, SEQ), BF16),
                        pltpu.VMEM((LANES, DEC_SEQ), F32), pltpu.VMEM((N_EXPERTS * CAP_DEC, DEC_SEQ), BF16)],
        compiler_params=_cparams(),
        name="moe_gather",
    )(aff, hff, aff, hff)


FF_CHUNK = 512
N_FF_CHUNKS = EXPERT_FF // FF_CHUNK


def _moe_ffn_kernel(xsc_ref, xsd_ref, gsc_ref, gsd_ref, wg_ref, wu_ref, wd_ref, yc_ref, yd_ref, acc_ref):
    f = pl.program_id(1)

    @pl.when(f == 0)
    def _():
        acc_ref[...] = jnp.zeros_like(acc_ref)

    wg = wg_ref[...].astype(BF16)
    wu = wu_ref[...].astype(BF16)
    wd = wd_ref[...].astype(BF16)
    for xs_ref, r0, rows in ((xsc_ref, 0, XS_CTX_ROWS), (xsd_ref, XS_CTX_ROWS, XS_DEC_ROWS)):
        xs = xs_ref[...]
        act = _silu(_dot(xs, wg)) * _dot(xs, wu)
        acc_ref[r0:r0 + rows, :] += _dot(act.astype(BF16), wd)

    @pl.when(f == N_FF_CHUNKS - 1)
    def _():
        for r in range(BATCH):
            rows = slice(r * CAP_CTX, (r + 1) * CAP_CTX)
            yc_ref[r] = (acc_ref[rows, :] * gsc_ref[rows, :]).astype(BF16)
        for r in range(DEC_BATCH):
            rows = slice(r * CAP_DEC, (r + 1) * CAP_DEC)
            acc_rows = slice(XS_CTX_ROWS + rows.start, XS_CTX_ROWS + rows.stop)
            yd_ref[r] = (acc_ref[acc_rows, :] * gsd_ref[rows, :]).astype(BF16)


def _moe_ffn(layer, xsc, xsd, gsc, gsd, w_gate, w_up, w_down):
    per_e = lambda rows, w: pl.BlockSpec((None, rows, w), lambda e, f: (e, 0, 0))
    return pl.pallas_call(
        _moe_ffn_kernel,
        grid=(N_EXPERTS, N_FF_CHUNKS),
        in_specs=[per_e(XS_CTX_ROWS, D_MODEL), per_e(XS_DEC_ROWS, D_MODEL),
                  per_e(XS_CTX_ROWS, 1), per_e(XS_DEC_ROWS, 1),
                  pl.BlockSpec((None, None, D_MODEL, FF_CHUNK), lambda e, f: (layer, e, 0, f)),
                  pl.BlockSpec((None, None, D_MODEL, FF_CHUNK), lambda e, f: (layer, e, 0, f)),
                  pl.BlockSpec((None, None, FF_CHUNK, D_MODEL), lambda e, f: (layer, e, f, 0))],
        out_specs=[pl.BlockSpec((BATCH, CAP_CTX, D_MODEL), lambda e, f: (0, e, 0)),
                   pl.BlockSpec((DEC_BATCH, CAP_DEC, D_MODEL), lambda e, f: (0, e, 0))],
        out_shape=[jax.ShapeDtypeStruct((BATCH, N_EXPERTS * CAP_CTX, D_MODEL), BF16),
                   jax.ShapeDtypeStruct((DEC_BATCH, N_EXPERTS * CAP_DEC, D_MODEL), BF16)],
        scratch_shapes=[pltpu.VMEM((XS_CTX_ROWS + XS_DEC_ROWS, D_MODEL), F32)],
        compiler_params=_cparams(2),
        name="moe_ffn",
    )(xsc, xsd, gsc, gsd, w_gate, w_up, w_down)


def _moe_combine_kernel(final, gc_ref, yc_ref, gd_ref, yd_ref, x1_ref, mod_ref, fg_ref, *o_refs):
    j = pl.program_id(0)

    def finish(comb, o_ref):
        x2 = x1_ref[...] + mod_ref[5:6, :] * comb
        o_ref[...] = _rms(x2, fg_ref[...]) if final else x2

    @pl.when(j < N_CTX_TILES)
    def _():
        finish(_dot(gc_ref[...], yc_ref[...]), o_refs[0])

    @pl.when(j >= N_CTX_TILES)
    def _():
        finish(_dot(gd_ref[...], yd_ref[...]), o_refs[-1])


def _moe_combine(final, gc, yc, gd, yd, x1, modt, fg):
    if final:
        out_specs = [pl.BlockSpec((TOK, D_MODEL), lambda j: (_ctx_tile(j), 0)),
                     pl.BlockSpec((TOK, D_MODEL), lambda j: (_dec_tile(j), 0))]
        out_shape = [jax.ShapeDtypeStruct((CTX_TOK, D_MODEL), F32),
                     jax.ShapeDtypeStruct((N_TOK - CTX_TOK, D_MODEL), F32)]
    else:
        out_specs = pl.BlockSpec((TOK, D_MODEL), lambda j: (j, 0))
        out_shape = jax.ShapeDtypeStruct((N_TOK, D_MODEL), F32)
    return pl.pallas_call(
        functools.partial(_moe_combine_kernel, final),
        grid=(N_TILES,),
        in_specs=[pl.BlockSpec((TOK, N_EXPERTS * CAP_CTX), lambda j: (_ctx_tile(j), 0)),
                  pl.BlockSpec((None, N_EXPERTS * CAP_CTX, D_MODEL), lambda j: (_ctx_tile(j), 0, 0)),
                  pl.BlockSpec((TOK, N_EXPERTS * CAP_DEC), lambda j: (_dec_tile(j), 0)),
                  pl.BlockSpec((None, N_EXPERTS * CAP_DEC, D_MODEL),
                               lambda j: (_dec_tile(j) // DEC_TILES_PER_REQ, 0, 0)),
                  pl.BlockSpec((TOK, D_MODEL), lambda j: (j, 0)),
                  pl.BlockSpec((None, 6, D_MODEL), lambda j: (j, 0, 0)),
                  _const_spec((1, D_MODEL))],
        out_specs=out_specs, out_shape=out_shape,
        compiler_params=_cparams(),
        name="moe_combine",
    )(gc, yc, gd, yd, x1, modt, fg)


def _moe(layer, final, x1, hff, aff, modt, w_gate, w_up, w_down, fg):
    xsc, gsc, gc, xsd, gsd, gd = _moe_gather(aff, hff)
    yc, yd = _moe_ffn(layer, xsc, xsd, gsc, gsd, w_gate, w_up, w_down)
    return _moe_combine(final, gc, yc, gd, yd, x1, modt, fg)


def _axial_rope(n_tokens, rot_dim):
    rows = n_tokens // GRID_W
    per_axis = rot_dim // 4
    freqs = ROPE_THETA ** (-jnp.arange(per_axis, dtype=F32) / per_axis)
    row = jnp.repeat(jnp.arange(rows, dtype=F32), GRID_W)
    col = jnp.tile(jnp.arange(GRID_W, dtype=F32), rows)
    ang = jnp.concatenate([row[:, None] * freqs, col[:, None] * freqs], axis=-1)
    return jnp.cos(ang), jnp.sin(ang)


def _rope_table():
    cg, sg = _axial_rope(DEC_SEQ, HEAD_DIM)
    cm, sm = _axial_rope(DEC_SEQ, MLA_ROPE)
    z = jnp.zeros_like(cm)
    pos = jnp.concatenate([
        jnp.concatenate([cg, cg], -1), jnp.concatenate([-sg, sg], -1),
        jnp.concatenate([cm, cm, z, z], -1), jnp.concatenate([-sm, z, z, z], -1),
        jnp.concatenate([z, sm, z, z], -1)], axis=-1)
    one, zero = jnp.ones((TOK, LANES), F32), jnp.zeros((TOK, LANES), F32)
    ident = jnp.concatenate([one, zero, one, zero, zero], axis=-1)
    return jnp.concatenate([ident, pos], axis=0)


def _router_weights(w):
    hi = w.astype(BF16)
    lo = (w - hi.astype(F32)).astype(BF16)
    z = jnp.zeros((w.shape[0], LANES - 2 * N_EXPERTS), BF16)
    return jnp.stack([jnp.concatenate([hi, lo, z], axis=1),
                      jnp.concatenate([hi, jnp.zeros_like(lo), z], axis=1)])


def _mla_weights(w_qb, w_kvb):
    qb = w_qb.astype(BF16).reshape(MLA_Q_RANK, MLA_HEADS, MLA_NOPE + MLA_ROPE)
    qb_pe = jnp.pad(qb[:, :, MLA_NOPE:], ((0, 0), (0, 0), (0, LANES - MLA_ROPE)))
    wqb = jnp.concatenate([qb[:, :, :MLA_NOPE].reshape(MLA_Q_RANK, -1), qb_pe.reshape(MLA_Q_RANK, -1)], axis=1)
    kvb = w_kvb.astype(BF16).reshape(MLA_KV_RANK, MLA_HEADS, MLA_NOPE + MLA_V)
    wkvb = jnp.concatenate([kvb[:, :, :MLA_NOPE].reshape(MLA_KV_RANK, -1),
                            kvb[:, :, MLA_NOPE:].reshape(MLA_KV_RANK, -1)], axis=1)
    return wqb, wkvb


def _s5_weights(a_re, a_im, log_step, b_re, b_im, c_re, c_im):
    ar, ai = a_re.astype(F32), a_im.astype(F32)
    step = jnp.exp(log_step.astype(F32))[..., None]

    def cexp(k):
        mag = jnp.exp(k * ar * step)
        return mag * jnp.cos(k * ai * step), mag * jnp.sin(k * ai * step)

    lr, li = cexp(1.0)
    den = ar * ar + ai * ai
    gr = ((lr - 1.0) * ar + li * ai) / den
    gi = (li * ar - (lr - 1.0) * ai) / den
    br, bi = b_re.astype(F32), b_im.astype(F32)
    bbr = gr[..., None] * br - gi[..., None] * bi
    bbi = gr[..., None] * bi + gi[..., None] * br
    gq = S5_GROUPS // S5_Q
    group_of_col = (np.arange(2 * S5_QS) % S5_QS) // S5_STATE
    own = jnp.asarray(np.arange(gq)[:, None] == group_of_col[None, :])

    def rows_in(m):
        return m.reshape(2, S5_Q, gq, S5_STATE, S5_GROUP).transpose(0, 1, 4, 2, 3).reshape(2, S5_Q, S5_GROUP, S5_QS)

    def cols_out(m):
        return m.reshape(2, S5_Q, gq, S5_GROUP, S5_STATE).transpose(0, 1, 2, 4, 3).reshape(2, S5_Q, S5_QS, S5_GROUP)

    b_rows = jnp.concatenate([rows_in(bbr), rows_in(bbi)], axis=-1).astype(BF16)
    wb = jnp.where(own[None, None, :, None, :], b_rows[:, :, None, :, :], 0).reshape(2, S5_Q, S5_QC, 2 * S5_QS)
    c_cols = jnp.concatenate([cols_out(c_re.astype(F32)), cols_out(-c_im.astype(F32))], axis=-2).astype(BF16)
    wc = jnp.where(own.T[None, None, :, :, None], c_cols[:, :, :, None, :], 0).reshape(2, S5_Q, 2 * S5_QS, S5_QC)
    to_q = lambda v: v.reshape(2, S5_Q, S5_QS)
    lam_q = jnp.stack([to_q(lr), to_q(li)], axis=2)
    cr_, ci_ = cexp(float(TOK))
    lam_chunk = jnp.stack([to_q(cr_), to_q(ci_)], axis=2).reshape(2, S5_STATE_COLS)
    return wb, wc, lam_q, lam_chunk


def _cmul_cols(a, b):
    a4 = a.reshape(a.shape[:-1] + (S5_Q, 2, S5_QS))
    b4 = b.reshape(b.shape[:-1] + (S5_Q, 2, S5_QS))
    re = a4[..., 0, :] * b4[..., 0, :] - a4[..., 1, :] * b4[..., 1, :]
    im = a4[..., 0, :] * b4[..., 1, :] + a4[..., 1, :] * b4[..., 0, :]
    return jnp.stack([re, im], axis=-2).reshape(re.shape[:-2] + (S5_STATE_COLS,))


def _s5_branch(us, state_i, a_re, a_im, log_step, b_re, b_im, c_re, c_im, dskip):
    wb, wc, lam_q, lam_chunk = _s5_weights(a_re, a_im, log_step, b_re, b_im, c_re, c_im)
    h0 = state_i.reshape(DEC_BATCH, 2, 2, S5_Q, S5_QS).transpose(0, 1, 3, 2, 4).reshape(DEC_BATCH, 2, S5_STATE_COLS)
    s0 = jnp.zeros((2, N_TILES, S5_STATE_COLS), F32)
    first = N_CTX_TILES + DEC_TILES_PER_REQ * jnp.arange(DEC_BATCH)
    s0 = s0.at[0, first].set(h0[:, 0]).at[1, first + DEC_TILES_PER_REQ - 1].set(h0[:, 1])
    yf, yr, fin = _s5(N_TILES, us.reshape(N_TILES, TOK, S5_DIM), wb, wc, lam_q, s0, dskip)
    fz = fin[:, N_CTX_TILES:].reshape(2, DEC_BATCH, DEC_TILES_PER_REQ, S5_STATE_COLS)
    f1 = fz[0, :, 0]
    f2 = fz[0, :, 1] + _cmul_cols(lam_chunk[0], f1)
    f3 = fz[0, :, 2] + _cmul_cols(lam_chunk[0], f2)
    r2 = fz[1, :, 3]
    r1 = fz[1, :, 2] + _cmul_cols(lam_chunk[1], r2)
    r0 = fz[1, :, 1] + _cmul_cols(lam_chunk[1], r1)
    pad = jnp.zeros((S5_FIX_ROWS - DEC_BATCH * (DEC_TILES_PER_REQ - 1), S5_STATE_COLS), F32)
    sin_f = jnp.concatenate([jnp.stack([f1, f2, f3], 1).reshape(-1, S5_STATE_COLS), pad])
    sin_r = jnp.concatenate([jnp.stack([r0, r1, r2], 1).reshape(-1, S5_STATE_COLS), pad])
    cf, cr, _ = _s5(S5_FIX_ROWS, None, None, wc, lam_q, jnp.stack([sin_f, sin_r]), None)
    rows2d = lambda a: a.reshape(-1, S5_DIM)
    new_state = fin[:, :N_CTX_TILES].reshape(2, BATCH, S5_Q, 2, S5_QS).transpose(1, 0, 3, 2, 4)
    new_state = new_state.reshape(BATCH, 2, 2, S5_GROUPS, S5_STATE)
    return rows2d(yf), rows2d(yr), rows2d(cf), rows2d(cr), new_state


def kernel(x_prompt, x_sample, cache_gqa_k, cache_gqa_v, cache_mla_ckv, cache_mla_kpe, state_s5, c, c_ctx, w_mod, b_mod, norm1_g, norm2_g, attn_w_in, gqa_q_norm, gqa_k_norm, mla_qa_norm, mla_w_qb, mla_kva_norm, mla_w_kvb, attn_w_out, cs_w_in, conv_w, conv_b, conv_ln_g, conv_ln_b, s5_a_re, s5_a_im, s5_log_step, s5_b_re, s5_b_im, s5_c_re, s5_c_im, s5_d, s5_w_glu, s5_b_glu, cs_w_out, moe_router, moe_w_gate, moe_w_up, moe_w_down, final_norm_g):
    x = (x_prompt.reshape(CTX_TOK, D_MODEL), x_sample.reshape(DEC_BATCH * DEC_SEQ, D_MODEL))
    cond8 = jnp.concatenate([c_ctx[None, :], c, jnp.zeros((8 - 1 - DEC_BATCH, D_MODEL), F32)])
    mod = _modulation(cond8, w_mod, b_mod)
    tile_row = np.concatenate([np.zeros(N_CTX_TILES, np.int32),
                               1 + np.repeat(np.arange(DEC_BATCH, dtype=np.int32), DEC_TILES_PER_REQ)])
    modt = mod[:, tile_row].reshape(DEPTH, N_TILES, 6, D_MODEL)
    rope_tab = _rope_table()
    row = lambda v: v.reshape(1, -1)
    fg = row(final_norm_g)
    attn_w_in_b, attn_w_out_b = attn_w_in.astype(BF16), attn_w_out.astype(BF16)
    cs_w_in_b, cs_w_out_b, w_glu_b = cs_w_in.astype(BF16), cs_w_out.astype(BF16), s5_w_glu.astype(BF16)
    new_k, new_v, new_ckv, new_kpe, new_s5 = [], [], [], [], []
    for l in range(DEPTH):
        i = l // 2
        wr = _router_weights(moe_router[l])
        if l % 2 == 0:
            wqb, wkvb = _mla_weights(mla_w_qb[i], mla_w_kvb[i])
            (qg, kg, vg, kf, vf, ckvf, kpef, qmn, qmp, kvm, kpeb) = _attn_pre(
                i, x, modt[l], row(norm1_g[l]), attn_w_in_b, row(gqa_q_norm[i]), row(gqa_k_norm[i]),
                row(mla_qa_norm[i]), row(mla_kva_norm[i]), wqb, wkvb, rope_tab)
            n_past = DEC_BATCH * PAST_LEN
            pk = cache_gqa_k[:, i].reshape(n_past, -1).astype(BF16)
            pv = cache_gqa_v[:, i].reshape(n_past, -1).astype(BF16)
            pkvm = _rows_matmul(cache_mla_ckv[:, i].reshape(n_past, MLA_KV_RANK), wkvb, BF16, "mla_cache_kv")
            pkpe = jnp.pad(cache_mla_kpe[:, i].reshape(n_past, MLA_ROPE).astype(BF16),
                           ((0, 0), (0, LANES - MLA_ROPE)))
            o = _attn(qg, kg, vg, qmn, qmp, kvm, kpeb, pk, pv, pkvm, pkpe)
            x1, hff, aff = _attn_out(i, o, attn_w_out_b, x, modt[l], row(norm2_g[l]), wr)
            new_k.append(kf.reshape(BATCH, SEQ, GQA_KV_HEADS, HEAD_DIM))
            new_v.append(vf.reshape(BATCH, SEQ, GQA_KV_HEADS, HEAD_DIM))
            new_ckv.append(ckvf.reshape(BATCH, SEQ, MLA_KV_RANK))
            new_kpe.append(kpef.reshape(BATCH, SEQ, MLA_ROPE))
        else:
            us, uc = _cs_pre(i, x, modt[l], row(norm1_g[l]), cs_w_in_b,
                             conv_w[i], row(conv_b[i]), row(conv_ln_g[i]), row(conv_ln_b[i]))
            yf, yr, cf, cr, ns = _s5_branch(us, state_s5[:, i], s5_a_re[i], s5_a_im[i], s5_log_step[i],
                                            s5_b_re[i], s5_b_im[i], s5_c_re[i], s5_c_im[i], row(s5_d[i]))
            new_s5.append(ns)
            x1, hff, aff = _cs_post(i, uc, yf, yr, cf, cr, w_glu_b, row(s5_b_glu[i]),
                                    cs_w_out_b, x, modt[l], row(norm2_g[l]), wr)
        x = _moe(l, l == DEPTH - 1, x1, hff, aff, modt[l], moe_w_gate, moe_w_up, moe_w_down, fg)
    y_prompt = x[0].reshape(BATCH, SEQ, D_MODEL)
    y_sample = x[1].reshape(DEC_BATCH, DEC_SEQ, D_MODEL)
    return (y_prompt, y_sample, jnp.stack(new_k, axis=1), jnp.stack(new_v, axis=1),
            jnp.stack(new_ckv, axis=1), jnp.stack(new_kpe, axis=1), jnp.stack(new_s5, axis=1))
```

```python
import functools
import math

import jax
import jax.numpy as jnp
import numpy as np
from jax import lax
from jax.experimental import pallas as pl
from jax.experimental.pallas import tpu as pltpu

F32 = jnp.float32
BF16 = jnp.bfloat16

D_MODEL = 2048
BATCH = 16
SEQ = 256
DEPTH = 4
DEC_BATCH = 2
DEC_SEQ = 1024
PAST_LEN = 512
GRID_W = 64
HEAD_DIM = 128
ROPE_THETA = 10000.0
EPS = 1e-6
GQA_HEADS = 8
GQA_KV_HEADS = 2
MLA_HEADS = 8
MLA_Q_RANK = 512
MLA_KV_RANK = 256
MLA_NOPE = 128
MLA_ROPE = 64
MLA_V = 128
CONV_DIM = D_MODEL // 2
CONV_WIDTH = 31
S5_DIM = D_MODEL // 2
S5_GROUP = 16
S5_GROUPS = S5_DIM // S5_GROUP
S5_STATE = 64
N_EXPERTS = 16
EXPERT_FF = 1024
EC_CAPACITY = 2

LANES = 128
VMEM_LIMIT = 56 * 1024 * 1024

TOK = 256
N_CTX_TILES = BATCH * SEQ // TOK
DEC_TILES_PER_REQ = DEC_SEQ // TOK
N_DEC_TILES = DEC_BATCH * DEC_TILES_PER_REQ
N_TILES = N_CTX_TILES + N_DEC_TILES
N_TOK = N_TILES * TOK
CTX_TOK = N_CTX_TILES * TOK
CAP_CTX = EC_CAPACITY * SEQ // N_EXPERTS
CAP_DEC = EC_CAPACITY * DEC_SEQ // N_EXPERTS
XS_CTX_ROWS = BATCH * CAP_CTX
XS_DEC_ROWS = DEC_BATCH * CAP_DEC

G_SCALE = HEAD_DIM ** -0.5
M_SCALE = (MLA_NOPE + MLA_ROPE) ** -0.5
LOG2_E = math.log2(math.e)

S5_TT = 16
S5_NB = TOK // S5_TT
S5_Q = 4
S5_QC = S5_DIM // S5_Q
S5_QS = S5_GROUPS // S5_Q * S5_STATE
S5_STATE_COLS = S5_Q * 2 * S5_QS
S5_FIX_ROWS = 8

_NT = (((1,), (1,)), ((), ()))


def _cparams(n_grid_dims=1):
    return pltpu.CompilerParams(dimension_semantics=("arbitrary",) * n_grid_dims,
                                vmem_limit_bytes=VMEM_LIMIT)


def _const_spec(shape):
    nd = len(shape)
    return pl.BlockSpec(shape, lambda *_: (0,) * nd)


def _sel(stacked, *idx):
    rest = stacked.shape[len(idx):]
    return stacked, pl.BlockSpec((None,) * len(idx) + rest, lambda *_: tuple(idx) + (0,) * len(rest))


def _mod_operand(modt_all, layer):
    return modt_all, pl.BlockSpec((None, None, 6, D_MODEL),
                                  lambda j: (layer, jnp.minimum(j, N_TILES - 1), 0, 0))


def _rms(x, g):
    return x * lax.rsqrt(jnp.mean(x * x, axis=-1, keepdims=True) + EPS) * g


def _silu(x):
    return x * jax.nn.sigmoid(x)


def _dot(a, b):
    return jnp.dot(a, b, preferred_element_type=F32)


def _dot_nt(a, b):
    return lax.dot_general(a, b, _NT, preferred_element_type=F32)


def _ctx_tile(j):
    return jnp.minimum(j, N_CTX_TILES - 1)


def _dec_tile(j):
    return jnp.maximum(j - N_CTX_TILES, 0)


def _x_in(x):
    if isinstance(x, tuple):
        return ([pl.BlockSpec((TOK, D_MODEL), lambda j: (_ctx_tile(j), 0)),
                 pl.BlockSpec((TOK, D_MODEL), lambda j: (_dec_tile(j), 0))], list(x))
    return [pl.BlockSpec((TOK, D_MODEL), lambda j: (j, 0))], [x]


def _x_tile(x_refs):
    if len(x_refs) == 2:
        return jnp.where(pl.program_id(0) < N_CTX_TILES, x_refs[0][...], x_refs[1][...])
    return x_refs[0][...]


MOD_TN = 1024


def _mod_kernel(cond_ref, w_ref, b_ref, o_ref):
    a = _silu(cond_ref[...]).astype(BF16)
    o_ref[...] = _dot(a, w_ref[...].astype(BF16)) + b_ref[...]


def _modulation(cond8, w_mod, b_mod):
    n_out = 6 * D_MODEL
    return pl.pallas_call(
        _mod_kernel,
        grid=(DEPTH, n_out // MOD_TN),
        in_specs=[
            pl.BlockSpec((8, D_MODEL), lambda l, n: (0, 0)),
            pl.BlockSpec((None, D_MODEL, MOD_TN), lambda l, n: (l, 0, n)),
            pl.BlockSpec((None, 1, MOD_TN), lambda l, n: (l, 0, n)),
        ],
        out_specs=pl.BlockSpec((None, 8, MOD_TN), lambda l, n: (l, 0, n)),
        out_shape=jax.ShapeDtypeStruct((DEPTH, 8, n_out), F32),
        compiler_params=_cparams(2),
        name="modulation",
    )(cond8, w_mod, b_mod.reshape(DEPTH, 1, n_out))


def _attn_pre_kernel(nx, *refs):
    (mod_ref, g1_ref, win_ref, qn_ref, kn_ref, qan_ref, kvan_ref, wqb_ref, wkvb_ref, rope_ref,
     qg_ref, kg_ref, vg_ref, kf_ref, vf_ref, ckvf_ref, kpef_ref, qmn_ref, qmp_ref, kvm_ref, kpeb_ref) = refs[nx:]
    m = mod_ref[...]
    h = _rms(_x_tile(refs[:nx]), g1_ref[...]) * (1.0 + m[1:2]) + m[0:1]
    proj = _dot(h.astype(BF16), win_ref[...])
    rope = rope_ref[...]
    ga, gb, ma, mb, md = [rope[:, i * LANES:(i + 1) * LANES] for i in range(5)]

    def rope_g(xh):
        return xh * ga + pltpu.roll(xh, 64, 1) * gb

    def rope_m(xh):
        return xh * ma + pltpu.roll(xh, 96, 1) * mb + pltpu.roll(xh, 32, 1) * md

    for hh in range(GQA_HEADS):
        sl = slice(hh * HEAD_DIM, (hh + 1) * HEAD_DIM)
        qg_ref[:, sl] = rope_g(_rms(proj[:, sl], qn_ref[...])).astype(BF16)
    k0 = GQA_HEADS * HEAD_DIM
    k = jnp.concatenate([rope_g(_rms(proj[:, k0 + hh * HEAD_DIM:k0 + (hh + 1) * HEAD_DIM], kn_ref[...]))
                         for hh in range(GQA_KV_HEADS)], axis=1)
    kg_ref[...] = k.astype(BF16)
    v0 = k0 + GQA_KV_HEADS * HEAD_DIM
    v = proj[:, v0:v0 + GQA_KV_HEADS * HEAD_DIM]
    vg_ref[...] = v.astype(BF16)
    c0 = v0 + GQA_KV_HEADS * HEAD_DIM
    cq = _rms(proj[:, c0:c0 + MLA_Q_RANK], qan_ref[...])
    qm = _dot(cq.astype(BF16), wqb_ref[...])
    n_nope = MLA_HEADS * MLA_NOPE
    qmn_ref[...] = qm[:, :n_nope].astype(BF16)
    for hh in range(MLA_HEADS):
        sl = slice(hh * LANES, (hh + 1) * LANES)
        qmp_ref[:, sl] = rope_m(qm[:, n_nope + hh * LANES:n_nope + (hh + 1) * LANES]).astype(BF16)
    kv0 = c0 + MLA_Q_RANK
    ckv = _rms(proj[:, kv0:kv0 + MLA_KV_RANK], kvan_ref[...])
    kvm_ref[...] = _dot(ckv.astype(BF16), wkvb_ref[...]).astype(BF16)
    kpe_raw = proj[:, kv0 + MLA_KV_RANK:kv0 + MLA_KV_RANK + MLA_ROPE]
    kpe = rope_m(jnp.concatenate([kpe_raw, jnp.zeros((TOK, LANES - MLA_ROPE), F32)], axis=1))
    kpeb_ref[...] = kpe.astype(BF16)

    @pl.when(pl.program_id(0) < N_CTX_TILES)
    def _():
        kf_ref[...] = k
        vf_ref[...] = v
        ckvf_ref[...] = ckv
        kpef_ref[...] = kpe_raw


def _rope_tile(j):
    return jnp.where(j < N_CTX_TILES, 0, 1 + (j - N_CTX_TILES) % DEC_TILES_PER_REQ)


def _attn_pre(x, mod, consts, rope_tab):
    kvw = GQA_KV_HEADS * HEAD_DIM
    outs = [
        (GQA_HEADS * HEAD_DIM, BF16), (kvw, BF16), (kvw, BF16),
        (kvw, F32), (kvw, F32), (MLA_KV_RANK, F32), (MLA_ROPE, F32),
        (MLA_HEADS * MLA_NOPE, BF16), (MLA_HEADS * LANES, BF16),
        (MLA_HEADS * (MLA_NOPE + MLA_V), BF16), (LANES, BF16),
    ]
    out_specs = [pl.BlockSpec((TOK, w), (lambda j: (_ctx_tile(j), 0)) if dt == F32 else (lambda j: (j, 0)))
                 for w, dt in outs]
    out_shape = [jax.ShapeDtypeStruct((CTX_TOK if dt == F32 else N_TOK, w), dt) for w, dt in outs]
    x_specs, x_args = _x_in(x)
    return pl.pallas_call(
        functools.partial(_attn_pre_kernel, len(x_args)),
        grid=(N_TILES,),
        in_specs=x_specs + [mod[1]] + [s for _, s in consts]
        + [pl.BlockSpec((TOK, 5 * LANES), lambda j: (_rope_tile(j), 0))],
        out_specs=out_specs, out_shape=out_shape,
        compiler_params=_cparams(),
        name="attn_pre",
    )(*x_args, mod[0], *[a for a, _ in consts], rope_tab)


def _rows_matmul_kernel(a_ref, w_ref, o_ref):
    o_ref[...] = _dot(a_ref[...].astype(BF16), w_ref[...]).astype(o_ref.dtype)


def _mla_cache_kv(cache_ckv, layer_i, wkvb):
    per_req = PAST_LEN // TOK
    n = wkvb[0].shape[-1]
    return pl.pallas_call(
        _rows_matmul_kernel,
        grid=(DEC_BATCH * per_req,),
        in_specs=[pl.BlockSpec((None, None, TOK, MLA_KV_RANK), lambda g: (g // per_req, layer_i, g % per_req, 0)),
                  wkvb[1]],
        out_specs=pl.BlockSpec((TOK, n), lambda g: (g, 0)),
        out_shape=jax.ShapeDtypeStruct((DEC_BATCH * PAST_LEN, n), BF16),
        compiler_params=_cparams(),
        name="mla_cache_kv",
    )(cache_ckv, wkvb[0])


def _attend(scores, values, scale):
    m = scores[0].max(axis=-1, keepdims=True)
    for s in scores[1:]:
        m = jnp.maximum(m, s.max(axis=-1, keepdims=True))
    ps = [jnp.exp2((s - m) * (scale * LOG2_E)) for s in scores]
    l = ps[0].sum(axis=-1, keepdims=True)
    for p in ps[1:]:
        l = l + p.sum(axis=-1, keepdims=True)
    o = _dot(ps[0].astype(BF16), values[0])
    for p, v in zip(ps[1:], values[1:]):
        o = o + _dot(p.astype(BF16), v)
    return o / l


def _attn_heads(q_ref, qn_ref, qp_ref, segs, o_ref):
    for hh in range(GQA_HEADS):
        sl = slice(hh * HEAD_DIM, (hh + 1) * HEAD_DIM)
        kh = hh // (GQA_HEADS // GQA_KV_HEADS)
        ksl = slice(kh * HEAD_DIM, (kh + 1) * HEAD_DIM)
        q = q_ref[:, sl]
        scores = [_dot_nt(q, s[0][:, ksl]) for s in segs]
        o_ref[:, sl] = _attend(scores, [s[1][:, ksl] for s in segs], G_SCALE).astype(BF16)
    o0 = GQA_HEADS * HEAD_DIM
    v0 = MLA_HEADS * MLA_NOPE
    for hh in range(MLA_HEADS):
        sl = slice(hh * LANES, (hh + 1) * LANES)
        q = jnp.concatenate([qn_ref[:, sl], qp_ref[:, sl]], axis=1)
        scores = [_dot_nt(q, jnp.concatenate([s[2][:, sl], s[3][...]], axis=1)) for s in segs]
        vals = [s[2][:, v0 + hh * MLA_V:v0 + (hh + 1) * MLA_V] for s in segs]
        o_ref[:, o0 + hh * MLA_V:o0 + (hh + 1) * MLA_V] = _attend(scores, vals, M_SCALE).astype(BF16)


def _attn_kernel(q_ref, qn_ref, qp_ref,
                 ck_ref, cv_ref, ckvm_ref, ckpe_ref,
                 dk_ref, dv_ref, dkvm_ref, dkpe_ref,
                 pk_ref, pv_ref, pkvm_ref, pkpe_ref,
                 o_ref):
    j = pl.program_id(0)

    @pl.when(j < N_CTX_TILES)
    def _():
        _attn_heads(q_ref, qn_ref, qp_ref, [(ck_ref, cv_ref, ckvm_ref, ckpe_ref)], o_ref)

    @pl.when(j >= N_CTX_TILES)
    def _():
        _attn_heads(q_ref, qn_ref, qp_ref,
                    [(dk_ref, dv_ref, dkvm_ref, dkpe_ref), (pk_ref, pv_ref, pkvm_ref, pkpe_ref)], o_ref)


def _dec_req(j):
    return jnp.maximum(j - N_CTX_TILES, 0) // DEC_TILES_PER_REQ


def _attn(layer_i, qg, kg, vg, qmn, qmp, kvm, kpeb, pk, pv, pkvm, pkpe):
    tile = lambda w: pl.BlockSpec((TOK, w), lambda j: (j, 0))
    ctx = lambda w: pl.BlockSpec((TOK, w), lambda j: (jnp.minimum(j, N_CTX_TILES - 1), 0))
    dec = lambda w: pl.BlockSpec((DEC_SEQ, w), lambda j: (CTX_TOK // DEC_SEQ + _dec_req(j), 0))
    past = lambda w: pl.BlockSpec((PAST_LEN, w), lambda j: (_dec_req(j), 0))
    cache = lambda w: pl.BlockSpec((None, None, PAST_LEN, w), lambda j: (_dec_req(j), layer_i, 0, 0))
    kvw = GQA_KV_HEADS * HEAD_DIM
    mw = MLA_HEADS * (MLA_NOPE + MLA_V)
    return pl.pallas_call(
        _attn_kernel,
        grid=(N_TILES,),
        in_specs=[tile(GQA_HEADS * HEAD_DIM), tile(MLA_HEADS * MLA_NOPE), tile(MLA_HEADS * LANES),
                  ctx(kvw), ctx(kvw), ctx(mw), ctx(LANES),
                  dec(kvw), dec(kvw), dec(mw), dec(LANES),
                  cache(kvw), cache(kvw), past(mw), cache(LANES)],
        out_specs=tile(D_MODEL),
        out_shape=jax.ShapeDtypeStruct((N_TOK, D_MODEL), BF16),
        compiler_params=_cparams(),
        name="attention",
    )(qg, qmn, qmp, kg, vg, kvm, kpeb, kg, vg, kvm, kpeb, pk, pv, pkvm, pkpe)


def _residual_and_router(x, mix_out, m, g2_ref, wr_ref, x1_ref, hff_ref, aff_ref):
    x1 = x + m[2:3] * mix_out
    x1_ref[...] = x1
    hf = _rms(x1, g2_ref[...]) * (1.0 + m[4:5]) + m[3:4]
    hi = hf.astype(BF16)
    hff_ref[...] = hi
    lo = (hf - hi.astype(F32)).astype(BF16)
    r = _dot(hi, wr_ref[0]) + _dot(lo, wr_ref[1])
    logits = r + pltpu.roll(r, LANES - N_EXPERTS, 1)
    lane = lax.broadcasted_iota(jnp.int32, logits.shape, 1)
    logits = jnp.where(lane < N_EXPERTS, logits, -jnp.inf)
    e = jnp.exp(logits - logits.max(axis=-1, keepdims=True))
    aff_ref[...] = e / e.sum(axis=-1, keepdims=True)


def _epilogue_specs():
    tile = lambda w: pl.BlockSpec((TOK, w), lambda j: (j, 0))
    out_specs = [tile(D_MODEL), tile(D_MODEL), tile(LANES)]
    out_shape = [jax.ShapeDtypeStruct((N_TOK, D_MODEL), F32),
                 jax.ShapeDtypeStruct((N_TOK, D_MODEL), BF16),
                 jax.ShapeDtypeStruct((N_TOK, LANES), F32)]
    return out_specs, out_shape


def _attn_out_kernel(nx, *refs):
    o_ref, mod_ref, wout_ref, g2_ref, wr_ref, x1_ref, hff_ref, aff_ref = refs[nx:]
    out = _dot(o_ref[...], wout_ref[...])
    _residual_and_router(_x_tile(refs[:nx]), out, mod_ref[...], g2_ref, wr_ref, x1_ref, hff_ref, aff_ref)


def _attn_out(x, o, mod, consts):
    out_specs, out_shape = _epilogue_specs()
    x_specs, x_args = _x_in(x)
    return pl.pallas_call(
        functools.partial(_attn_out_kernel, len(x_args)),
        grid=(N_TILES,),
        in_specs=x_specs + [pl.BlockSpec((TOK, D_MODEL), lambda j: (j, 0)), mod[1]] + [s for _, s in consts],
        out_specs=out_specs, out_shape=out_shape,
        compiler_params=_cparams(),
        name="attn_out",
    )(*x_args, o, mod[0], *[a for a, _ in consts])


CONV_HALO = 16
SUBLANES = 8
CONV_BASE = CONV_HALO - CONV_WIDTH // 2
CONV_SH_ROWS = (CONV_BASE + CONV_WIDTH - 1) // SUBLANES * SUBLANES + TOK
CONV_RING = 4


def _cs_pre_kernel(x_ref, mod_ref, g1_ref, win_ref, w_ref, b_ref, lng_ref, lnb_ref, us_ref, o_ref,
                   ring_ref, pad_ref, sh_ref, u_ref):
    j = pl.program_id(0)
    slot = lambda d: (j + d) & (CONV_RING - 1)

    @pl.when(j == 0)
    def _():
        ring_ref[...] = jnp.zeros_like(ring_ref)

    @pl.when(j < N_TILES)
    def _():
        m = mod_ref[...]
        h = _rms(x_ref[...], g1_ref[...]) * (1.0 + m[1:2]) + m[0:1]
        proj = _dot(h.astype(BF16), win_ref[...])
        ring_ref[slot(0)] = proj[:, :CONV_DIM] * jax.nn.sigmoid(proj[:, CONV_DIM:2 * CONV_DIM])
        us_ref[...] = proj[:, 2 * CONV_DIM:]

    @pl.when(j >= 1)
    def _():
        c = j - 1
        q = (c - N_CTX_TILES) % DEC_TILES_PER_REQ
        has_prev = jnp.logical_and(c >= N_CTX_TILES, q != 0)
        has_next = jnp.logical_and(c >= N_CTX_TILES, q != DEC_TILES_PER_REQ - 1)
        pad_ref[0:CONV_HALO, :] = jnp.where(has_prev, ring_ref[slot(-2), TOK - CONV_HALO:, :], 0.0)
        pad_ref[CONV_HALO:CONV_HALO + TOK, :] = ring_ref[slot(-1)]
        pad_ref[CONV_HALO + TOK:, :] = jnp.where(has_next, ring_ref[slot(0), 0:CONV_HALO, :], 0.0)
        for s in range(1, SUBLANES):
            sh_ref[s - 1] = pad_ref[s:s + CONV_SH_ROWS, :]
        for lt in range(CONV_DIM // LANES):
            ls = slice(lt * LANES, (lt + 1) * LANES)
            acc = None
            for tap in range(CONV_WIDTH):
                a8, s = (CONV_BASE + tap) // SUBLANES * SUBLANES, (CONV_BASE + tap) % SUBLANES
                win = pad_ref[a8:a8 + TOK, ls] if s == 0 else sh_ref[s - 1, a8:a8 + TOK, ls]
                term = w_ref[tap:tap + 1, ls] * win
                acc = term if acc is None else acc + term
            u_ref[:, ls] = acc + b_ref[:, ls]
        u = u_ref[...]
        mu = jnp.mean(u, axis=-1, keepdims=True)
        uc = u - mu
        y = uc * lax.rsqrt(jnp.mean(uc * uc, axis=-1, keepdims=True) + EPS) * lng_ref[...] + lnb_ref[...]
        o_ref[...] = _silu(y).astype(BF16)


def _cs_pre(x, mod, consts):
    last = lambda j: jnp.minimum(j, N_TILES - 1)
    return pl.pallas_call(
        _cs_pre_kernel,
        grid=(N_TILES + 1,),
        in_specs=[pl.BlockSpec((TOK, D_MODEL), lambda j: (last(j), 0)), mod[1]] + [s for _, s in consts],
        out_specs=[pl.BlockSpec((TOK, S5_DIM), lambda j: (last(j), 0)),
                   pl.BlockSpec((TOK, CONV_DIM), lambda j: (jnp.maximum(j - 1, 0), 0))],
        out_shape=[jax.ShapeDtypeStruct((N_TOK, S5_DIM), F32),
                   jax.ShapeDtypeStruct((N_TOK, CONV_DIM), BF16)],
        scratch_shapes=[pltpu.VMEM((CONV_RING, TOK, CONV_DIM), F32),
                        pltpu.VMEM((TOK + 2 * CONV_HALO, CONV_DIM), F32),
                        pltpu.VMEM((SUBLANES - 1, CONV_SH_ROWS, CONV_DIM), F32),
                        pltpu.VMEM((TOK, CONV_DIM), F32)],
        compiler_params=_cparams(),
        name="cs_pre",
    )(x, mod[0], *[a for a, _ in consts])


S5_PITCH = 24
S5_SLABS = S5_DIM // LANES
S5_SCAN_COLS = 256


def _to_time_major(blk_ref, slab_ref, rows):
    for s in range(S5_SLABS):
        for j in range(rows):
            slab_ref[s, j * S5_PITCH:j * S5_PITCH + S5_TT, :] = blk_ref[j, :, s * LANES:(s + 1) * LANES]
    steps = [jnp.concatenate([slab_ref[s, pl.ds(t, rows, stride=S5_PITCH), :] for s in range(S5_SLABS)], axis=1)
             for t in range(S5_TT)]
    return jnp.concatenate(steps, axis=0)


def _from_time_major(y, slab_ref, out_ref, rows, c0):
    n_slabs = y.shape[1] // LANES
    for t in range(S5_TT):
        for s in range(n_slabs):
            slab_ref[s, pl.ds(t, rows, stride=S5_PITCH), :] = y[t * rows:(t + 1) * rows, s * LANES:(s + 1) * LANES]
    for s in range(n_slabs):
        for j in range(rows):
            out_ref[j, :, c0 + s * LANES:c0 + (s + 1) * LANES] = slab_ref[s, j * S5_PITCH:j * S5_PITCH + S5_TT, :]


def _s5_scan_quarter(lam_ref, st_ref, bu, d, q, rows, reverse):
    ncb = S5_QS // S5_SCAN_COLS
    states = [[None] * (2 * ncb) for _ in range(S5_TT)]
    s0 = q * 2 * S5_QS
    for cb in range(ncb):
        c_re = slice(cb * S5_SCAN_COLS, (cb + 1) * S5_SCAN_COLS)
        c_im = slice(S5_QS + cb * S5_SCAN_COLS, S5_QS + (cb + 1) * S5_SCAN_COLS)
        st_re = slice(s0 + c_re.start, s0 + c_re.stop)
        st_im = slice(s0 + c_im.start, s0 + c_im.stop)
        lr = lam_ref[d, q, 0:1, c_re]
        li = lam_ref[d, q, 1:2, c_re]
        sr = st_ref[d, :, st_re]
        si = st_ref[d, :, st_im]
        for k in range(S5_TT):
            t = (S5_TT - 1 - k) if reverse else k
            nr = lr * sr - li * si
            ni = lr * si + li * sr
            if bu is not None:
                nr = nr + bu[t * rows:(t + 1) * rows, c_re]
                ni = ni + bu[t * rows:(t + 1) * rows, c_im]
            sr, si = nr, ni
            states[t][cb] = sr
            states[t][ncb + cb] = si
        st_ref[d, :, st_re] = sr
        st_ref[d, :, st_im] = si
    return jnp.concatenate([jnp.concatenate(row, axis=1) for row in states], axis=0)


def _s5_kernel(rows, has_input, *refs):
    if has_input:
        uf_ref, ur_ref, wb_ref, dsk_ref = refs[:4]
        refs = refs[4:]
    wc_ref, lam_ref, s0_ref, yf_ref, yr_ref, fin_ref, st_ref, uslab_ref, yslab_ref = refs
    i = pl.program_id(0)

    @pl.when(i == 0)
    def _():
        st_ref[...] = s0_ref[...]

    for d, (y_ref, reverse) in enumerate(((yf_ref, False), (yr_ref, True))):
        if has_input:
            u = _to_time_major(ur_ref if reverse else uf_ref, uslab_ref, rows)
            ub = u.astype(BF16)
        for q in range(S5_Q):
            csl = slice(q * S5_QC, (q + 1) * S5_QC)
            bu = _dot(ub[:, csl], wb_ref[d, q]) if has_input else None
            s_all = _s5_scan_quarter(lam_ref, st_ref, bu, d, q, rows, reverse)
            y = _dot(s_all.astype(BF16), wc_ref[d, q])
            if has_input and not reverse:
                y = y + dsk_ref[:, csl] * u[:, csl]
            _from_time_major(y, yslab_ref.at[q % 2], y_ref, rows, q * S5_QC)

    @pl.when(i == S5_NB - 1)
    def _():
        fin_ref[...] = st_ref[...]


def _s5(rows, u3, wb, wc, lam, s0, dskip):
    fwd = pl.BlockSpec((rows, S5_TT, S5_DIM), lambda i: (0, i, 0))
    rev = pl.BlockSpec((rows, S5_TT, S5_DIM), lambda i: (0, S5_NB - 1 - i, 0))
    has_input = u3 is not None
    in_specs, args = [], []
    if has_input:
        in_specs += [fwd, rev, wb[1], dskip[1]]
        args += [u3, u3, wb[0], dskip[0]]
    in_specs += [wc[1], lam[1], s0[1]]
    args += [wc[0], lam[0], s0[0]]
    y_shape = jax.ShapeDtypeStruct((rows, TOK, S5_DIM), F32)
    st_shape = (2, rows, S5_STATE_COLS)
    return pl.pallas_call(
        functools.partial(_s5_kernel, rows, has_input),
        grid=(S5_NB,),
        in_specs=in_specs,
        out_specs=[fwd, rev, _const_spec(st_shape)],
        out_shape=[y_shape, y_shape, jax.ShapeDtypeStruct(st_shape, F32)],
        scratch_shapes=[pltpu.VMEM(st_shape, F32),
                        pltpu.VMEM((S5_SLABS, rows * S5_PITCH, LANES), F32),
                        pltpu.VMEM((2, S5_QC // LANES, rows * S5_PITCH, LANES), F32)],
        compiler_params=_cparams(),
        name="s5_scan" if has_input else "s5_carry_fix",
    )(*args)


def _gelu_tanh(x):
    return x * (0.5 * (1.0 + jnp.tanh(math.sqrt(2.0 / math.pi) * (x + 0.044715 * (x * x * x)))))


def _cs_post_kernel(uc_ref, yf_ref, yr_ref, cf_ref, cr_ref, x_ref, mod_ref, wglu_ref, bglu_ref, wout_ref,
                    g2_ref, wr_ref, x1_ref, hff_ref, aff_ref):
    y = _gelu_tanh(yf_ref[...] + yr_ref[...] + cf_ref[...] + cr_ref[...])
    y = y * jax.nn.sigmoid(_dot(y.astype(BF16), wglu_ref[...]) + bglu_ref[...])
    out = _dot(uc_ref[...], wout_ref[0:CONV_DIM, :]) + _dot(y.astype(BF16), wout_ref[CONV_DIM:, :])
    _residual_and_router(x_ref[...], out, mod_ref[...], g2_ref, wr_ref, x1_ref, hff_ref, aff_ref)


def _fix_slot(j, reverse):
    d = j - N_CTX_TILES
    r, q = d // DEC_TILES_PER_REQ, d % DEC_TILES_PER_REQ
    if reverse:
        ok, slot = q != DEC_TILES_PER_REQ - 1, r * (DEC_TILES_PER_REQ - 1) + q
    else:
        ok, slot = q != 0, r * (DEC_TILES_PER_REQ - 1) + q - 1
    return jnp.where(jnp.logical_and(j >= N_CTX_TILES, ok), slot, S5_FIX_ROWS - 1)


def _cs_post(uc, yf, yr, cf, cr, x, mod, consts):
    tile = lambda w: pl.BlockSpec((TOK, w), lambda j: (j, 0))
    out_specs, out_shape = _epilogue_specs()
    return pl.pallas_call(
        _cs_post_kernel,
        grid=(N_TILES,),
        in_specs=[tile(CONV_DIM), tile(S5_DIM), tile(S5_DIM),
                  pl.BlockSpec((TOK, S5_DIM), lambda j: (_fix_slot(j, False), 0)),
                  pl.BlockSpec((TOK, S5_DIM), lambda j: (_fix_slot(j, True), 0)),
                  tile(D_MODEL), mod[1]] + [s for _, s in consts],
        out_specs=out_specs, out_shape=out_shape,
        compiler_params=_cparams(),
        name="cs_post",
    )(uc, yf, yr, cf, cr, x, mod[0], *[a for a, _ in consts])


RANK_CHUNK = 128
GATHER_ROWS = 512


def _moe_select(cap, aff_ref, hff_ref, xs_ref, gs_ref, g_ref, rank_ref, p_ref):
    n = aff_ref.shape[0]
    nblk = n // RANK_CHUNK
    a_t = aff_ref[...].T
    sub = lax.broadcasted_iota(jnp.int32, (RANK_CHUNK, RANK_CHUNK), 0)
    lan = lax.broadcasted_iota(jnp.int32, (RANK_CHUNK, RANK_CHUNK), 1)
    earlier = sub < lan
    slot = lax.broadcasted_iota(jnp.int32, (cap, n), 0).astype(F32)
    rank_ref[...] = jnp.zeros_like(rank_ref)
    for e in range(N_EXPERTS):
        row = a_t[e:e + 1, :]
        cols = [aff_ref[c * RANK_CHUNK:(c + 1) * RANK_CHUNK, e:e + 1] for c in range(nblk)]
        for b in range(nblk):
            rb = row[:, b * RANK_CHUNK:(b + 1) * RANK_CHUNK]
            cnt = None
            for c in range(nblk):
                if c < b:
                    part = jnp.where(cols[c] >= rb, 1.0, 0.0)
                elif c > b:
                    part = jnp.where(cols[c] > rb, 1.0, 0.0)
                else:
                    tie = jnp.where(earlier, cols[c], -1.0) == rb
                    part = jnp.where(cols[c] > rb, 1.0, 0.0) + jnp.where(tie, 1.0, 0.0)
                cnt = part if cnt is None else cnt + part
            rank_ref[e:e + 1, b * RANK_CHUNK:(b + 1) * RANK_CHUNK] = jnp.sum(cnt, axis=0, keepdims=True)
        rank = rank_ref[e:e + 1, :]
        onehot = slot == rank
        p_ref[e * cap:(e + 1) * cap, :] = jnp.where(onehot, 1.0, 0.0).astype(BF16)
        gs_ref[e] = jnp.sum(jnp.where(onehot, row, 0.0), axis=1, keepdims=True)
    grp = GATHER_ROWS // cap
    for e0 in range(0, N_EXPERTS, grp):
        xs = _dot(p_ref[e0 * cap:(e0 + grp) * cap, :], hff_ref[...])
        for k in range(grp):
            xs_ref[e0 + k] = xs[k * cap:(k + 1) * cap].astype(BF16)
    rank_t = rank_ref[...].T
    per = LANES // cap
    lane = lax.broadcasted_iota(jnp.int32, (n, LANES), 1)
    slot_lane = (lane & (cap - 1)).astype(F32)
    for blk in range(N_EXPERTS // per):
        rc = rank_t[:, blk * per:blk * per + 1]
        for k in range(1, per):
            rc = jnp.where(lane >= k * cap, rank_t[:, blk * per + k:blk * per + k + 1], rc)
        g_ref[:, blk * LANES:(blk + 1) * LANES] = jnp.where(slot_lane == rc, 1.0, 0.0).astype(BF16)


def _moe_gather_kernel(affc_ref, hffc_ref, affd_ref, hffd_ref,
                       xsc_ref, gsc_ref, gc_ref, xsd_ref, gsd_ref, gd_ref,
                       rankc_ref, pc_ref, rankd_ref, pd_ref):
    s = pl.program_id(0)

    @pl.when(s < BATCH)
    def _():
        _moe_select(CAP_CTX, affc_ref, hffc_ref, xsc_ref, gsc_ref, gc_ref, rankc_ref, pc_ref)

    @pl.when(s >= BATCH)
    def _():
        _moe_select(CAP_DEC, affd_ref, hffd_ref, xsd_ref, gsd_ref, gd_ref, rankd_ref, pd_ref)


def _moe_gather(aff, hff):
    cidx = lambda s: jnp.minimum(s, BATCH - 1)
    didx = lambda s: jnp.maximum(s - BATCH, 0)
    dec0 = CTX_TOK // DEC_SEQ
    return pl.pallas_call(
        _moe_gather_kernel,
        grid=(BATCH + DEC_BATCH,),
        in_specs=[pl.BlockSpec((SEQ, LANES), lambda s: (cidx(s), 0)),
                  pl.BlockSpec((SEQ, D_MODEL), lambda s: (cidx(s), 0)),
                  pl.BlockSpec((DEC_SEQ, LANES), lambda s: (dec0 + didx(s), 0)),
                  pl.BlockSpec((DEC_SEQ, D_MODEL), lambda s: (dec0 + didx(s), 0))],
        out_specs=[pl.BlockSpec((N_EXPERTS, CAP_CTX, D_MODEL), lambda s: (0, cidx(s), 0)),
                   pl.BlockSpec((N_EXPERTS, CAP_CTX, 1), lambda s: (0, cidx(s), 0)),
                   pl.BlockSpec((SEQ, N_EXPERTS * CAP_CTX), lambda s: (cidx(s), 0)),
                   pl.BlockSpec((N_EXPERTS, CAP_DEC, D_MODEL), lambda s: (0, didx(s), 0)),
                   pl.BlockSpec((N_EXPERTS, CAP_DEC, 1), lambda s: (0, didx(s), 0)),
                   pl.BlockSpec((DEC_SEQ, N_EXPERTS * CAP_DEC), lambda s: (didx(s), 0))],
        out_shape=[jax.ShapeDtypeStruct((N_EXPERTS, XS_CTX_ROWS, D_MODEL), BF16),
                   jax.ShapeDtypeStruct((N_EXPERTS, XS_CTX_ROWS, 1), F32),
                   jax.ShapeDtypeStruct((CTX_TOK, N_EXPERTS * CAP_CTX), BF16),
                   jax.ShapeDtypeStruct((N_EXPERTS, XS_DEC_ROWS, D_MODEL), BF16),
                   jax.ShapeDtypeStruct((N_EXPERTS, XS_DEC_ROWS, 1), F32),
                   jax.ShapeDtypeStruct((DEC_BATCH * DEC_SEQ, N_EXPERTS * CAP_DEC), BF16)],
        scratch_shapes=[pltpu.VMEM((LANES, SEQ), F32), pltpu.VMEM((N_EXPERTS * CAP_CTX, SEQ), BF16),
                        pltpu.VMEM((LANES, DEC_SEQ), F32), pltpu.VMEM((N_EXPERTS * CAP_DEC, DEC_SEQ), BF16)],
        compiler_params=_cparams(),
        name="moe_gather",
    )(aff, hff, aff, hff)


FF_CHUNK = 512
N_FF_CHUNKS = EXPERT_FF // FF_CHUNK


def _moe_ffn_kernel(xsc_ref, xsd_ref, gsc_ref, gsd_ref, wg_ref, wu_ref, wd_ref, yc_ref, yd_ref, acc_ref):
    f = pl.program_id(1)

    @pl.when(f == 0)
    def _():
        acc_ref[...] = jnp.zeros_like(acc_ref)

    wg = wg_ref[...].astype(BF16)
    wu = wu_ref[...].astype(BF16)
    wd = wd_ref[...].astype(BF16)
    for xs_ref, r0, rows in ((xsc_ref, 0, XS_CTX_ROWS), (xsd_ref, XS_CTX_ROWS, XS_DEC_ROWS)):
        xs = xs_ref[...]
        act = _silu(_dot(xs, wg)) * _dot(xs, wu)
        acc_ref[r0:r0 + rows, :] += _dot(act.astype(BF16), wd)

    @pl.when(f == N_FF_CHUNKS - 1)
    def _():
        for r in range(BATCH):
            rows = slice(r * CAP_CTX, (r + 1) * CAP_CTX)
            yc_ref[r] = (acc_ref[rows, :] * gsc_ref[rows, :]).astype(BF16)
        for r in range(DEC_BATCH):
            rows = slice(r * CAP_DEC, (r + 1) * CAP_DEC)
            acc_rows = slice(XS_CTX_ROWS + rows.start, XS_CTX_ROWS + rows.stop)
            yd_ref[r] = (acc_ref[acc_rows, :] * gsd_ref[rows, :]).astype(BF16)


def _moe_ffn(layer, xsc, xsd, gsc, gsd, w_gate, w_up, w_down):
    per_e = lambda rows, w: pl.BlockSpec((None, rows, w), lambda e, f: (e, 0, 0))
    return pl.pallas_call(
        _moe_ffn_kernel,
        grid=(N_EXPERTS, N_FF_CHUNKS),
        in_specs=[per_e(XS_CTX_ROWS, D_MODEL), per_e(XS_DEC_ROWS, D_MODEL),
                  per_e(XS_CTX_ROWS, 1), per_e(XS_DEC_ROWS, 1),
                  pl.BlockSpec((None, None, D_MODEL, FF_CHUNK), lambda e, f: (layer, e, 0, f)),
                  pl.BlockSpec((None, None, D_MODEL, FF_CHUNK), lambda e, f: (layer, e, 0, f)),
                  pl.BlockSpec((None, None, FF_CHUNK, D_MODEL), lambda e, f: (layer, e, f, 0))],
        out_specs=[pl.BlockSpec((BATCH, CAP_CTX, D_MODEL), lambda e, f: (0, e, 0)),
                   pl.BlockSpec((DEC_BATCH, CAP_DEC, D_MODEL), lambda e, f: (0, e, 0))],
        out_shape=[jax.ShapeDtypeStruct((BATCH, N_EXPERTS * CAP_CTX, D_MODEL), BF16),
                   jax.ShapeDtypeStruct((DEC_BATCH, N_EXPERTS * CAP_DEC, D_MODEL), BF16)],
        scratch_shapes=[pltpu.VMEM((XS_CTX_ROWS + XS_DEC_ROWS, D_MODEL), F32)],
        compiler_params=_cparams(2),
        name="moe_ffn",
    )(xsc, xsd, gsc, gsd, w_gate, w_up, w_down)


def _moe_combine_kernel(final, gc_ref, yc_ref, gd_ref, yd_ref, x1_ref, mod_ref, fg_ref, *o_refs):
    j = pl.program_id(0)

    def finish(comb, o_ref):
        x2 = x1_ref[...] + mod_ref[5:6, :] * comb
        o_ref[...] = _rms(x2, fg_ref[...]) if final else x2

    @pl.when(j < N_CTX_TILES)
    def _():
        finish(_dot(gc_ref[...], yc_ref[...]), o_refs[0])

    @pl.when(j >= N_CTX_TILES)
    def _():
        finish(_dot(gd_ref[...], yd_ref[...]), o_refs[-1])


def _moe_combine(final, gc, yc, gd, yd, x1, mod, fg):
    if final:
        out_specs = [pl.BlockSpec((TOK, D_MODEL), lambda j: (_ctx_tile(j), 0)),
                     pl.BlockSpec((TOK, D_MODEL), lambda j: (_dec_tile(j), 0))]
        out_shape = [jax.ShapeDtypeStruct((CTX_TOK, D_MODEL), F32),
                     jax.ShapeDtypeStruct((N_TOK - CTX_TOK, D_MODEL), F32)]
    else:
        out_specs = pl.BlockSpec((TOK, D_MODEL), lambda j: (j, 0))
        out_shape = jax.ShapeDtypeStruct((N_TOK, D_MODEL), F32)
    return pl.pallas_call(
        functools.partial(_moe_combine_kernel, final),
        grid=(N_TILES,),
        in_specs=[pl.BlockSpec((TOK, N_EXPERTS * CAP_CTX), lambda j: (_ctx_tile(j), 0)),
                  pl.BlockSpec((None, N_EXPERTS * CAP_CTX, D_MODEL), lambda j: (_ctx_tile(j), 0, 0)),
                  pl.BlockSpec((TOK, N_EXPERTS * CAP_DEC), lambda j: (_dec_tile(j), 0)),
                  pl.BlockSpec((None, N_EXPERTS * CAP_DEC, D_MODEL),
                               lambda j: (_dec_tile(j) // DEC_TILES_PER_REQ, 0, 0)),
                  pl.BlockSpec((TOK, D_MODEL), lambda j: (j, 0)),
                  mod[1], _const_spec((1, D_MODEL))],
        out_specs=out_specs, out_shape=out_shape,
        compiler_params=_cparams(),
        name="moe_combine",
    )(gc, yc, gd, yd, x1, mod[0], fg)


def _moe(layer, final, x1, hff, aff, mod, w_gate, w_up, w_down, fg):
    xsc, gsc, gc, xsd, gsd, gd = _moe_gather(aff, hff)
    yc, yd = _moe_ffn(layer, xsc, xsd, gsc, gsd, w_gate, w_up, w_down)
    return _moe_combine(final, gc, yc, gd, yd, x1, mod, fg)


def _axial_rope(n_tokens, rot_dim):
    rows = n_tokens // GRID_W
    per_axis = rot_dim // 4
    freqs = ROPE_THETA ** (-jnp.arange(per_axis, dtype=F32) / per_axis)
    row = jnp.repeat(jnp.arange(rows, dtype=F32), GRID_W)
    col = jnp.tile(jnp.arange(GRID_W, dtype=F32), rows)
    ang = jnp.concatenate([row[:, None] * freqs, col[:, None] * freqs], axis=-1)
    return jnp.cos(ang), jnp.sin(ang)


def _rope_table():
    cg, sg = _axial_rope(DEC_SEQ, HEAD_DIM)
    cm, sm = _axial_rope(DEC_SEQ, MLA_ROPE)
    z = jnp.zeros_like(cm)
    pos = jnp.concatenate([
        jnp.concatenate([cg, cg], -1), jnp.concatenate([-sg, sg], -1),
        jnp.concatenate([cm, cm, z, z], -1), jnp.concatenate([-sm, z, z, z], -1),
        jnp.concatenate([z, sm, z, z], -1)], axis=-1)
    one, zero = jnp.ones((TOK, LANES), F32), jnp.zeros((TOK, LANES), F32)
    ident = jnp.concatenate([one, zero, one, zero, zero], axis=-1)
    return jnp.concatenate([ident, pos], axis=0)


def _router_weights(w):
    hi = w.astype(BF16)
    lo = (w - hi.astype(F32)).astype(BF16)
    z = jnp.zeros(w.shape[:-1] + (LANES - 2 * N_EXPERTS,), BF16)
    return jnp.stack([jnp.concatenate([hi, lo, z], axis=-1),
                      jnp.concatenate([hi, jnp.zeros_like(lo), z], axis=-1)], axis=1)


def _mla_weights(w_qb, w_kvb):
    n = w_qb.shape[0]
    qb = w_qb.astype(BF16).reshape(n, MLA_Q_RANK, MLA_HEADS, MLA_NOPE + MLA_ROPE)
    qb_pe = jnp.pad(qb[..., MLA_NOPE:], ((0, 0), (0, 0), (0, 0), (0, LANES - MLA_ROPE)))
    wqb = jnp.concatenate([qb[..., :MLA_NOPE].reshape(n, MLA_Q_RANK, -1), qb_pe.reshape(n, MLA_Q_RANK, -1)], axis=-1)
    kvb = w_kvb.astype(BF16).reshape(n, MLA_KV_RANK, MLA_HEADS, MLA_NOPE + MLA_V)
    wkvb = jnp.concatenate([kvb[..., :MLA_NOPE].reshape(n, MLA_KV_RANK, -1),
                            kvb[..., MLA_NOPE:].reshape(n, MLA_KV_RANK, -1)], axis=-1)
    return wqb, wkvb


def _s5_weights(a_re, a_im, log_step, b_re, b_im, c_re, c_im):
    n_ld = a_re.shape[0] * a_re.shape[1]
    flat = lambda v: v.astype(F32).reshape((n_ld,) + v.shape[2:])
    a_re, a_im, log_step, b_re, b_im, c_re, c_im = map(flat, (a_re, a_im, log_step, b_re, b_im, c_re, c_im))
    ar, ai = a_re, a_im
    step = jnp.exp(log_step)[..., None]

    def cexp(k):
        mag = jnp.exp(k * ar * step)
        return mag * jnp.cos(k * ai * step), mag * jnp.sin(k * ai * step)

    lr, li = cexp(1.0)
    den = ar * ar + ai * ai
    gr = ((lr - 1.0) * ar + li * ai) / den
    gi = (li * ar - (lr - 1.0) * ai) / den
    br, bi = b_re, b_im
    bbr = gr[..., None] * br - gi[..., None] * bi
    bbi = gr[..., None] * bi + gi[..., None] * br
    gq = S5_GROUPS // S5_Q
    group_of_col = (np.arange(2 * S5_QS) % S5_QS) // S5_STATE
    own = jnp.asarray(np.arange(gq)[:, None] == group_of_col[None, :])

    def rows_in(m):
        m = m.reshape(n_ld, S5_Q, gq, S5_STATE, S5_GROUP).transpose(0, 1, 4, 2, 3)
        return m.reshape(n_ld, S5_Q, S5_GROUP, S5_QS)

    def cols_out(m):
        m = m.reshape(n_ld, S5_Q, gq, S5_GROUP, S5_STATE).transpose(0, 1, 2, 4, 3)
        return m.reshape(n_ld, S5_Q, S5_QS, S5_GROUP)

    b_rows = jnp.concatenate([rows_in(bbr), rows_in(bbi)], axis=-1).astype(BF16)
    wb = jnp.where(own[None, None, :, None, :], b_rows[:, :, None, :, :], 0)
    c_cols = jnp.concatenate([cols_out(c_re), cols_out(-c_im)], axis=-2).astype(BF16)
    wc = jnp.where(own.T[None, None, :, :, None], c_cols[:, :, :, None, :], 0)
    to_q = lambda v: v.reshape(n_ld, S5_Q, S5_QS)
    lam_q = jnp.stack([to_q(lr), to_q(li)], axis=2)
    cr_, ci_ = cexp(float(TOK))
    lam_chunk = jnp.stack([to_q(cr_), to_q(ci_)], axis=2)
    per_layer = lambda v, tail: v.reshape((n_ld // 2, 2) + tail)
    return (per_layer(wb, (S5_Q, S5_QC, 2 * S5_QS)), per_layer(wc, (S5_Q, 2 * S5_QS, S5_QC)),
            per_layer(lam_q, (S5_Q, 2, S5_QS)), per_layer(lam_chunk, (S5_STATE_COLS,)))


def _cmul_cols(a, b):
    a4 = a.reshape(a.shape[:-1] + (S5_Q, 2, S5_QS))
    b4 = b.reshape(b.shape[:-1] + (S5_Q, 2, S5_QS))
    re = a4[..., 0, :] * b4[..., 0, :] - a4[..., 1, :] * b4[..., 1, :]
    im = a4[..., 0, :] * b4[..., 1, :] + a4[..., 1, :] * b4[..., 0, :]
    return jnp.stack([re, im], axis=-2).reshape(re.shape[:-2] + (S5_STATE_COLS,))


def _s5_initial_states(state_s5):
    ns = state_s5.shape[1]
    h0 = state_s5.reshape(DEC_BATCH, ns, 2, 2, S5_Q, S5_QS).transpose(1, 2, 0, 4, 3, 5)
    h0 = h0.reshape(ns, 2, DEC_BATCH, S5_STATE_COLS)
    first = N_CTX_TILES + DEC_TILES_PER_REQ * np.arange(DEC_BATCH)
    s0 = jnp.zeros((ns, 2, N_TILES, S5_STATE_COLS), F32)
    return s0.at[:, 0, first].set(h0[:, 0]).at[:, 1, first + DEC_TILES_PER_REQ - 1].set(h0[:, 1])


def _s5_branch(us, wb, wc, lam_q, s0, dskip, lam_chunk):
    yf, yr, fin = _s5(N_TILES, us.reshape(N_TILES, TOK, S5_DIM), wb, wc, lam_q, s0, dskip)
    fz = fin[:, N_CTX_TILES:].reshape(2, DEC_BATCH, DEC_TILES_PER_REQ, S5_STATE_COLS)
    f1 = fz[0, :, 0]
    f2 = fz[0, :, 1] + _cmul_cols(lam_chunk[0], f1)
    f3 = fz[0, :, 2] + _cmul_cols(lam_chunk[0], f2)
    r2 = fz[1, :, 3]
    r1 = fz[1, :, 2] + _cmul_cols(lam_chunk[1], r2)
    r0 = fz[1, :, 1] + _cmul_cols(lam_chunk[1], r1)
    pad = jnp.zeros((S5_FIX_ROWS - DEC_BATCH * (DEC_TILES_PER_REQ - 1), S5_STATE_COLS), F32)
    sin_f = jnp.concatenate([jnp.stack([f1, f2, f3], 1).reshape(-1, S5_STATE_COLS), pad])
    sin_r = jnp.concatenate([jnp.stack([r0, r1, r2], 1).reshape(-1, S5_STATE_COLS), pad])
    sin = jnp.stack([sin_f, sin_r])
    cf, cr, _ = _s5(S5_FIX_ROWS, None, None, wc, lam_q, (sin, _const_spec(sin.shape)), None)
    rows2d = lambda a: a.reshape(-1, S5_DIM)
    new_state = fin[:, :N_CTX_TILES].reshape(2, BATCH, S5_Q, 2, S5_QS).transpose(1, 0, 3, 2, 4)
    new_state = new_state.reshape(BATCH, 2, 2, S5_GROUPS, S5_STATE)
    return rows2d(yf), rows2d(yr), rows2d(cf), rows2d(cr), new_state


def kernel(x_prompt, x_sample, cache_gqa_k, cache_gqa_v, cache_mla_ckv, cache_mla_kpe, state_s5, c, c_ctx, w_mod, b_mod, norm1_g, norm2_g, attn_w_in, gqa_q_norm, gqa_k_norm, mla_qa_norm, mla_w_qb, mla_kva_norm, mla_w_kvb, attn_w_out, cs_w_in, conv_w, conv_b, conv_ln_g, conv_ln_b, s5_a_re, s5_a_im, s5_log_step, s5_b_re, s5_b_im, s5_c_re, s5_c_im, s5_d, s5_w_glu, s5_b_glu, cs_w_out, moe_router, moe_w_gate, moe_w_up, moe_w_down, final_norm_g):
    x = (x_prompt.reshape(CTX_TOK, D_MODEL), x_sample.reshape(DEC_BATCH * DEC_SEQ, D_MODEL))
    cond8 = jnp.concatenate([c_ctx[None, :], c, jnp.zeros((8 - 1 - DEC_BATCH, D_MODEL), F32)])
    mod = _modulation(cond8, w_mod, b_mod)
    tile_row = np.concatenate([np.zeros(N_CTX_TILES, np.int32),
                               1 + np.repeat(np.arange(DEC_BATCH, dtype=np.int32), DEC_TILES_PER_REQ)])
    modt = mod[:, tile_row].reshape(DEPTH, N_TILES, 6, D_MODEL)
    rope_tab = _rope_table()
    fg = final_norm_g.reshape(1, D_MODEL)
    rows3 = lambda v: v.reshape(v.shape[0], 1, v.shape[-1])
    norm1, norm2 = rows3(norm1_g), rows3(norm2_g)
    attn_w_in_b, attn_w_out_b = attn_w_in.astype(BF16), attn_w_out.astype(BF16)
    cs_w_in_b, cs_w_out_b, w_glu_b = cs_w_in.astype(BF16), cs_w_out.astype(BF16), s5_w_glu.astype(BF16)
    wr_all = _router_weights(moe_router)
    wqb_all, wkvb_all = _mla_weights(mla_w_qb, mla_w_kvb)
    qn, kn, qan, kvan = rows3(gqa_q_norm), rows3(gqa_k_norm), rows3(mla_qa_norm), rows3(mla_kva_norm)
    n_attn = cache_gqa_k.shape[1]
    pk_all = cache_gqa_k.reshape(DEC_BATCH, n_attn, PAST_LEN, -1).astype(BF16)
    pv_all = cache_gqa_v.reshape(DEC_BATCH, n_attn, PAST_LEN, -1).astype(BF16)
    pkpe_all = jnp.pad(cache_mla_kpe.astype(BF16), ((0, 0), (0, 0), (0, 0), (0, LANES - MLA_ROPE)))
    conv_b3, ln_g3, ln_b3, dskip3, b_glu3 = map(rows3, (conv_b, conv_ln_g, conv_ln_b, s5_d, s5_b_glu))
    wb_all, wc_all, lam_all, lam_chunk_all = _s5_weights(s5_a_re, s5_a_im, s5_log_step, s5_b_re, s5_b_im,
                                                         s5_c_re, s5_c_im)
    s0_all = _s5_initial_states(state_s5)
    new_k, new_v, new_ckv, new_kpe, new_s5 = [], [], [], [], []
    for l in range(DEPTH):
        i = l // 2
        mod_l = _mod_operand(modt, l)
        if l % 2 == 0:
            wkvb = _sel(wkvb_all, i)
            (qg, kg, vg, kf, vf, ckvf, kpef, qmn, qmp, kvm, kpeb) = _attn_pre(
                x, mod_l, [_sel(norm1, l), _sel(attn_w_in_b, i), _sel(qn, i), _sel(kn, i), _sel(qan, i),
                           _sel(kvan, i), _sel(wqb_all, i), wkvb], rope_tab)
            pkvm = _mla_cache_kv(cache_mla_ckv, i, wkvb)
            o = _attn(i, qg, kg, vg, qmn, qmp, kvm, kpeb, pk_all, pv_all, pkvm, pkpe_all)
            x1, hff, aff = _attn_out(x, o, mod_l, [_sel(attn_w_out_b, i), _sel(norm2, l), _sel(wr_all, l)])
            new_k.append(kf.reshape(BATCH, SEQ, GQA_KV_HEADS, HEAD_DIM))
            new_v.append(vf.reshape(BATCH, SEQ, GQA_KV_HEADS, HEAD_DIM))
            new_ckv.append(ckvf.reshape(BATCH, SEQ, MLA_KV_RANK))
            new_kpe.append(kpef.reshape(BATCH, SEQ, MLA_ROPE))
        else:
            us, uc = _cs_pre(x, mod_l, [_sel(norm1, l), _sel(cs_w_in_b, i), _sel(conv_w, i), _sel(conv_b3, i),
                                        _sel(ln_g3, i), _sel(ln_b3, i)])
            yf, yr, cf, cr, ns = _s5_branch(us, _sel(wb_all, i), _sel(wc_all, i), _sel(lam_all, i),
                                            _sel(s0_all, i), _sel(dskip3, i), lam_chunk_all[i])
            new_s5.append(ns)
            x1, hff, aff = _cs_post(uc, yf, yr, cf, cr, x, mod_l,
                                    [_sel(w_glu_b, i), _sel(b_glu3, i), _sel(cs_w_out_b, i), _sel(norm2, l),
                                     _sel(wr_all, l)])
        x = _moe(l, l == DEPTH - 1, x1, hff, aff, mod_l, moe_w_gate, moe_w_up, moe_w_down, fg)
    y_prompt = x[0].reshape(BATCH, SEQ, D_MODEL)
    y_sample = x[1].reshape(DEC_BATCH, DEC_SEQ, D_MODEL)
    return (y_prompt, y_sample, jnp.stack(new_k, axis=1), jnp.stack(new_v, axis=1),
            jnp.stack(new_ckv, axis=1), jnp.stack(new_kpe, axis=1), jnp.stack(new_s5, axis=1))
```

```python
import functools
import math

import jax
import jax.numpy as jnp
import numpy as np
from jax import lax
from jax.experimental import pallas as pl
from jax.experimental.pallas import tpu as pltpu

F32 = jnp.float32
BF16 = jnp.bfloat16

D_MODEL = 2048
BATCH = 16
SEQ = 256
DEPTH = 4
DEC_BATCH = 2
DEC_SEQ = 1024
PAST_LEN = 512
GRID_W = 64
HEAD_DIM = 128
ROPE_THETA = 10000.0
EPS = 1e-6
GQA_HEADS = 8
GQA_KV_HEADS = 2
MLA_HEADS = 8
MLA_Q_RANK = 512
MLA_KV_RANK = 256
MLA_NOPE = 128
MLA_ROPE = 64
MLA_V = 128
CONV_DIM = D_MODEL // 2
CONV_WIDTH = 31
S5_DIM = D_MODEL // 2
S5_GROUP = 16
S5_GROUPS = S5_DIM // S5_GROUP
S5_STATE = 64
N_EXPERTS = 16
EXPERT_FF = 1024
EC_CAPACITY = 2

LANES = 128
VMEM_LIMIT = 56 * 1024 * 1024

TOK = 256
N_CTX_TILES = BATCH * SEQ // TOK
DEC_TILES_PER_REQ = DEC_SEQ // TOK
N_DEC_TILES = DEC_BATCH * DEC_TILES_PER_REQ
N_TILES = N_CTX_TILES + N_DEC_TILES
N_TOK = N_TILES * TOK
CTX_TOK = N_CTX_TILES * TOK
CAP_CTX = EC_CAPACITY * SEQ // N_EXPERTS
CAP_DEC = EC_CAPACITY * DEC_SEQ // N_EXPERTS
XS_CTX_ROWS = BATCH * CAP_CTX
XS_DEC_ROWS = DEC_BATCH * CAP_DEC

G_SCALE = HEAD_DIM ** -0.5
M_SCALE = (MLA_NOPE + MLA_ROPE) ** -0.5
LOG2_E = math.log2(math.e)

S5_TT = 16
S5_NB = TOK // S5_TT
S5_Q = 4
S5_QC = S5_DIM // S5_Q
S5_QS = S5_GROUPS // S5_Q * S5_STATE
S5_STATE_COLS = S5_Q * 2 * S5_QS
S5_FIX_ROWS = 8

_NT = (((1,), (1,)), ((), ()))


def _cparams(n_grid_dims=1):
    return pltpu.CompilerParams(dimension_semantics=("arbitrary",) * n_grid_dims,
                                vmem_limit_bytes=VMEM_LIMIT)


def _const_spec(shape):
    nd = len(shape)
    return pl.BlockSpec(shape, lambda *_: (0,) * nd)


def _sel(stacked, *idx):
    rest = stacked.shape[len(idx):]
    return stacked, pl.BlockSpec((None,) * len(idx) + rest, lambda *_: tuple(idx) + (0,) * len(rest))


def _mod_operand(modt_all, layer):
    return modt_all, pl.BlockSpec((None, None, 6, D_MODEL),
                                  lambda j: (layer, jnp.minimum(j, N_TILES - 1), 0, 0))


def _rms(x, g):
    return x * lax.rsqrt(jnp.mean(x * x, axis=-1, keepdims=True) + EPS) * g


def _silu(x):
    return x * jax.nn.sigmoid(x)


def _dot(a, b):
    return jnp.dot(a, b, preferred_element_type=F32)


def _dot_nt(a, b):
    return lax.dot_general(a, b, _NT, preferred_element_type=F32)


def _ctx_tile(j):
    return jnp.minimum(j, N_CTX_TILES - 1)


def _dec_tile(j):
    return jnp.maximum(j - N_CTX_TILES, 0)


def _x_in(x):
    if isinstance(x, tuple):
        return ([pl.BlockSpec((TOK, D_MODEL), lambda j: (_ctx_tile(j), 0)),
                 pl.BlockSpec((TOK, D_MODEL), lambda j: (_dec_tile(j), 0))], list(x))
    return [pl.BlockSpec((TOK, D_MODEL), lambda j: (j, 0))], [x]


def _x_tile(x_refs):
    if len(x_refs) == 2:
        return jnp.where(pl.program_id(0) < N_CTX_TILES, x_refs[0][...], x_refs[1][...])
    return x_refs[0][...]


MOD_TN = 1024


def _mod_kernel(cond_ref, w_ref, b_ref, o_ref):
    a = _silu(cond_ref[...]).astype(BF16)
    o_ref[...] = _dot(a, w_ref[...].astype(BF16)) + b_ref[...]


def _modulation(cond8, w_mod, b_mod):
    n_out = 6 * D_MODEL
    return pl.pallas_call(
        _mod_kernel,
        grid=(DEPTH, n_out // MOD_TN),
        in_specs=[
            pl.BlockSpec((8, D_MODEL), lambda l, n: (0, 0)),
            pl.BlockSpec((None, D_MODEL, MOD_TN), lambda l, n: (l, 0, n)),
            pl.BlockSpec((None, 1, MOD_TN), lambda l, n: (l, 0, n)),
        ],
        out_specs=pl.BlockSpec((None, 8, MOD_TN), lambda l, n: (l, 0, n)),
        out_shape=jax.ShapeDtypeStruct((DEPTH, 8, n_out), F32),
        compiler_params=_cparams(2),
        name="modulation",
    )(cond8, w_mod, b_mod.reshape(DEPTH, 1, n_out))


def _attn_pre_kernel(nx, *refs):
    (mod_ref, g1_ref, win_ref, qn_ref, kn_ref, qan_ref, kvan_ref, wqb_ref, wkvb_ref, rope_ref,
     qg_ref, kg_ref, vg_ref, kf_ref, vf_ref, ckvf_ref, kpef_ref, qmn_ref, qmp_ref, kvm_ref, kpeb_ref) = refs[nx:]
    m = mod_ref[...]
    h = _rms(_x_tile(refs[:nx]), g1_ref[...]) * (1.0 + m[1:2]) + m[0:1]
    proj = _dot(h.astype(BF16), win_ref[...])
    rope = rope_ref[...]
    ga, gb, ma, mb, md = [rope[:, i * LANES:(i + 1) * LANES] for i in range(5)]

    def rope_g(xh):
        return xh * ga + pltpu.roll(xh, 64, 1) * gb

    def rope_m(xh):
        return xh * ma + pltpu.roll(xh, 96, 1) * mb + pltpu.roll(xh, 32, 1) * md

    for hh in range(GQA_HEADS):
        sl = slice(hh * HEAD_DIM, (hh + 1) * HEAD_DIM)
        qg_ref[:, sl] = rope_g(_rms(proj[:, sl], qn_ref[...])).astype(BF16)
    k0 = GQA_HEADS * HEAD_DIM
    k = jnp.concatenate([rope_g(_rms(proj[:, k0 + hh * HEAD_DIM:k0 + (hh + 1) * HEAD_DIM], kn_ref[...]))
                         for hh in range(GQA_KV_HEADS)], axis=1)
    kg_ref[...] = k.astype(BF16)
    v0 = k0 + GQA_KV_HEADS * HEAD_DIM
    v = proj[:, v0:v0 + GQA_KV_HEADS * HEAD_DIM]
    vg_ref[...] = v.astype(BF16)
    c0 = v0 + GQA_KV_HEADS * HEAD_DIM
    cq = _rms(proj[:, c0:c0 + MLA_Q_RANK], qan_ref[...])
    qm = _dot(cq.astype(BF16), wqb_ref[...])
    n_nope = MLA_HEADS * MLA_NOPE
    qmn_ref[...] = qm[:, :n_nope].astype(BF16)
    for hh in range(MLA_HEADS):
        sl = slice(hh * LANES, (hh + 1) * LANES)
        qmp_ref[:, sl] = rope_m(qm[:, n_nope + hh * LANES:n_nope + (hh + 1) * LANES]).astype(BF16)
    kv0 = c0 + MLA_Q_RANK
    ckv = _rms(proj[:, kv0:kv0 + MLA_KV_RANK], kvan_ref[...])
    kvm_ref[...] = _dot(ckv.astype(BF16), wkvb_ref[...]).astype(BF16)
    kpe_raw = proj[:, kv0 + MLA_KV_RANK:kv0 + MLA_KV_RANK + MLA_ROPE]
    kpe = rope_m(jnp.concatenate([kpe_raw, jnp.zeros((TOK, LANES - MLA_ROPE), F32)], axis=1))
    kpeb_ref[...] = kpe.astype(BF16)

    @pl.when(pl.program_id(0) < N_CTX_TILES)
    def _():
        kf_ref[...] = k
        vf_ref[...] = v
        ckvf_ref[...] = ckv
        kpef_ref[...] = kpe_raw


def _rope_tile(j):
    return jnp.where(j < N_CTX_TILES, 0, 1 + (j - N_CTX_TILES) % DEC_TILES_PER_REQ)


def _attn_pre(x, mod, consts, rope_tab):
    kvw = GQA_KV_HEADS * HEAD_DIM
    outs = [
        (GQA_HEADS * HEAD_DIM, BF16), (kvw, BF16), (kvw, BF16),
        (kvw, F32), (kvw, F32), (MLA_KV_RANK, F32), (MLA_ROPE, F32),
        (MLA_HEADS * MLA_NOPE, BF16), (MLA_HEADS * LANES, BF16),
        (MLA_HEADS * (MLA_NOPE + MLA_V), BF16), (LANES, BF16),
    ]
    out_specs = [pl.BlockSpec((TOK, w), (lambda j: (_ctx_tile(j), 0)) if dt == F32 else (lambda j: (j, 0)))
                 for w, dt in outs]
    out_shape = [jax.ShapeDtypeStruct((CTX_TOK if dt == F32 else N_TOK, w), dt) for w, dt in outs]
    x_specs, x_args = _x_in(x)
    return pl.pallas_call(
        functools.partial(_attn_pre_kernel, len(x_args)),
        grid=(N_TILES,),
        in_specs=x_specs + [mod[1]] + [s for _, s in consts]
        + [pl.BlockSpec((TOK, 5 * LANES), lambda j: (_rope_tile(j), 0))],
        out_specs=out_specs, out_shape=out_shape,
        compiler_params=_cparams(),
        name="attn_pre",
    )(*x_args, mod[0], *[a for a, _ in consts], rope_tab)


def _rows_matmul_kernel(a_ref, w_ref, o_ref):
    o_ref[...] = _dot(a_ref[...].astype(BF16), w_ref[...]).astype(o_ref.dtype)


def _mla_cache_kv(cache_ckv, layer_i, wkvb):
    per_req = PAST_LEN // TOK
    n = wkvb[0].shape[-1]
    return pl.pallas_call(
        _rows_matmul_kernel,
        grid=(DEC_BATCH * per_req,),
        in_specs=[pl.BlockSpec((None, None, TOK, MLA_KV_RANK), lambda g: (g // per_req, layer_i, g % per_req, 0)),
                  wkvb[1]],
        out_specs=pl.BlockSpec((TOK, n), lambda g: (g, 0)),
        out_shape=jax.ShapeDtypeStruct((DEC_BATCH * PAST_LEN, n), BF16),
        compiler_params=_cparams(),
        name="mla_cache_kv",
    )(cache_ckv, wkvb[0])


def _attend(scores, values, scale):
    m = scores[0].max(axis=-1, keepdims=True)
    for s in scores[1:]:
        m = jnp.maximum(m, s.max(axis=-1, keepdims=True))
    ps = [jnp.exp2((s - m) * (scale * LOG2_E)) for s in scores]
    l = ps[0].sum(axis=-1, keepdims=True)
    for p in ps[1:]:
        l = l + p.sum(axis=-1, keepdims=True)
    o = _dot(ps[0].astype(BF16), values[0])
    for p, v in zip(ps[1:], values[1:]):
        o = o + _dot(p.astype(BF16), v)
    return o / l


def _attn_heads(q_ref, qn_ref, qp_ref, segs, o_ref):
    group = GQA_HEADS // GQA_KV_HEADS
    for kh in range(GQA_KV_HEADS):
        ksl = slice(kh * HEAD_DIM, (kh + 1) * HEAD_DIM)
        heads = [slice(hh * HEAD_DIM, (hh + 1) * HEAD_DIM) for hh in range(kh * group, (kh + 1) * group)]
        q = jnp.concatenate([q_ref[:, sl] for sl in heads], axis=0)
        scores = [_dot_nt(q, s[0][:, ksl]) for s in segs]
        o = _attend(scores, [s[1][:, ksl] for s in segs], G_SCALE).astype(BF16)
        for g, sl in enumerate(heads):
            o_ref[:, sl] = o[g * TOK:(g + 1) * TOK]
    o0 = GQA_HEADS * HEAD_DIM
    v0 = MLA_HEADS * MLA_NOPE
    for hh in range(MLA_HEADS):
        sl = slice(hh * LANES, (hh + 1) * LANES)
        q = jnp.concatenate([qn_ref[:, sl], qp_ref[:, sl]], axis=1)
        scores = [_dot_nt(q, jnp.concatenate([s[2][:, sl], s[3][...]], axis=1)) for s in segs]
        vals = [s[2][:, v0 + hh * MLA_V:v0 + (hh + 1) * MLA_V] for s in segs]
        o_ref[:, o0 + hh * MLA_V:o0 + (hh + 1) * MLA_V] = _attend(scores, vals, M_SCALE).astype(BF16)


def _attn_kernel(q_ref, qn_ref, qp_ref,
                 ck_ref, cv_ref, ckvm_ref, ckpe_ref,
                 dk_ref, dv_ref, dkvm_ref, dkpe_ref,
                 pk_ref, pv_ref, pkvm_ref, pkpe_ref,
                 o_ref):
    j = pl.program_id(0)

    @pl.when(j < N_CTX_TILES)
    def _():
        _attn_heads(q_ref, qn_ref, qp_ref, [(ck_ref, cv_ref, ckvm_ref, ckpe_ref)], o_ref)

    @pl.when(j >= N_CTX_TILES)
    def _():
        _attn_heads(q_ref, qn_ref, qp_ref,
                    [(dk_ref, dv_ref, dkvm_ref, dkpe_ref), (pk_ref, pv_ref, pkvm_ref, pkpe_ref)], o_ref)


def _dec_req(j):
    return jnp.maximum(j - N_CTX_TILES, 0) // DEC_TILES_PER_REQ


def _attn(layer_i, qg, kg, vg, qmn, qmp, kvm, kpeb, pk, pv, pkvm, pkpe):
    tile = lambda w: pl.BlockSpec((TOK, w), lambda j: (j, 0))
    ctx = lambda w: pl.BlockSpec((TOK, w), lambda j: (jnp.minimum(j, N_CTX_TILES - 1), 0))
    dec = lambda w: pl.BlockSpec((DEC_SEQ, w), lambda j: (CTX_TOK // DEC_SEQ + _dec_req(j), 0))
    past = lambda w: pl.BlockSpec((PAST_LEN, w), lambda j: (_dec_req(j), 0))
    cache = lambda w: pl.BlockSpec((None, None, PAST_LEN, w), lambda j: (_dec_req(j), layer_i, 0, 0))
    kvw = GQA_KV_HEADS * HEAD_DIM
    mw = MLA_HEADS * (MLA_NOPE + MLA_V)
    return pl.pallas_call(
        _attn_kernel,
        grid=(N_TILES,),
        in_specs=[tile(GQA_HEADS * HEAD_DIM), tile(MLA_HEADS * MLA_NOPE), tile(MLA_HEADS * LANES),
                  ctx(kvw), ctx(kvw), ctx(mw), ctx(LANES),
                  dec(kvw), dec(kvw), dec(mw), dec(LANES),
                  cache(kvw), cache(kvw), past(mw), cache(LANES)],
        out_specs=tile(D_MODEL),
        out_shape=jax.ShapeDtypeStruct((N_TOK, D_MODEL), BF16),
        compiler_params=_cparams(),
        name="attention",
    )(qg, qmn, qmp, kg, vg, kvm, kpeb, kg, vg, kvm, kpeb, pk, pv, pkvm, pkpe)


def _residual_and_router(x, mix_out, m, g2_ref, wr_ref, x1_ref, hff_ref, aff_ref):
    x1 = x + m[2:3] * mix_out
    x1_ref[...] = x1
    hf = _rms(x1, g2_ref[...]) * (1.0 + m[4:5]) + m[3:4]
    hi = hf.astype(BF16)
    hff_ref[...] = hi
    lo = (hf - hi.astype(F32)).astype(BF16)
    r = _dot(hi, wr_ref[0]) + _dot(lo, wr_ref[1])
    logits = r + pltpu.roll(r, LANES - N_EXPERTS, 1)
    lane = lax.broadcasted_iota(jnp.int32, logits.shape, 1)
    logits = jnp.where(lane < N_EXPERTS, logits, -jnp.inf)
    e = jnp.exp(logits - logits.max(axis=-1, keepdims=True))
    aff_ref[...] = e / e.sum(axis=-1, keepdims=True)


def _epilogue_specs():
    tile = lambda w: pl.BlockSpec((TOK, w), lambda j: (j, 0))
    out_specs = [tile(D_MODEL), tile(D_MODEL), tile(LANES)]
    out_shape = [jax.ShapeDtypeStruct((N_TOK, D_MODEL), F32),
                 jax.ShapeDtypeStruct((N_TOK, D_MODEL), BF16),
                 jax.ShapeDtypeStruct((N_TOK, LANES), F32)]
    return out_specs, out_shape


def _attn_out_kernel(nx, *refs):
    o_ref, mod_ref, wout_ref, g2_ref, wr_ref, x1_ref, hff_ref, aff_ref = refs[nx:]
    out = _dot(o_ref[...], wout_ref[...])
    _residual_and_router(_x_tile(refs[:nx]), out, mod_ref[...], g2_ref, wr_ref, x1_ref, hff_ref, aff_ref)


def _attn_out(x, o, mod, consts):
    out_specs, out_shape = _epilogue_specs()
    x_specs, x_args = _x_in(x)
    return pl.pallas_call(
        functools.partial(_attn_out_kernel, len(x_args)),
        grid=(N_TILES,),
        in_specs=x_specs + [pl.BlockSpec((TOK, D_MODEL), lambda j: (j, 0)), mod[1]] + [s for _, s in consts],
        out_specs=out_specs, out_shape=out_shape,
        compiler_params=_cparams(),
        name="attn_out",
    )(*x_args, o, mod[0], *[a for a, _ in consts])


CONV_HALO = 16
SUBLANES = 8
CONV_BASE = CONV_HALO - CONV_WIDTH // 2
CONV_SH_ROWS = (CONV_BASE + CONV_WIDTH - 1) // SUBLANES * SUBLANES + TOK
CONV_RING = 4


def _cs_pre_kernel(x_ref, mod_ref, g1_ref, win_ref, w_ref, b_ref, lng_ref, lnb_ref, us_ref, o_ref,
                   ring_ref, pad_ref, sh_ref, u_ref):
    j = pl.program_id(0)
    slot = lambda d: (j + d) & (CONV_RING - 1)

    @pl.when(j == 0)
    def _():
        ring_ref[...] = jnp.zeros_like(ring_ref)

    @pl.when(j < N_TILES)
    def _():
        m = mod_ref[...]
        h = _rms(x_ref[...], g1_ref[...]) * (1.0 + m[1:2]) + m[0:1]
        proj = _dot(h.astype(BF16), win_ref[...])
        ring_ref[slot(0)] = proj[:, :CONV_DIM] * jax.nn.sigmoid(proj[:, CONV_DIM:2 * CONV_DIM])
        us_ref[...] = proj[:, 2 * CONV_DIM:]

    @pl.when(j >= 1)
    def _():
        c = j - 1
        q = (c - N_CTX_TILES) % DEC_TILES_PER_REQ
        has_prev = jnp.logical_and(c >= N_CTX_TILES, q != 0)
        has_next = jnp.logical_and(c >= N_CTX_TILES, q != DEC_TILES_PER_REQ - 1)
        pad_ref[0:CONV_HALO, :] = jnp.where(has_prev, ring_ref[slot(-2), TOK - CONV_HALO:, :], 0.0)
        pad_ref[CONV_HALO:CONV_HALO + TOK, :] = ring_ref[slot(-1)]
        pad_ref[CONV_HALO + TOK:, :] = jnp.where(has_next, ring_ref[slot(0), 0:CONV_HALO, :], 0.0)
        for s in range(1, SUBLANES):
            sh_ref[s - 1] = pad_ref[s:s + CONV_SH_ROWS, :]
        for lt in range(CONV_DIM // LANES):
            ls = slice(lt * LANES, (lt + 1) * LANES)
            acc = None
            for tap in range(CONV_WIDTH):
                a8, s = (CONV_BASE + tap) // SUBLANES * SUBLANES, (CONV_BASE + tap) % SUBLANES
                win = pad_ref[a8:a8 + TOK, ls] if s == 0 else sh_ref[s - 1, a8:a8 + TOK, ls]
                term = w_ref[tap:tap + 1, ls] * win
                acc = term if acc is None else acc + term
            u_ref[:, ls] = acc + b_ref[:, ls]
        u = u_ref[...]
        mu = jnp.mean(u, axis=-1, keepdims=True)
        uc = u - mu
        y = uc * lax.rsqrt(jnp.mean(uc * uc, axis=-1, keepdims=True) + EPS) * lng_ref[...] + lnb_ref[...]
        o_ref[...] = _silu(y).astype(BF16)


def _cs_pre(x, mod, consts):
    last = lambda j: jnp.minimum(j, N_TILES - 1)
    return pl.pallas_call(
        _cs_pre_kernel,
        grid=(N_TILES + 1,),
        in_specs=[pl.BlockSpec((TOK, D_MODEL), lambda j: (last(j), 0)), mod[1]] + [s for _, s in consts],
        out_specs=[pl.BlockSpec((TOK, S5_DIM), lambda j: (last(j), 0)),
                   pl.BlockSpec((TOK, CONV_DIM), lambda j: (jnp.maximum(j - 1, 0), 0))],
        out_shape=[jax.ShapeDtypeStruct((N_TOK, S5_DIM), F32),
                   jax.ShapeDtypeStruct((N_TOK, CONV_DIM), BF16)],
        scratch_shapes=[pltpu.VMEM((CONV_RING, TOK, CONV_DIM), F32),
                        pltpu.VMEM((TOK + 2 * CONV_HALO, CONV_DIM), F32),
                        pltpu.VMEM((SUBLANES - 1, CONV_SH_ROWS, CONV_DIM), F32),
                        pltpu.VMEM((TOK, CONV_DIM), F32)],
        compiler_params=_cparams(),
        name="cs_pre",
    )(x, mod[0], *[a for a, _ in consts])


S5_PITCH = 24
S5_SLABS = S5_DIM // LANES
S5_SCAN_COLS = 256


def _to_time_major(blk_ref, slab_ref, rows):
    for s in range(S5_SLABS):
        for j in range(rows):
            slab_ref[s, j * S5_PITCH:j * S5_PITCH + S5_TT, :] = blk_ref[j, :, s * LANES:(s + 1) * LANES]
    steps = [jnp.concatenate([slab_ref[s, pl.ds(t, rows, stride=S5_PITCH), :] for s in range(S5_SLABS)], axis=1)
             for t in range(S5_TT)]
    return jnp.concatenate(steps, axis=0)


def _from_time_major(y, slab_ref, out_ref, rows, c0):
    n_slabs = y.shape[1] // LANES
    for t in range(S5_TT):
        for s in range(n_slabs):
            slab_ref[s, pl.ds(t, rows, stride=S5_PITCH), :] = y[t * rows:(t + 1) * rows, s * LANES:(s + 1) * LANES]
    for s in range(n_slabs):
        for j in range(rows):
            out_ref[j, :, c0 + s * LANES:c0 + (s + 1) * LANES] = slab_ref[s, j * S5_PITCH:j * S5_PITCH + S5_TT, :]


def _s5_scan_quarter(lam_ref, st_ref, bu, d, q, rows, reverse):
    ncb = S5_QS // S5_SCAN_COLS
    states = [[None] * (2 * ncb) for _ in range(S5_TT)]
    s0 = q * 2 * S5_QS
    for cb in range(ncb):
        c_re = slice(cb * S5_SCAN_COLS, (cb + 1) * S5_SCAN_COLS)
        c_im = slice(S5_QS + cb * S5_SCAN_COLS, S5_QS + (cb + 1) * S5_SCAN_COLS)
        st_re = slice(s0 + c_re.start, s0 + c_re.stop)
        st_im = slice(s0 + c_im.start, s0 + c_im.stop)
        lr = lam_ref[d, q, 0:1, c_re]
        li = lam_ref[d, q, 1:2, c_re]
        sr = st_ref[d, :, st_re]
        si = st_ref[d, :, st_im]
        for k in range(S5_TT):
            t = (S5_TT - 1 - k) if reverse else k
            nr = lr * sr - li * si
            ni = lr * si + li * sr
            if bu is not None:
                nr = nr + bu[t * rows:(t + 1) * rows, c_re]
                ni = ni + bu[t * rows:(t + 1) * rows, c_im]
            sr, si = nr, ni
            states[t][cb] = sr
            states[t][ncb + cb] = si
        st_ref[d, :, st_re] = sr
        st_ref[d, :, st_im] = si
    return jnp.concatenate([jnp.concatenate(row, axis=1) for row in states], axis=0)


def _s5_kernel(rows, has_input, *refs):
    if has_input:
        uf_ref, ur_ref, wb_ref, dsk_ref = refs[:4]
        refs = refs[4:]
    wc_ref, lam_ref, s0_ref, yf_ref, yr_ref, fin_ref, st_ref, uslab_ref, yslab_ref = refs
    i = pl.program_id(0)

    @pl.when(i == 0)
    def _():
        st_ref[...] = s0_ref[...]

    for d, (y_ref, reverse) in enumerate(((yf_ref, False), (yr_ref, True))):
        if has_input:
            u = _to_time_major(ur_ref if reverse else uf_ref, uslab_ref, rows)
            ub = u.astype(BF16)
        for q in range(S5_Q):
            csl = slice(q * S5_QC, (q + 1) * S5_QC)
            bu = _dot(ub[:, csl], wb_ref[d, q]) if has_input else None
            s_all = _s5_scan_quarter(lam_ref, st_ref, bu, d, q, rows, reverse)
            y = _dot(s_all.astype(BF16), wc_ref[d, q])
            if has_input and not reverse:
                y = y + dsk_ref[:, csl] * u[:, csl]
            _from_time_major(y, yslab_ref.at[q % 2], y_ref, rows, q * S5_QC)

    @pl.when(i == S5_NB - 1)
    def _():
        fin_ref[...] = st_ref[...]


def _s5(rows, u3, wb, wc, lam, s0, dskip):
    fwd = pl.BlockSpec((rows, S5_TT, S5_DIM), lambda i: (0, i, 0))
    rev = pl.BlockSpec((rows, S5_TT, S5_DIM), lambda i: (0, S5_NB - 1 - i, 0))
    has_input = u3 is not None
    in_specs, args = [], []
    if has_input:
        in_specs += [fwd, rev, wb[1], dskip[1]]
        args += [u3, u3, wb[0], dskip[0]]
    in_specs += [wc[1], lam[1], s0[1]]
    args += [wc[0], lam[0], s0[0]]
    y_shape = jax.ShapeDtypeStruct((rows, TOK, S5_DIM), F32)
    st_shape = (2, rows, S5_STATE_COLS)
    return pl.pallas_call(
        functools.partial(_s5_kernel, rows, has_input),
        grid=(S5_NB,),
        in_specs=in_specs,
        out_specs=[fwd, rev, _const_spec(st_shape)],
        out_shape=[y_shape, y_shape, jax.ShapeDtypeStruct(st_shape, F32)],
        scratch_shapes=[pltpu.VMEM(st_shape, F32),
                        pltpu.VMEM((S5_SLABS, rows * S5_PITCH, LANES), F32),
                        pltpu.VMEM((2, S5_QC // LANES, rows * S5_PITCH, LANES), F32)],
        compiler_params=_cparams(),
        name="s5_scan" if has_input else "s5_carry_fix",
    )(*args)


def _gelu_tanh(x):
    return x * (0.5 * (1.0 + jnp.tanh(math.sqrt(2.0 / math.pi) * (x + 0.044715 * (x * x * x)))))


def _cs_post_kernel(uc_ref, yf_ref, yr_ref, cf_ref, cr_ref, x_ref, mod_ref, wglu_ref, bglu_ref, wout_ref,
                    g2_ref, wr_ref, x1_ref, hff_ref, aff_ref):
    y = _gelu_tanh(yf_ref[...] + yr_ref[...] + cf_ref[...] + cr_ref[...])
    y = y * jax.nn.sigmoid(_dot(y.astype(BF16), wglu_ref[...]) + bglu_ref[...])
    out = _dot(uc_ref[...], wout_ref[0:CONV_DIM, :]) + _dot(y.astype(BF16), wout_ref[CONV_DIM:, :])
    _residual_and_router(x_ref[...], out, mod_ref[...], g2_ref, wr_ref, x1_ref, hff_ref, aff_ref)


def _fix_slot(j, reverse):
    d = j - N_CTX_TILES
    r, q = d // DEC_TILES_PER_REQ, d % DEC_TILES_PER_REQ
    if reverse:
        ok, slot = q != DEC_TILES_PER_REQ - 1, r * (DEC_TILES_PER_REQ - 1) + q
    else:
        ok, slot = q != 0, r * (DEC_TILES_PER_REQ - 1) + q - 1
    return jnp.where(jnp.logical_and(j >= N_CTX_TILES, ok), slot, S5_FIX_ROWS - 1)


def _cs_post(uc, yf, yr, cf, cr, x, mod, consts):
    tile = lambda w: pl.BlockSpec((TOK, w), lambda j: (j, 0))
    out_specs, out_shape = _epilogue_specs()
    return pl.pallas_call(
        _cs_post_kernel,
        grid=(N_TILES,),
        in_specs=[tile(CONV_DIM), tile(S5_DIM), tile(S5_DIM),
                  pl.BlockSpec((TOK, S5_DIM), lambda j: (_fix_slot(j, False), 0)),
                  pl.BlockSpec((TOK, S5_DIM), lambda j: (_fix_slot(j, True), 0)),
                  tile(D_MODEL), mod[1]] + [s for _, s in consts],
        out_specs=out_specs, out_shape=out_shape,
        compiler_params=_cparams(),
        name="cs_post",
    )(uc, yf, yr, cf, cr, x, mod[0], *[a for a, _ in consts])


RANK_CHUNK = 128
GATHER_ROWS = 512


def _moe_select(cap, aff_ref, hff_ref, xs_ref, gs_ref, g_ref, rank_ref, p_ref):
    n = aff_ref.shape[0]
    nblk = n // RANK_CHUNK
    a_t = aff_ref[...].T
    sub = lax.broadcasted_iota(jnp.int32, (RANK_CHUNK, RANK_CHUNK), 0)
    lan = lax.broadcasted_iota(jnp.int32, (RANK_CHUNK, RANK_CHUNK), 1)
    earlier = sub < lan
    slot = lax.broadcasted_iota(jnp.int32, (cap, n), 0).astype(F32)
    rank_ref[...] = jnp.zeros_like(rank_ref)
    for e in range(N_EXPERTS):
        row = a_t[e:e + 1, :]
        cols = [aff_ref[c * RANK_CHUNK:(c + 1) * RANK_CHUNK, e:e + 1] for c in range(nblk)]
        for b in range(nblk):
            rb = row[:, b * RANK_CHUNK:(b + 1) * RANK_CHUNK]
            cnt = None
            for c in range(nblk):
                if c < b:
                    part = jnp.where(cols[c] >= rb, 1.0, 0.0)
                elif c > b:
                    part = jnp.where(cols[c] > rb, 1.0, 0.0)
                else:
                    tie = jnp.where(earlier, cols[c], -1.0) == rb
                    part = jnp.where(cols[c] > rb, 1.0, 0.0) + jnp.where(tie, 1.0, 0.0)
                cnt = part if cnt is None else cnt + part
            rank_ref[e:e + 1, b * RANK_CHUNK:(b + 1) * RANK_CHUNK] = jnp.sum(cnt, axis=0, keepdims=True)
        rank = rank_ref[e:e + 1, :]
        onehot = slot == rank
        p_ref[e * cap:(e + 1) * cap, :] = jnp.where(onehot, 1.0, 0.0).astype(BF16)
        gs_ref[e] = jnp.sum(jnp.where(onehot, row, 0.0), axis=1, keepdims=True)
    grp = GATHER_ROWS // cap
    for e0 in range(0, N_EXPERTS, grp):
        xs = _dot(p_ref[e0 * cap:(e0 + grp) * cap, :], hff_ref[...])
        for k in range(grp):
            xs_ref[e0 + k] = xs[k * cap:(k + 1) * cap].astype(BF16)
    rank_t = rank_ref[...].T
    per = LANES // cap
    lane = lax.broadcasted_iota(jnp.int32, (n, LANES), 1)
    slot_lane = (lane & (cap - 1)).astype(F32)
    for blk in range(N_EXPERTS // per):
        rc = rank_t[:, blk * per:blk * per + 1]
        for k in range(1, per):
            rc = jnp.where(lane >= k * cap, rank_t[:, blk * per + k:blk * per + k + 1], rc)
        g_ref[:, blk * LANES:(blk + 1) * LANES] = jnp.where(slot_lane == rc, 1.0, 0.0).astype(BF16)


def _moe_gather_kernel(affc_ref, hffc_ref, affd_ref, hffd_ref,
                       xsc_ref, gsc_ref, gc_ref, xsd_ref, gsd_ref, gd_ref,
                       rankc_ref, pc_ref, rankd_ref, pd_ref):
    s = pl.program_id(0)

    @pl.when(s < BATCH)
    def _():
        _moe_select(CAP_CTX, affc_ref, hffc_ref, xsc_ref, gsc_ref, gc_ref, rankc_ref, pc_ref)

    @pl.when(s >= BATCH)
    def _():
        _moe_select(CAP_DEC, affd_ref, hffd_ref, xsd_ref, gsd_ref, gd_ref, rankd_ref, pd_ref)


def _moe_gather(aff, hff):
    cidx = lambda s: jnp.minimum(s, BATCH - 1)
    didx = lambda s: jnp.maximum(s - BATCH, 0)
    dec0 = CTX_TOK // DEC_SEQ
    return pl.pallas_call(
        _moe_gather_kernel,
        grid=(BATCH + DEC_BATCH,),
        in_specs=[pl.BlockSpec((SEQ, LANES), lambda s: (cidx(s), 0)),
                  pl.BlockSpec((SEQ, D_MODEL), lambda s: (cidx(s), 0)),
                  pl.BlockSpec((DEC_SEQ, LANES), lambda s: (dec0 + didx(s), 0)),
                  pl.BlockSpec((DEC_SEQ, D_MODEL), lambda s: (dec0 + didx(s), 0))],
        out_specs=[pl.BlockSpec((N_EXPERTS, CAP_CTX, D_MODEL), lambda s: (0, cidx(s), 0)),
                   pl.BlockSpec((N_EXPERTS, CAP_CTX, 1), lambda s: (0, cidx(s), 0)),
                   pl.BlockSpec((SEQ, N_EXPERTS * CAP_CTX), lambda s: (cidx(s), 0)),
                   pl.BlockSpec((N_EXPERTS, CAP_DEC, D_MODEL), lambda s: (0, didx(s), 0)),
                   pl.BlockSpec((N_EXPERTS, CAP_DEC, 1), lambda s: (0, didx(s), 0)),
                   pl.BlockSpec((DEC_SEQ, N_EXPERTS * CAP_DEC), lambda s: (didx(s), 0))],
        out_shape=[jax.ShapeDtypeStruct((N_EXPERTS, XS_CTX_ROWS, D_MODEL), BF16),
                   jax.ShapeDtypeStruct((N_EXPERTS, XS_CTX_ROWS, 1), F32),
                   jax.ShapeDtypeStruct((CTX_TOK, N_EXPERTS * CAP_CTX), BF16),
                   jax.ShapeDtypeStruct((N_EXPERTS, XS_DEC_ROWS, D_MODEL), BF16),
                   jax.ShapeDtypeStruct((N_EXPERTS, XS_DEC_ROWS, 1), F32),
                   jax.ShapeDtypeStruct((DEC_BATCH * DEC_SEQ, N_EXPERTS * CAP_DEC), BF16)],
        scratch_shapes=[pltpu.VMEM((LANES, SEQ), F32), pltpu.VMEM((N_EXPERTS * CAP_CTX, SEQ), BF16),
                        pltpu.VMEM((LANES, DEC_SEQ), F32), pltpu.VMEM((N_EXPERTS * CAP_DEC, DEC_SEQ), BF16)],
        compiler_params=_cparams(),
        name="moe_gather",
    )(aff, hff, aff, hff)


FF_CHUNK = 512
N_FF_CHUNKS = EXPERT_FF // FF_CHUNK
assert N_FF_CHUNKS >= 2


def _moe_ffn_kernel(xsc_ref, xsd_ref, gsc_ref, gsd_ref, wg_ref, wu_ref, wd_ref, yc_ref, yd_ref, acc_ref):
    f = pl.program_id(1)
    groups = ((xsc_ref, gsc_ref, yc_ref, 0, BATCH, CAP_CTX), (xsd_ref, gsd_ref, yd_ref, XS_CTX_ROWS, DEC_BATCH, CAP_DEC))

    def chunk(consume):
        wg = wg_ref[...].astype(BF16)
        wu = wu_ref[...].astype(BF16)
        wd = wd_ref[...].astype(BF16)
        for grp in groups:
            xs = grp[0][...]
            act = _silu(_dot(xs, wg)) * _dot(xs, wu)
            consume(grp, _dot(act.astype(BF16), wd))

    def rows_of(grp):
        return slice(grp[3], grp[3] + grp[4] * grp[5])

    @pl.when(f == 0)
    def _():
        def first(grp, part):
            acc_ref[rows_of(grp), :] = part
        chunk(first)

    if N_FF_CHUNKS > 2:
        @pl.when(jnp.logical_and(f > 0, f < N_FF_CHUNKS - 1))
        def _():
            def middle(grp, part):
                acc_ref[rows_of(grp), :] += part
            chunk(middle)

    @pl.when(f == N_FF_CHUNKS - 1)
    def _():
        def last(grp, part):
            _, gs_ref, y_ref, _, n_req, cap = grp
            y = ((acc_ref[rows_of(grp), :] + part) * gs_ref[...]).astype(BF16)
            for r in range(n_req):
                y_ref[r] = y[r * cap:(r + 1) * cap]
        chunk(last)


def _moe_ffn(layer, xsc, xsd, gsc, gsd, w_gate, w_up, w_down):
    per_e = lambda rows, w: pl.BlockSpec((None, rows, w), lambda e, f: (e, 0, 0))
    return pl.pallas_call(
        _moe_ffn_kernel,
        grid=(N_EXPERTS, N_FF_CHUNKS),
        in_specs=[per_e(XS_CTX_ROWS, D_MODEL), per_e(XS_DEC_ROWS, D_MODEL),
                  per_e(XS_CTX_ROWS, 1), per_e(XS_DEC_ROWS, 1),
                  pl.BlockSpec((None, None, D_MODEL, FF_CHUNK), lambda e, f: (layer, e, 0, f)),
                  pl.BlockSpec((None, None, D_MODEL, FF_CHUNK), lambda e, f: (layer, e, 0, f)),
                  pl.BlockSpec((None, None, FF_CHUNK, D_MODEL), lambda e, f: (layer, e, f, 0))],
        out_specs=[pl.BlockSpec((BATCH, CAP_CTX, D_MODEL), lambda e, f: (0, e, 0)),
                   pl.BlockSpec((DEC_BATCH, CAP_DEC, D_MODEL), lambda e, f: (0, e, 0))],
        out_shape=[jax.ShapeDtypeStruct((BATCH, N_EXPERTS * CAP_CTX, D_MODEL), BF16),
                   jax.ShapeDtypeStruct((DEC_BATCH, N_EXPERTS * CAP_DEC, D_MODEL), BF16)],
        scratch_shapes=[pltpu.VMEM((XS_CTX_ROWS + XS_DEC_ROWS, D_MODEL), F32)],
        compiler_params=_cparams(2),
        name="moe_ffn",
    )(xsc, xsd, gsc, gsd, w_gate, w_up, w_down)


def _moe_combine_kernel(final, gc_ref, yc_ref, gd_ref, yd_ref, x1_ref, mod_ref, fg_ref, *o_refs):
    j = pl.program_id(0)

    def finish(comb, o_ref):
        x2 = x1_ref[...] + mod_ref[5:6, :] * comb
        o_ref[...] = _rms(x2, fg_ref[...]) if final else x2

    @pl.when(j < N_CTX_TILES)
    def _():
        finish(_dot(gc_ref[...], yc_ref[...]), o_refs[0])

    @pl.when(j >= N_CTX_TILES)
    def _():
        finish(_dot(gd_ref[...], yd_ref[...]), o_refs[-1])


def _moe_combine(final, gc, yc, gd, yd, x1, mod, fg):
    if final:
        out_specs = [pl.BlockSpec((TOK, D_MODEL), lambda j: (_ctx_tile(j), 0)),
                     pl.BlockSpec((TOK, D_MODEL), lambda j: (_dec_tile(j), 0))]
        out_shape = [jax.ShapeDtypeStruct((CTX_TOK, D_MODEL), F32),
                     jax.ShapeDtypeStruct((N_TOK - CTX_TOK, D_MODEL), F32)]
    else:
        out_specs = pl.BlockSpec((TOK, D_MODEL), lambda j: (j, 0))
        out_shape = jax.ShapeDtypeStruct((N_TOK, D_MODEL), F32)
    return pl.pallas_call(
        functools.partial(_moe_combine_kernel, final),
        grid=(N_TILES,),
        in_specs=[pl.BlockSpec((TOK, N_EXPERTS * CAP_CTX), lambda j: (_ctx_tile(j), 0)),
                  pl.BlockSpec((None, N_EXPERTS * CAP_CTX, D_MODEL), lambda j: (_ctx_tile(j), 0, 0)),
                  pl.BlockSpec((TOK, N_EXPERTS * CAP_DEC), lambda j: (_dec_tile(j), 0)),
                  pl.BlockSpec((None, N_EXPERTS * CAP_DEC, D_MODEL),
                               lambda j: (_dec_tile(j) // DEC_TILES_PER_REQ, 0, 0)),
                  pl.BlockSpec((TOK, D_MODEL), lambda j: (j, 0)),
                  mod[1], _const_spec((1, D_MODEL))],
        out_specs=out_specs, out_shape=out_shape,
        compiler_params=_cparams(),
        name="moe_combine",
    )(gc, yc, gd, yd, x1, mod[0], fg)


def _moe(layer, final, x1, hff, aff, mod, w_gate, w_up, w_down, fg):
    xsc, gsc, gc, xsd, gsd, gd = _moe_gather(aff, hff)
    yc, yd = _moe_ffn(layer, xsc, xsd, gsc, gsd, w_gate, w_up, w_down)
    return _moe_combine(final, gc, yc, gd, yd, x1, mod, fg)


def _axial_rope(n_tokens, rot_dim):
    rows = n_tokens // GRID_W
    per_axis = rot_dim // 4
    freqs = ROPE_THETA ** (-jnp.arange(per_axis, dtype=F32) / per_axis)
    row = jnp.repeat(jnp.arange(rows, dtype=F32), GRID_W)
    col = jnp.tile(jnp.arange(GRID_W, dtype=F32), rows)
    ang = jnp.concatenate([row[:, None] * freqs, col[:, None] * freqs], axis=-1)
    return jnp.cos(ang), jnp.sin(ang)


def _rope_table():
    cg, sg = _axial_rope(DEC_SEQ, HEAD_DIM)
    cm, sm = _axial_rope(DEC_SEQ, MLA_ROPE)
    z = jnp.zeros_like(cm)
    pos = jnp.concatenate([
        jnp.concatenate([cg, cg], -1), jnp.concatenate([-sg, sg], -1),
        jnp.concatenate([cm, cm, z, z], -1), jnp.concatenate([-sm, z, z, z], -1),
        jnp.concatenate([z, sm, z, z], -1)], axis=-1)
    one, zero = jnp.ones((TOK, LANES), F32), jnp.zeros((TOK, LANES), F32)
    ident = jnp.concatenate([one, zero, one, zero, zero], axis=-1)
    return jnp.concatenate([ident, pos], axis=0)


def _router_weights(w):
    hi = w.astype(BF16)
    lo = (w - hi.astype(F32)).astype(BF16)
    z = jnp.zeros(w.shape[:-1] + (LANES - 2 * N_EXPERTS,), BF16)
    return jnp.stack([jnp.concatenate([hi, lo, z], axis=-1),
                      jnp.concatenate([hi, jnp.zeros_like(lo), z], axis=-1)], axis=1)


def _mla_weights(w_qb, w_kvb):
    n = w_qb.shape[0]
    qb = w_qb.astype(BF16).reshape(n, MLA_Q_RANK, MLA_HEADS, MLA_NOPE + MLA_ROPE)
    qb_pe = jnp.pad(qb[..., MLA_NOPE:], ((0, 0), (0, 0), (0, 0), (0, LANES - MLA_ROPE)))
    wqb = jnp.concatenate([qb[..., :MLA_NOPE].reshape(n, MLA_Q_RANK, -1), qb_pe.reshape(n, MLA_Q_RANK, -1)], axis=-1)
    kvb = w_kvb.astype(BF16).reshape(n, MLA_KV_RANK, MLA_HEADS, MLA_NOPE + MLA_V)
    wkvb = jnp.concatenate([kvb[..., :MLA_NOPE].reshape(n, MLA_KV_RANK, -1),
                            kvb[..., MLA_NOPE:].reshape(n, MLA_KV_RANK, -1)], axis=-1)
    return wqb, wkvb


def _s5_weights(a_re, a_im, log_step, b_re, b_im, c_re, c_im):
    n_ld = a_re.shape[0] * a_re.shape[1]
    flat = lambda v: v.astype(F32).reshape((n_ld,) + v.shape[2:])
    a_re, a_im, log_step, b_re, b_im, c_re, c_im = map(flat, (a_re, a_im, log_step, b_re, b_im, c_re, c_im))
    ar, ai = a_re, a_im
    step = jnp.exp(log_step)[..., None]

    def cexp(k):
        mag = jnp.exp(k * ar * step)
        return mag * jnp.cos(k * ai * step), mag * jnp.sin(k * ai * step)

    lr, li = cexp(1.0)
    den = ar * ar + ai * ai
    gr = ((lr - 1.0) * ar + li * ai) / den
    gi = (li * ar - (lr - 1.0) * ai) / den
    br, bi = b_re, b_im
    bbr = gr[..., None] * br - gi[..., None] * bi
    bbi = gr[..., None] * bi + gi[..., None] * br
    gq = S5_GROUPS // S5_Q
    group_of_col = (np.arange(2 * S5_QS) % S5_QS) // S5_STATE
    own = jnp.asarray(np.arange(gq)[:, None] == group_of_col[None, :])

    def rows_in(m):
        m = m.reshape(n_ld, S5_Q, gq, S5_STATE, S5_GROUP).transpose(0, 1, 4, 2, 3)
        return m.reshape(n_ld, S5_Q, S5_GROUP, S5_QS)

    def cols_out(m):
        m = m.reshape(n_ld, S5_Q, gq, S5_GROUP, S5_STATE).transpose(0, 1, 2, 4, 3)
        return m.reshape(n_ld, S5_Q, S5_QS, S5_GROUP)

    b_rows = jnp.concatenate([rows_in(bbr), rows_in(bbi)], axis=-1).astype(BF16)
    wb = jnp.where(own[None, None, :, None, :], b_rows[:, :, None, :, :], 0)
    c_cols = jnp.concatenate([cols_out(c_re), cols_out(-c_im)], axis=-2).astype(BF16)
    wc = jnp.where(own.T[None, None, :, :, None], c_cols[:, :, :, None, :], 0)
    to_q = lambda v: v.reshape(n_ld, S5_Q, S5_QS)
    lam_q = jnp.stack([to_q(lr), to_q(li)], axis=2)
    cr_, ci_ = cexp(float(TOK))
    lam_chunk = jnp.stack([to_q(cr_), to_q(ci_)], axis=2)
    per_layer = lambda v, tail: v.reshape((n_ld // 2, 2) + tail)
    return (per_layer(wb, (S5_Q, S5_QC, 2 * S5_QS)), per_layer(wc, (S5_Q, 2 * S5_QS, S5_QC)),
            per_layer(lam_q, (S5_Q, 2, S5_QS)), per_layer(lam_chunk, (S5_STATE_COLS,)))


def _cmul_cols(a, b):
    a4 = a.reshape(a.shape[:-1] + (S5_Q, 2, S5_QS))
    b4 = b.reshape(b.shape[:-1] + (S5_Q, 2, S5_QS))
    re = a4[..., 0, :] * b4[..., 0, :] - a4[..., 1, :] * b4[..., 1, :]
    im = a4[..., 0, :] * b4[..., 1, :] + a4[..., 1, :] * b4[..., 0, :]
    return jnp.stack([re, im], axis=-2).reshape(re.shape[:-2] + (S5_STATE_COLS,))


def _s5_initial_states(state_s5):
    ns = state_s5.shape[1]
    h0 = state_s5.reshape(DEC_BATCH, ns, 2, 2, S5_Q, S5_QS).transpose(1, 2, 0, 4, 3, 5)
    h0 = h0.reshape(ns, 2, DEC_BATCH, S5_STATE_COLS)
    first = N_CTX_TILES + DEC_TILES_PER_REQ * np.arange(DEC_BATCH)
    s0 = jnp.zeros((ns, 2, N_TILES, S5_STATE_COLS), F32)
    return s0.at[:, 0, first].set(h0[:, 0]).at[:, 1, first + DEC_TILES_PER_REQ - 1].set(h0[:, 1])


def _s5_branch(us, wb, wc, lam_q, s0, dskip, lam_chunk):
    yf, yr, fin = _s5(N_TILES, us.reshape(N_TILES, TOK, S5_DIM), wb, wc, lam_q, s0, dskip)
    fz = fin[:, N_CTX_TILES:].reshape(2, DEC_BATCH, DEC_TILES_PER_REQ, S5_STATE_COLS)
    f1 = fz[0, :, 0]
    f2 = fz[0, :, 1] + _cmul_cols(lam_chunk[0], f1)
    f3 = fz[0, :, 2] + _cmul_cols(lam_chunk[0], f2)
    r2 = fz[1, :, 3]
    r1 = fz[1, :, 2] + _cmul_cols(lam_chunk[1], r2)
    r0 = fz[1, :, 1] + _cmul_cols(lam_chunk[1], r1)
    pad = jnp.zeros((S5_FIX_ROWS - DEC_BATCH * (DEC_TILES_PER_REQ - 1), S5_STATE_COLS), F32)
    sin_f = jnp.concatenate([jnp.stack([f1, f2, f3], 1).reshape(-1, S5_STATE_COLS), pad])
    sin_r = jnp.concatenate([jnp.stack([r0, r1, r2], 1).reshape(-1, S5_STATE_COLS), pad])
    sin = jnp.stack([sin_f, sin_r])
    cf, cr, _ = _s5(S5_FIX_ROWS, None, None, wc, lam_q, (sin, _const_spec(sin.shape)), None)
    rows2d = lambda a: a.reshape(-1, S5_DIM)
    new_state = fin[:, :N_CTX_TILES].reshape(2, BATCH, S5_Q, 2, S5_QS).transpose(1, 0, 3, 2, 4)
    new_state = new_state.reshape(BATCH, 2, 2, S5_GROUPS, S5_STATE)
    return rows2d(yf), rows2d(yr), rows2d(cf), rows2d(cr), new_state


def kernel(x_prompt, x_sample, cache_gqa_k, cache_gqa_v, cache_mla_ckv, cache_mla_kpe, state_s5, c, c_ctx, w_mod, b_mod, norm1_g, norm2_g, attn_w_in, gqa_q_norm, gqa_k_norm, mla_qa_norm, mla_w_qb, mla_kva_norm, mla_w_kvb, attn_w_out, cs_w_in, conv_w, conv_b, conv_ln_g, conv_ln_b, s5_a_re, s5_a_im, s5_log_step, s5_b_re, s5_b_im, s5_c_re, s5_c_im, s5_d, s5_w_glu, s5_b_glu, cs_w_out, moe_router, moe_w_gate, moe_w_up, moe_w_down, final_norm_g):
    x = (x_prompt.reshape(CTX_TOK, D_MODEL), x_sample.reshape(DEC_BATCH * DEC_SEQ, D_MODEL))
    cond8 = jnp.concatenate([c_ctx[None, :], c, jnp.zeros((8 - 1 - DEC_BATCH, D_MODEL), F32)])
    mod = _modulation(cond8, w_mod, b_mod)
    tile_row = np.concatenate([np.zeros(N_CTX_TILES, np.int32),
                               1 + np.repeat(np.arange(DEC_BATCH, dtype=np.int32), DEC_TILES_PER_REQ)])
    modt = mod[:, tile_row].reshape(DEPTH, N_TILES, 6, D_MODEL)
    rope_tab = _rope_table()
    fg = final_norm_g.reshape(1, D_MODEL)
    rows3 = lambda v: v.reshape(v.shape[0], 1, v.shape[-1])
    norm1, norm2 = rows3(norm1_g), rows3(norm2_g)
    attn_w_in_b, attn_w_out_b = attn_w_in.astype(BF16), attn_w_out.astype(BF16)
    cs_w_in_b, cs_w_out_b, w_glu_b = cs_w_in.astype(BF16), cs_w_out.astype(BF16), s5_w_glu.astype(BF16)
    wr_all = _router_weights(moe_router)
    wqb_all, wkvb_all = _mla_weights(mla_w_qb, mla_w_kvb)
    qn, kn, qan, kvan = rows3(gqa_q_norm), rows3(gqa_k_norm), rows3(mla_qa_norm), rows3(mla_kva_norm)
    n_attn = cache_gqa_k.shape[1]
    pk_all = cache_gqa_k.reshape(DEC_BATCH, n_attn, PAST_LEN, -1).astype(BF16)
    pv_all = cache_gqa_v.reshape(DEC_BATCH, n_attn, PAST_LEN, -1).astype(BF16)
    pkpe_all = jnp.pad(cache_mla_kpe.astype(BF16), ((0, 0), (0, 0), (0, 0), (0, LANES - MLA_ROPE)))
    conv_b3, ln_g3, ln_b3, dskip3, b_glu3 = map(rows3, (conv_b, conv_ln_g, conv_ln_b, s5_d, s5_b_glu))
    wb_all, wc_all, lam_all, lam_chunk_all = _s5_weights(s5_a_re, s5_a_im, s5_log_step, s5_b_re, s5_b_im,
                                                         s5_c_re, s5_c_im)
    s0_all = _s5_initial_states(state_s5)
    new_k, new_v, new_ckv, new_kpe, new_s5 = [], [], [], [], []
    for l in range(DEPTH):
        i = l // 2
        mod_l = _mod_operand(modt, l)
        if l % 2 == 0:
            wkvb = _sel(wkvb_all, i)
            (qg, kg, vg, kf, vf, ckvf, kpef, qmn, qmp, kvm, kpeb) = _attn_pre(
                x, mod_l, [_sel(norm1, l), _sel(attn_w_in_b, i), _sel(qn, i), _sel(kn, i), _sel(qan, i),
                           _sel(kvan, i), _sel(wqb_all, i), wkvb], rope_tab)
            pkvm = _mla_cache_kv(cache_mla_ckv, i, wkvb)
            o = _attn(i, qg, kg, vg, qmn, qmp, kvm, kpeb, pk_all, pv_all, pkvm, pkpe_all)
            x1, hff, aff = _attn_out(x, o, mod_l, [_sel(attn_w_out_b, i), _sel(norm2, l), _sel(wr_all, l)])
            new_k.append(kf.reshape(BATCH, SEQ, GQA_KV_HEADS, HEAD_DIM))
            new_v.append(vf.reshape(BATCH, SEQ, GQA_KV_HEADS, HEAD_DIM))
            new_ckv.append(ckvf.reshape(BATCH, SEQ, MLA_KV_RANK))
            new_kpe.append(kpef.reshape(BATCH, SEQ, MLA_ROPE))
        else:
            us, uc = _cs_pre(x, mod_l, [_sel(norm1, l), _sel(cs_w_in_b, i), _sel(conv_w, i), _sel(conv_b3, i),
                                        _sel(ln_g3, i), _sel(ln_b3, i)])
            yf, yr, cf, cr, ns = _s5_branch(us, _sel(wb_all, i), _sel(wc_all, i), _sel(lam_all, i),
                                            _sel(s0_all, i), _sel(dskip3, i), lam_chunk_all[i])
            new_s5.append(ns)
            x1, hff, aff = _cs_post(uc, yf, yr, cf, cr, x, mod_l,
                                    [_sel(w_glu_b, i), _sel(b_glu3, i), _sel(cs_w_out_b, i), _sel(norm2, l),
                                     _sel(wr_all, l)])
        x = _moe(l, l == DEPTH - 1, x1, hff, aff, mod_l, moe_w_gate, moe_w_up, moe_w_down, fg)
    y_prompt = x[0].reshape(BATCH, SEQ, D_MODEL)
    y_sample = x[1].reshape(DEC_BATCH, DEC_SEQ, D_MODEL)
    return (y_prompt, y_sample, jnp.stack(new_k, axis=1), jnp.stack(new_v, axis=1),
            jnp.stack(new_ckv, axis=1), jnp.stack(new_kpe, axis=1), jnp.stack(new_s5, axis=1))
```

```python
import functools
import math

import jax
import jax.numpy as jnp
import numpy as np
from jax import lax
from jax.experimental import pallas as pl
from jax.experimental.pallas import tpu as pltpu

F32 = jnp.float32
BF16 = jnp.bfloat16

D_MODEL = 2048
BATCH = 16
SEQ = 256
DEPTH = 4
DEC_BATCH = 2
DEC_SEQ = 1024
PAST_LEN = 512
GRID_W = 64
HEAD_DIM = 128
ROPE_THETA = 10000.0
EPS = 1e-6
GQA_HEADS = 8
GQA_KV_HEADS = 2
MLA_HEADS = 8
MLA_Q_RANK = 512
MLA_KV_RANK = 256
MLA_NOPE = 128
MLA_ROPE = 64
MLA_V = 128
CONV_DIM = D_MODEL // 2
CONV_WIDTH = 31
S5_DIM = D_MODEL // 2
S5_GROUP = 16
S5_GROUPS = S5_DIM // S5_GROUP
S5_STATE = 64
N_EXPERTS = 16
EXPERT_FF = 1024
EC_CAPACITY = 2

LANES = 128
VMEM_LIMIT = 56 * 1024 * 1024

TOK = 256
N_CTX_TILES = BATCH * SEQ // TOK
DEC_TILES_PER_REQ = DEC_SEQ // TOK
N_DEC_TILES = DEC_BATCH * DEC_TILES_PER_REQ
N_TILES = N_CTX_TILES + N_DEC_TILES
N_TOK = N_TILES * TOK
CTX_TOK = N_CTX_TILES * TOK
CAP_CTX = EC_CAPACITY * SEQ // N_EXPERTS
CAP_DEC = EC_CAPACITY * DEC_SEQ // N_EXPERTS
XS_CTX_ROWS = BATCH * CAP_CTX
XS_DEC_ROWS = DEC_BATCH * CAP_DEC

G_SCALE = HEAD_DIM ** -0.5
M_SCALE = (MLA_NOPE + MLA_ROPE) ** -0.5
LOG2_E = math.log2(math.e)

S5_TT = 16
S5_NB = TOK // S5_TT
S5_Q = 4
S5_QC = S5_DIM // S5_Q
S5_QS = S5_GROUPS // S5_Q * S5_STATE
S5_STATE_COLS = S5_Q * 2 * S5_QS
S5_FIX_ROWS = 8

_NT = (((1,), (1,)), ((), ()))


def _cparams(n_grid_dims=1):
    return pltpu.CompilerParams(dimension_semantics=("arbitrary",) * n_grid_dims,
                                vmem_limit_bytes=VMEM_LIMIT)


def _const_spec(shape):
    nd = len(shape)
    return pl.BlockSpec(shape, lambda *_: (0,) * nd)


def _sel(stacked, *idx, single_buffer=False):
    rest = stacked.shape[len(idx):]
    return stacked, pl.BlockSpec((None,) * len(idx) + rest, lambda *_: tuple(idx) + (0,) * len(rest),
                                 pipeline_mode=pl.Buffered(1) if single_buffer else None)


def _mod_operand(modt_all, layer):
    return modt_all, pl.BlockSpec((None, None, 6, D_MODEL),
                                  lambda j: (layer, jnp.minimum(j, N_TILES - 1), 0, 0))


def _rms(x, g):
    return x * lax.rsqrt(jnp.mean(x * x, axis=-1, keepdims=True) + EPS) * g


def _silu(x):
    return x * jax.nn.sigmoid(x)


def _dot(a, b):
    return jnp.dot(a, b, preferred_element_type=F32)


def _dot_nt(a, b):
    return lax.dot_general(a, b, _NT, preferred_element_type=F32)


def _ctx_tile(j):
    return jnp.minimum(j, N_CTX_TILES - 1)


def _dec_tile(j):
    return jnp.maximum(j - N_CTX_TILES, 0)


def _x_in(x):
    if isinstance(x, tuple):
        return ([pl.BlockSpec((TOK, D_MODEL), lambda j: (_ctx_tile(j), 0)),
                 pl.BlockSpec((TOK, D_MODEL), lambda j: (_dec_tile(j), 0))], list(x))
    return [pl.BlockSpec((TOK, D_MODEL), lambda j: (j, 0))], [x]


def _x_tile(x_refs):
    if len(x_refs) == 2:
        return jnp.where(pl.program_id(0) < N_CTX_TILES, x_refs[0][...], x_refs[1][...])
    return x_refs[0][...]


MOD_TN = 1024


def _mod_kernel(cond_ref, w_ref, b_ref, o_ref):
    a = _silu(cond_ref[...]).astype(BF16)
    o_ref[...] = _dot(a, w_ref[...].astype(BF16)) + b_ref[...]


def _modulation(cond8, w_mod, b_mod):
    n_out = 6 * D_MODEL
    return pl.pallas_call(
        _mod_kernel,
        grid=(DEPTH, n_out // MOD_TN),
        in_specs=[
            pl.BlockSpec((8, D_MODEL), lambda l, n: (0, 0)),
            pl.BlockSpec((None, D_MODEL, MOD_TN), lambda l, n: (l, 0, n)),
            pl.BlockSpec((None, 1, MOD_TN), lambda l, n: (l, 0, n)),
        ],
        out_specs=pl.BlockSpec((None, 8, MOD_TN), lambda l, n: (l, 0, n)),
        out_shape=jax.ShapeDtypeStruct((DEPTH, 8, n_out), F32),
        compiler_params=_cparams(2),
        name="modulation",
    )(cond8, w_mod, b_mod.reshape(DEPTH, 1, n_out))


def _attn_pre_kernel(nx, *refs):
    (mod_ref, g1_ref, win_ref, qn_ref, kn_ref, qan_ref, kvan_ref, wqb_ref, wkvb_ref, rope_ref,
     qg_ref, kg_ref, vg_ref, kf_ref, vf_ref, ckvf_ref, kpef_ref, qmn_ref, qmp_ref, kvm_ref, kpeb_ref) = refs[nx:]
    m = mod_ref[...]
    h = _rms(_x_tile(refs[:nx]), g1_ref[...]) * (1.0 + m[1:2]) + m[0:1]
    proj = _dot(h.astype(BF16), win_ref[...])
    rope = rope_ref[...]
    ga, gb, ma, mb, md = [rope[:, i * LANES:(i + 1) * LANES] for i in range(5)]

    def rope_g(xh):
        return xh * ga + pltpu.roll(xh, 64, 1) * gb

    def rope_m(xh):
        return xh * ma + pltpu.roll(xh, 96, 1) * mb + pltpu.roll(xh, 32, 1) * md

    for hh in range(GQA_HEADS):
        sl = slice(hh * HEAD_DIM, (hh + 1) * HEAD_DIM)
        qg_ref[:, sl] = rope_g(_rms(proj[:, sl], qn_ref[...])).astype(BF16)
    k0 = GQA_HEADS * HEAD_DIM
    k = jnp.concatenate([rope_g(_rms(proj[:, k0 + hh * HEAD_DIM:k0 + (hh + 1) * HEAD_DIM], kn_ref[...]))
                         for hh in range(GQA_KV_HEADS)], axis=1)
    kg_ref[...] = k.astype(BF16)
    v0 = k0 + GQA_KV_HEADS * HEAD_DIM
    v = proj[:, v0:v0 + GQA_KV_HEADS * HEAD_DIM]
    vg_ref[...] = v.astype(BF16)
    c0 = v0 + GQA_KV_HEADS * HEAD_DIM
    cq = _rms(proj[:, c0:c0 + MLA_Q_RANK], qan_ref[...])
    qm = _dot(cq.astype(BF16), wqb_ref[...])
    n_nope = MLA_HEADS * MLA_NOPE
    qmn_ref[...] = qm[:, :n_nope].astype(BF16)
    for hh in range(MLA_HEADS):
        sl = slice(hh * LANES, (hh + 1) * LANES)
        qmp_ref[:, sl] = rope_m(qm[:, n_nope + hh * LANES:n_nope + (hh + 1) * LANES]).astype(BF16)
    kv0 = c0 + MLA_Q_RANK
    ckv = _rms(proj[:, kv0:kv0 + MLA_KV_RANK], kvan_ref[...])
    kvm_ref[...] = _dot(ckv.astype(BF16), wkvb_ref[...]).astype(BF16)
    kpe_raw = proj[:, kv0 + MLA_KV_RANK:kv0 + MLA_KV_RANK + MLA_ROPE]
    kpe = rope_m(jnp.concatenate([kpe_raw, jnp.zeros((TOK, LANES - MLA_ROPE), F32)], axis=1))
    kpeb_ref[...] = kpe.astype(BF16)

    @pl.when(pl.program_id(0) < N_CTX_TILES)
    def _():
        kf_ref[...] = k
        vf_ref[...] = v
        ckvf_ref[...] = ckv
        kpef_ref[...] = kpe_raw


def _rope_tile(j):
    return jnp.where(j < N_CTX_TILES, 0, 1 + (j - N_CTX_TILES) % DEC_TILES_PER_REQ)


def _attn_pre(x, mod, consts, rope_tab):
    kvw = GQA_KV_HEADS * HEAD_DIM
    outs = [
        (GQA_HEADS * HEAD_DIM, BF16), (kvw, BF16), (kvw, BF16),
        (kvw, F32), (kvw, F32), (MLA_KV_RANK, F32), (MLA_ROPE, F32),
        (MLA_HEADS * MLA_NOPE, BF16), (MLA_HEADS * LANES, BF16),
        (MLA_HEADS * (MLA_NOPE + MLA_V), BF16), (LANES, BF16),
    ]
    out_specs = [pl.BlockSpec((TOK, w), (lambda j: (_ctx_tile(j), 0)) if dt == F32 else (lambda j: (j, 0)))
                 for w, dt in outs]
    out_shape = [jax.ShapeDtypeStruct((CTX_TOK if dt == F32 else N_TOK, w), dt) for w, dt in outs]
    x_specs, x_args = _x_in(x)
    return pl.pallas_call(
        functools.partial(_attn_pre_kernel, len(x_args)),
        grid=(N_TILES,),
        in_specs=x_specs + [mod[1]] + [s for _, s in consts]
        + [pl.BlockSpec((TOK, 5 * LANES), lambda j: (_rope_tile(j), 0))],
        out_specs=out_specs, out_shape=out_shape,
        compiler_params=_cparams(),
        name="attn_pre",
    )(*x_args, mod[0], *[a for a, _ in consts], rope_tab)


def _rows_matmul_kernel(a_ref, w_ref, o_ref):
    o_ref[...] = _dot(a_ref[...].astype(BF16), w_ref[...]).astype(o_ref.dtype)


def _mla_cache_kv(cache_ckv, layer_i, wkvb):
    per_req = PAST_LEN // TOK
    n = wkvb[0].shape[-1]
    return pl.pallas_call(
        _rows_matmul_kernel,
        grid=(DEC_BATCH * per_req,),
        in_specs=[pl.BlockSpec((None, None, TOK, MLA_KV_RANK), lambda g: (g // per_req, layer_i, g % per_req, 0)),
                  wkvb[1]],
        out_specs=pl.BlockSpec((TOK, n), lambda g: (g, 0)),
        out_shape=jax.ShapeDtypeStruct((DEC_BATCH * PAST_LEN, n), BF16),
        compiler_params=_cparams(),
        name="mla_cache_kv",
    )(cache_ckv, wkvb[0])


def _attend(scores, values, scale):
    m = scores[0].max(axis=-1, keepdims=True)
    for s in scores[1:]:
        m = jnp.maximum(m, s.max(axis=-1, keepdims=True))
    ps = [jnp.exp2((s - m) * (scale * LOG2_E)) for s in scores]
    l = ps[0].sum(axis=-1, keepdims=True)
    for p in ps[1:]:
        l = l + p.sum(axis=-1, keepdims=True)
    o = _dot(ps[0].astype(BF16), values[0])
    for p, v in zip(ps[1:], values[1:]):
        o = o + _dot(p.astype(BF16), v)
    return o / l


def _attn_heads(q_ref, qn_ref, qp_ref, segs, o_ref):
    group = GQA_HEADS // GQA_KV_HEADS
    for kh in range(GQA_KV_HEADS):
        ksl = slice(kh * HEAD_DIM, (kh + 1) * HEAD_DIM)
        heads = [slice(hh * HEAD_DIM, (hh + 1) * HEAD_DIM) for hh in range(kh * group, (kh + 1) * group)]
        q = jnp.concatenate([q_ref[:, sl] for sl in heads], axis=0)
        scores = [_dot_nt(q, s[0][:, ksl]) for s in segs]
        o = _attend(scores, [s[1][:, ksl] for s in segs], G_SCALE).astype(BF16)
        for g, sl in enumerate(heads):
            o_ref[:, sl] = o[g * TOK:(g + 1) * TOK]
    o0 = GQA_HEADS * HEAD_DIM
    v0 = MLA_HEADS * MLA_NOPE
    for hh in range(MLA_HEADS):
        sl = slice(hh * LANES, (hh + 1) * LANES)
        q = jnp.concatenate([qn_ref[:, sl], qp_ref[:, sl]], axis=1)
        scores = [_dot_nt(q, jnp.concatenate([s[2][:, sl], s[3][...]], axis=1)) for s in segs]
        vals = [s[2][:, v0 + hh * MLA_V:v0 + (hh + 1) * MLA_V] for s in segs]
        o_ref[:, o0 + hh * MLA_V:o0 + (hh + 1) * MLA_V] = _attend(scores, vals, M_SCALE).astype(BF16)


def _attn_kernel(q_ref, qn_ref, qp_ref,
                 ck_ref, cv_ref, ckvm_ref, ckpe_ref,
                 dk_ref, dv_ref, dkvm_ref, dkpe_ref,
                 pk_ref, pv_ref, pkvm_ref, pkpe_ref,
                 o_ref):
    j = pl.program_id(0)

    @pl.when(j < N_CTX_TILES)
    def _():
        _attn_heads(q_ref, qn_ref, qp_ref, [(ck_ref, cv_ref, ckvm_ref, ckpe_ref)], o_ref)

    @pl.when(j >= N_CTX_TILES)
    def _():
        _attn_heads(q_ref, qn_ref, qp_ref,
                    [(dk_ref, dv_ref, dkvm_ref, dkpe_ref), (pk_ref, pv_ref, pkvm_ref, pkpe_ref)], o_ref)


def _dec_req(j):
    return jnp.maximum(j - N_CTX_TILES, 0) // DEC_TILES_PER_REQ


def _attn(layer_i, qg, kg, vg, qmn, qmp, kvm, kpeb, pk, pv, pkvm, pkpe):
    tile = lambda w: pl.BlockSpec((TOK, w), lambda j: (j, 0))
    ctx = lambda w: pl.BlockSpec((TOK, w), lambda j: (jnp.minimum(j, N_CTX_TILES - 1), 0))
    dec = lambda w: pl.BlockSpec((DEC_SEQ, w), lambda j: (CTX_TOK // DEC_SEQ + _dec_req(j), 0))
    past = lambda w: pl.BlockSpec((PAST_LEN, w), lambda j: (_dec_req(j), 0))
    cache = lambda w: pl.BlockSpec((None, None, PAST_LEN, w), lambda j: (_dec_req(j), layer_i, 0, 0))
    kvw = GQA_KV_HEADS * HEAD_DIM
    mw = MLA_HEADS * (MLA_NOPE + MLA_V)
    return pl.pallas_call(
        _attn_kernel,
        grid=(N_TILES,),
        in_specs=[tile(GQA_HEADS * HEAD_DIM), tile(MLA_HEADS * MLA_NOPE), tile(MLA_HEADS * LANES),
                  ctx(kvw), ctx(kvw), ctx(mw), ctx(LANES),
                  dec(kvw), dec(kvw), dec(mw), dec(LANES),
                  cache(kvw), cache(kvw), past(mw), cache(LANES)],
        out_specs=tile(D_MODEL),
        out_shape=jax.ShapeDtypeStruct((N_TOK, D_MODEL), BF16),
        compiler_params=_cparams(),
        name="attention",
    )(qg, qmn, qmp, kg, vg, kvm, kpeb, kg, vg, kvm, kpeb, pk, pv, pkvm, pkpe)


def _residual_and_router(x, mix_out, m, g2_ref, wr_ref, x1_ref, hff_ref, aff_ref):
    x1 = x + m[2:3] * mix_out
    x1_ref[...] = x1
    hf = _rms(x1, g2_ref[...]) * (1.0 + m[4:5]) + m[3:4]
    hi = hf.astype(BF16)
    hff_ref[...] = hi
    lo = (hf - hi.astype(F32)).astype(BF16)
    r = _dot(hi, wr_ref[0]) + _dot(lo, wr_ref[1])
    logits = r + pltpu.roll(r, LANES - N_EXPERTS, 1)
    lane = lax.broadcasted_iota(jnp.int32, logits.shape, 1)
    logits = jnp.where(lane < N_EXPERTS, logits, -jnp.inf)
    e = jnp.exp(logits - logits.max(axis=-1, keepdims=True))
    aff_ref[...] = e / e.sum(axis=-1, keepdims=True)


def _epilogue_specs():
    tile = lambda w: pl.BlockSpec((TOK, w), lambda j: (j, 0))
    out_specs = [tile(D_MODEL), tile(D_MODEL), tile(LANES)]
    out_shape = [jax.ShapeDtypeStruct((N_TOK, D_MODEL), F32),
                 jax.ShapeDtypeStruct((N_TOK, D_MODEL), BF16),
                 jax.ShapeDtypeStruct((N_TOK, LANES), F32)]
    return out_specs, out_shape


def _attn_out_kernel(nx, *refs):
    o_ref, mod_ref, wout_ref, g2_ref, wr_ref, x1_ref, hff_ref, aff_ref, wout_b_ref = refs[nx:]

    @pl.when(pl.program_id(0) == 0)
    def _():
        wout_b_ref[...] = wout_ref[...].astype(BF16)

    out = _dot(o_ref[...], wout_b_ref[...])
    _residual_and_router(_x_tile(refs[:nx]), out, mod_ref[...], g2_ref, wr_ref, x1_ref, hff_ref, aff_ref)


def _attn_out(x, o, mod, consts):
    out_specs, out_shape = _epilogue_specs()
    x_specs, x_args = _x_in(x)
    return pl.pallas_call(
        functools.partial(_attn_out_kernel, len(x_args)),
        grid=(N_TILES,),
        in_specs=x_specs + [pl.BlockSpec((TOK, D_MODEL), lambda j: (j, 0)), mod[1]] + [s for _, s in consts],
        out_specs=out_specs, out_shape=out_shape,
        scratch_shapes=[pltpu.VMEM((D_MODEL, D_MODEL), BF16)],
        compiler_params=_cparams(),
        name="attn_out",
    )(*x_args, o, mod[0], *[a for a, _ in consts])


CONV_HALO = 16
SUBLANES = 8
CONV_BASE = CONV_HALO - CONV_WIDTH // 2
CONV_SH_ROWS = (CONV_BASE + CONV_WIDTH - 1) // SUBLANES * SUBLANES + TOK
CONV_RING = 4


def _cs_pre_kernel(x_ref, mod_ref, g1_ref, win_ref, w_ref, b_ref, lng_ref, lnb_ref, us_ref, o_ref,
                   ring_ref, pad_ref, sh_ref, u_ref):
    j = pl.program_id(0)
    slot = lambda d: (j + d) & (CONV_RING - 1)

    @pl.when(j == 0)
    def _():
        ring_ref[...] = jnp.zeros_like(ring_ref)

    @pl.when(j < N_TILES)
    def _():
        m = mod_ref[...]
        h = _rms(x_ref[...], g1_ref[...]) * (1.0 + m[1:2]) + m[0:1]
        proj = _dot(h.astype(BF16), win_ref[...])
        ring_ref[slot(0)] = proj[:, :CONV_DIM] * jax.nn.sigmoid(proj[:, CONV_DIM:2 * CONV_DIM])
        us_ref[...] = proj[:, 2 * CONV_DIM:]

    @pl.when(j >= 1)
    def _():
        c = j - 1
        q = (c - N_CTX_TILES) % DEC_TILES_PER_REQ
        has_prev = jnp.logical_and(c >= N_CTX_TILES, q != 0)
        has_next = jnp.logical_and(c >= N_CTX_TILES, q != DEC_TILES_PER_REQ - 1)
        pad_ref[0:CONV_HALO, :] = jnp.where(has_prev, ring_ref[slot(-2), TOK - CONV_HALO:, :], 0.0)
        pad_ref[CONV_HALO:CONV_HALO + TOK, :] = ring_ref[slot(-1)]
        pad_ref[CONV_HALO + TOK:, :] = jnp.where(has_next, ring_ref[slot(0), 0:CONV_HALO, :], 0.0)
        for s in range(1, SUBLANES):
            sh_ref[s - 1] = pad_ref[s:s + CONV_SH_ROWS, :]
        for lt in range(CONV_DIM // LANES):
            ls = slice(lt * LANES, (lt + 1) * LANES)
            acc = None
            for tap in range(CONV_WIDTH):
                a8, s = (CONV_BASE + tap) // SUBLANES * SUBLANES, (CONV_BASE + tap) % SUBLANES
                win = pad_ref[a8:a8 + TOK, ls] if s == 0 else sh_ref[s - 1, a8:a8 + TOK, ls]
                term = w_ref[tap:tap + 1, ls] * win
                acc = term if acc is None else acc + term
            u_ref[:, ls] = acc + b_ref[:, ls]
        u = u_ref[...]
        mu = jnp.mean(u, axis=-1, keepdims=True)
        uc = u - mu
        y = uc * lax.rsqrt(jnp.mean(uc * uc, axis=-1, keepdims=True) + EPS) * lng_ref[...] + lnb_ref[...]
        o_ref[...] = _silu(y).astype(BF16)


def _cs_pre(x, mod, consts):
    last = lambda j: jnp.minimum(j, N_TILES - 1)
    return pl.pallas_call(
        _cs_pre_kernel,
        grid=(N_TILES + 1,),
        in_specs=[pl.BlockSpec((TOK, D_MODEL), lambda j: (last(j), 0)), mod[1]] + [s for _, s in consts],
        out_specs=[pl.BlockSpec((TOK, S5_DIM), lambda j: (last(j), 0)),
                   pl.BlockSpec((TOK, CONV_DIM), lambda j: (jnp.maximum(j - 1, 0), 0))],
        out_shape=[jax.ShapeDtypeStruct((N_TOK, S5_DIM), F32),
                   jax.ShapeDtypeStruct((N_TOK, CONV_DIM), BF16)],
        scratch_shapes=[pltpu.VMEM((CONV_RING, TOK, CONV_DIM), F32),
                        pltpu.VMEM((TOK + 2 * CONV_HALO, CONV_DIM), F32),
                        pltpu.VMEM((SUBLANES - 1, CONV_SH_ROWS, CONV_DIM), F32),
                        pltpu.VMEM((TOK, CONV_DIM), F32)],
        compiler_params=_cparams(),
        name="cs_pre",
    )(x, mod[0], *[a for a, _ in consts])


S5_PITCH = 24
S5_SLABS = S5_DIM // LANES
S5_SCAN_COLS = 256


def _to_time_major(blk_ref, slab_ref, rows):
    for s in range(S5_SLABS):
        for j in range(rows):
            slab_ref[s, j * S5_PITCH:j * S5_PITCH + S5_TT, :] = blk_ref[j, :, s * LANES:(s + 1) * LANES]
    steps = [jnp.concatenate([slab_ref[s, pl.ds(t, rows, stride=S5_PITCH), :] for s in range(S5_SLABS)], axis=1)
             for t in range(S5_TT)]
    return jnp.concatenate(steps, axis=0)


def _from_time_major(y, slab_ref, out_ref, rows, c0):
    n_slabs = y.shape[1] // LANES
    for t in range(S5_TT):
        for s in range(n_slabs):
            slab_ref[s, pl.ds(t, rows, stride=S5_PITCH), :] = y[t * rows:(t + 1) * rows, s * LANES:(s + 1) * LANES]
    for s in range(n_slabs):
        for j in range(rows):
            out_ref[j, :, c0 + s * LANES:c0 + (s + 1) * LANES] = slab_ref[s, j * S5_PITCH:j * S5_PITCH + S5_TT, :]


def _s5_scan_quarter(lam_ref, st_ref, bu, d, q, rows, reverse):
    ncb = S5_QS // S5_SCAN_COLS
    states = [[None] * (2 * ncb) for _ in range(S5_TT)]
    s0 = q * 2 * S5_QS
    for cb in range(ncb):
        c_re = slice(cb * S5_SCAN_COLS, (cb + 1) * S5_SCAN_COLS)
        c_im = slice(S5_QS + cb * S5_SCAN_COLS, S5_QS + (cb + 1) * S5_SCAN_COLS)
        st_re = slice(s0 + c_re.start, s0 + c_re.stop)
        st_im = slice(s0 + c_im.start, s0 + c_im.stop)
        lr = lam_ref[d, q, 0:1, c_re]
        li = lam_ref[d, q, 1:2, c_re]
        sr = st_ref[d, :, st_re]
        si = st_ref[d, :, st_im]
        for k in range(S5_TT):
            t = (S5_TT - 1 - k) if reverse else k
            nr = lr * sr - li * si
            ni = lr * si + li * sr
            if bu is not None:
                nr = nr + bu[t * rows:(t + 1) * rows, c_re]
                ni = ni + bu[t * rows:(t + 1) * rows, c_im]
            sr, si = nr, ni
            states[t][cb] = sr
            states[t][ncb + cb] = si
        st_ref[d, :, st_re] = sr
        st_ref[d, :, st_im] = si
    return jnp.concatenate([jnp.concatenate(row, axis=1) for row in states], axis=0)


def _s5_kernel(rows, has_input, *refs):
    if has_input:
        uf_ref, ur_ref, wb_ref, dsk_ref = refs[:4]
        refs = refs[4:]
    wc_ref, lam_ref, s0_ref, yf_ref, yr_ref, fin_ref, st_ref, uslab_ref, yslab_ref = refs
    i = pl.program_id(0)

    @pl.when(i == 0)
    def _():
        st_ref[...] = s0_ref[...]

    for d, (y_ref, reverse) in enumerate(((yf_ref, False), (yr_ref, True))):
        if has_input:
            u = _to_time_major(ur_ref if reverse else uf_ref, uslab_ref, rows)
            ub = u.astype(BF16)
        for q in range(S5_Q):
            csl = slice(q * S5_QC, (q + 1) * S5_QC)
            bu = _dot(ub[:, csl], wb_ref[d, q]) if has_input else None
            s_all = _s5_scan_quarter(lam_ref, st_ref, bu, d, q, rows, reverse)
            y = _dot(s_all.astype(BF16), wc_ref[d, q])
            if has_input and not reverse:
                y = y + dsk_ref[:, csl] * u[:, csl]
            _from_time_major(y, yslab_ref.at[q % 2], y_ref, rows, q * S5_QC)

    @pl.when(i == S5_NB - 1)
    def _():
        fin_ref[...] = st_ref[...]


def _s5(rows, u3, wb, wc, lam, s0, dskip):
    fwd = pl.BlockSpec((rows, S5_TT, S5_DIM), lambda i: (0, i, 0))
    rev = pl.BlockSpec((rows, S5_TT, S5_DIM), lambda i: (0, S5_NB - 1 - i, 0))
    has_input = u3 is not None
    in_specs, args = [], []
    if has_input:
        in_specs += [fwd, rev, wb[1], dskip[1]]
        args += [u3, u3, wb[0], dskip[0]]
    in_specs += [wc[1], lam[1], s0[1]]
    args += [wc[0], lam[0], s0[0]]
    y_shape = jax.ShapeDtypeStruct((rows, TOK, S5_DIM), F32)
    st_shape = (2, rows, S5_STATE_COLS)
    return pl.pallas_call(
        functools.partial(_s5_kernel, rows, has_input),
        grid=(S5_NB,),
        in_specs=in_specs,
        out_specs=[fwd, rev, _const_spec(st_shape)],
        out_shape=[y_shape, y_shape, jax.ShapeDtypeStruct(st_shape, F32)],
        scratch_shapes=[pltpu.VMEM(st_shape, F32),
                        pltpu.VMEM((S5_SLABS, rows * S5_PITCH, LANES), F32),
                        pltpu.VMEM((2, S5_QC // LANES, rows * S5_PITCH, LANES), F32)],
        compiler_params=_cparams(),
        name="s5_scan" if has_input else "s5_carry_fix",
    )(*args)


def _gelu_tanh(x):
    return x * (0.5 * (1.0 + jnp.tanh(math.sqrt(2.0 / math.pi) * (x + 0.044715 * (x * x * x)))))


def _cs_post_kernel(uc_ref, yf_ref, yr_ref, cf_ref, cr_ref, x_ref, mod_ref, wglu_ref, bglu_ref, wout_ref,
                    g2_ref, wr_ref, x1_ref, hff_ref, aff_ref, wglu_b_ref, wout_b_ref):
    @pl.when(pl.program_id(0) == 0)
    def _():
        wglu_b_ref[...] = wglu_ref[...].astype(BF16)
        wout_b_ref[...] = wout_ref[...].astype(BF16)

    y = _gelu_tanh(yf_ref[...] + yr_ref[...] + cf_ref[...] + cr_ref[...])
    y = y * jax.nn.sigmoid(_dot(y.astype(BF16), wglu_b_ref[...]) + bglu_ref[...])
    out = _dot(uc_ref[...], wout_b_ref[0:CONV_DIM, :]) + _dot(y.astype(BF16), wout_b_ref[CONV_DIM:, :])
    _residual_and_router(x_ref[...], out, mod_ref[...], g2_ref, wr_ref, x1_ref, hff_ref, aff_ref)


def _fix_slot(j, reverse):
    d = j - N_CTX_TILES
    r, q = d // DEC_TILES_PER_REQ, d % DEC_TILES_PER_REQ
    if reverse:
        ok, slot = q != DEC_TILES_PER_REQ - 1, r * (DEC_TILES_PER_REQ - 1) + q
    else:
        ok, slot = q != 0, r * (DEC_TILES_PER_REQ - 1) + q - 1
    return jnp.where(jnp.logical_and(j >= N_CTX_TILES, ok), slot, S5_FIX_ROWS - 1)


def _cs_post(uc, yf, yr, cf, cr, x, mod, consts):
    tile = lambda w: pl.BlockSpec((TOK, w), lambda j: (j, 0))
    out_specs, out_shape = _epilogue_specs()
    return pl.pallas_call(
        _cs_post_kernel,
        grid=(N_TILES,),
        in_specs=[tile(CONV_DIM), tile(S5_DIM), tile(S5_DIM),
                  pl.BlockSpec((TOK, S5_DIM), lambda j: (_fix_slot(j, False), 0)),
                  pl.BlockSpec((TOK, S5_DIM), lambda j: (_fix_slot(j, True), 0)),
                  tile(D_MODEL), mod[1]] + [s for _, s in consts],
        out_specs=out_specs, out_shape=out_shape,
        scratch_shapes=[pltpu.VMEM((S5_DIM, S5_DIM), BF16), pltpu.VMEM((D_MODEL, D_MODEL), BF16)],
        compiler_params=_cparams(),
        name="cs_post",
    )(uc, yf, yr, cf, cr, x, mod[0], *[a for a, _ in consts])


RANK_CHUNK = 128
GATHER_ROWS = 512


def _moe_select(cap, aff_ref, hff_ref, xs_ref, gs_ref, g_ref, rank_ref, p_ref):
    n = aff_ref.shape[0]
    nblk = n // RANK_CHUNK
    a_t = aff_ref[...].T
    sub = lax.broadcasted_iota(jnp.int32, (RANK_CHUNK, RANK_CHUNK), 0)
    lan = lax.broadcasted_iota(jnp.int32, (RANK_CHUNK, RANK_CHUNK), 1)
    earlier = sub < lan
    slot = lax.broadcasted_iota(jnp.int32, (cap, n), 0).astype(F32)
    rank_ref[...] = jnp.zeros_like(rank_ref)
    for e in range(N_EXPERTS):
        row = a_t[e:e + 1, :]
        cols = [aff_ref[c * RANK_CHUNK:(c + 1) * RANK_CHUNK, e:e + 1] for c in range(nblk)]
        for b in range(nblk):
            rb = row[:, b * RANK_CHUNK:(b + 1) * RANK_CHUNK]
            cnt = None
            for c in range(nblk):
                if c < b:
                    part = jnp.where(cols[c] >= rb, 1.0, 0.0)
                elif c > b:
                    part = jnp.where(cols[c] > rb, 1.0, 0.0)
                else:
                    tie = jnp.where(earlier, cols[c], -1.0) == rb
                    part = jnp.where(cols[c] > rb, 1.0, 0.0) + jnp.where(tie, 1.0, 0.0)
                cnt = part if cnt is None else cnt + part
            rank_ref[e:e + 1, b * RANK_CHUNK:(b + 1) * RANK_CHUNK] = jnp.sum(cnt, axis=0, keepdims=True)
        rank = rank_ref[e:e + 1, :]
        onehot = slot == rank
        p_ref[e * cap:(e + 1) * cap, :] = jnp.where(onehot, 1.0, 0.0).astype(BF16)
        gs_ref[e] = jnp.sum(jnp.where(onehot, row, 0.0), axis=1, keepdims=True)
    grp = GATHER_ROWS // cap
    for e0 in range(0, N_EXPERTS, grp):
        xs = _dot(p_ref[e0 * cap:(e0 + grp) * cap, :], hff_ref[...])
        for k in range(grp):
            xs_ref[e0 + k] = xs[k * cap:(k + 1) * cap].astype(BF16)
    rank_t = rank_ref[...].T
    per = LANES // cap
    lane = lax.broadcasted_iota(jnp.int32, (n, LANES), 1)
    slot_lane = (lane & (cap - 1)).astype(F32)
    for blk in range(N_EXPERTS // per):
        rc = rank_t[:, blk * per:blk * per + 1]
        for k in range(1, per):
            rc = jnp.where(lane >= k * cap, rank_t[:, blk * per + k:blk * per + k + 1], rc)
        g_ref[:, blk * LANES:(blk + 1) * LANES] = jnp.where(slot_lane == rc, 1.0, 0.0).astype(BF16)


def _moe_gather_kernel(affc_ref, hffc_ref, affd_ref, hffd_ref,
                       xsc_ref, gsc_ref, gc_ref, xsd_ref, gsd_ref, gd_ref,
                       rankc_ref, pc_ref, rankd_ref, pd_ref):
    s = pl.program_id(0)

    @pl.when(s < BATCH)
    def _():
        _moe_select(CAP_CTX, affc_ref, hffc_ref, xsc_ref, gsc_ref, gc_ref, rankc_ref, pc_ref)

    @pl.when(s >= BATCH)
    def _():
        _moe_select(CAP_DEC, affd_ref, hffd_ref, xsd_ref, gsd_ref, gd_ref, rankd_ref, pd_ref)


def _moe_gather(aff, hff):
    cidx = lambda s: jnp.minimum(s, BATCH - 1)
    didx = lambda s: jnp.maximum(s - BATCH, 0)
    dec0 = CTX_TOK // DEC_SEQ
    return pl.pallas_call(
        _moe_gather_kernel,
        grid=(BATCH + DEC_BATCH,),
        in_specs=[pl.BlockSpec((SEQ, LANES), lambda s: (cidx(s), 0)),
                  pl.BlockSpec((SEQ, D_MODEL), lambda s: (cidx(s), 0)),
                  pl.BlockSpec((DEC_SEQ, LANES), lambda s: (dec0 + didx(s), 0)),
                  pl.BlockSpec((DEC_SEQ, D_MODEL), lambda s: (dec0 + didx(s), 0))],
        out_specs=[pl.BlockSpec((N_EXPERTS, CAP_CTX, D_MODEL), lambda s: (0, cidx(s), 0)),
                   pl.BlockSpec((N_EXPERTS, CAP_CTX, 1), lambda s: (0, cidx(s), 0)),
                   pl.BlockSpec((SEQ, N_EXPERTS * CAP_CTX), lambda s: (cidx(s), 0)),
                   pl.BlockSpec((N_EXPERTS, CAP_DEC, D_MODEL), lambda s: (0, didx(s), 0)),
                   pl.BlockSpec((N_EXPERTS, CAP_DEC, 1), lambda s: (0, didx(s), 0)),
                   pl.BlockSpec((DEC_SEQ, N_EXPERTS * CAP_DEC), lambda s: (didx(s), 0))],
        out_shape=[jax.ShapeDtypeStruct((N_EXPERTS, XS_CTX_ROWS, D_MODEL), BF16),
                   jax.ShapeDtypeStruct((N_EXPERTS, XS_CTX_ROWS, 1), F32),
                   jax.ShapeDtypeStruct((CTX_TOK, N_EXPERTS * CAP_CTX), BF16),
                   jax.ShapeDtypeStruct((N_EXPERTS, XS_DEC_ROWS, D_MODEL), BF16),
                   jax.ShapeDtypeStruct((N_EXPERTS, XS_DEC_ROWS, 1), F32),
                   jax.ShapeDtypeStruct((DEC_BATCH * DEC_SEQ, N_EXPERTS * CAP_DEC), BF16)],
        scratch_shapes=[pltpu.VMEM((LANES, SEQ), F32), pltpu.VMEM((N_EXPERTS * CAP_CTX, SEQ), BF16),
                        pltpu.VMEM((LANES, DEC_SEQ), F32), pltpu.VMEM((N_EXPERTS * CAP_DEC, DEC_SEQ), BF16)],
        compiler_params=_cparams(),
        name="moe_gather",
    )(aff, hff, aff, hff)


FF_CHUNK = 512
N_FF_CHUNKS = EXPERT_FF // FF_CHUNK
assert N_FF_CHUNKS >= 2


def _moe_ffn_kernel(xsc_ref, xsd_ref, gsc_ref, gsd_ref, wg_ref, wu_ref, wd_ref, yc_ref, yd_ref, acc_ref):
    f = pl.program_id(1)
    groups = ((xsc_ref, gsc_ref, yc_ref, 0, BATCH, CAP_CTX), (xsd_ref, gsd_ref, yd_ref, XS_CTX_ROWS, DEC_BATCH, CAP_DEC))

    def chunk(consume):
        wg = wg_ref[...].astype(BF16)
        wu = wu_ref[...].astype(BF16)
        wd = wd_ref[...].astype(BF16)
        for grp in groups:
            xs = grp[0][...]
            act = _silu(_dot(xs, wg)) * _dot(xs, wu)
            consume(grp, _dot(act.astype(BF16), wd))

    def rows_of(grp):
        return slice(grp[3], grp[3] + grp[4] * grp[5])

    @pl.when(f == 0)
    def _():
        def first(grp, part):
            acc_ref[rows_of(grp), :] = part
        chunk(first)

    if N_FF_CHUNKS > 2:
        @pl.when(jnp.logical_and(f > 0, f < N_FF_CHUNKS - 1))
        def _():
            def middle(grp, part):
                acc_ref[rows_of(grp), :] += part
            chunk(middle)

    @pl.when(f == N_FF_CHUNKS - 1)
    def _():
        def last(grp, part):
            _, gs_ref, y_ref, _, n_req, cap = grp
            y = ((acc_ref[rows_of(grp), :] + part) * gs_ref[...]).astype(BF16)
            for r in range(n_req):
                y_ref[r] = y[r * cap:(r + 1) * cap]
        chunk(last)


def _moe_ffn(layer, xsc, xsd, gsc, gsd, w_gate, w_up, w_down):
    per_e = lambda rows, w: pl.BlockSpec((None, rows, w), lambda e, f: (e, 0, 0))
    return pl.pallas_call(
        _moe_ffn_kernel,
        grid=(N_EXPERTS, N_FF_CHUNKS),
        in_specs=[per_e(XS_CTX_ROWS, D_MODEL), per_e(XS_DEC_ROWS, D_MODEL),
                  per_e(XS_CTX_ROWS, 1), per_e(XS_DEC_ROWS, 1),
                  pl.BlockSpec((None, None, D_MODEL, FF_CHUNK), lambda e, f: (layer, e, 0, f)),
                  pl.BlockSpec((None, None, D_MODEL, FF_CHUNK), lambda e, f: (layer, e, 0, f)),
                  pl.BlockSpec((None, None, FF_CHUNK, D_MODEL), lambda e, f: (layer, e, f, 0))],
        out_specs=[pl.BlockSpec((BATCH, CAP_CTX, D_MODEL), lambda e, f: (0, e, 0)),
                   pl.BlockSpec((DEC_BATCH, CAP_DEC, D_MODEL), lambda e, f: (0, e, 0))],
        out_shape=[jax.ShapeDtypeStruct((BATCH, N_EXPERTS * CAP_CTX, D_MODEL), BF16),
                   jax.ShapeDtypeStruct((DEC_BATCH, N_EXPERTS * CAP_DEC, D_MODEL), BF16)],
        scratch_shapes=[pltpu.VMEM((XS_CTX_ROWS + XS_DEC_ROWS, D_MODEL), F32)],
        compiler_params=_cparams(2),
        name="moe_ffn",
    )(xsc, xsd, gsc, gsd, w_gate, w_up, w_down)


def _moe_combine_kernel(final, gc_ref, yc_ref, gd_ref, yd_ref, x1_ref, mod_ref, fg_ref, *o_refs):
    j = pl.program_id(0)

    def finish(comb, o_ref):
        x2 = x1_ref[...] + mod_ref[5:6, :] * comb
        o_ref[...] = _rms(x2, fg_ref[...]) if final else x2

    @pl.when(j < N_CTX_TILES)
    def _():
        finish(_dot(gc_ref[...], yc_ref[...]), o_refs[0])

    @pl.when(j >= N_CTX_TILES)
    def _():
        finish(_dot(gd_ref[...], yd_ref[...]), o_refs[-1])


def _moe_combine(final, gc, yc, gd, yd, x1, mod, fg):
    if final:
        out_specs = [pl.BlockSpec((TOK, D_MODEL), lambda j: (_ctx_tile(j), 0)),
                     pl.BlockSpec((TOK, D_MODEL), lambda j: (_dec_tile(j), 0))]
        out_shape = [jax.ShapeDtypeStruct((CTX_TOK, D_MODEL), F32),
                     jax.ShapeDtypeStruct((N_TOK - CTX_TOK, D_MODEL), F32)]
    else:
        out_specs = pl.BlockSpec((TOK, D_MODEL), lambda j: (j, 0))
        out_shape = jax.ShapeDtypeStruct((N_TOK, D_MODEL), F32)
    return pl.pallas_call(
        functools.partial(_moe_combine_kernel, final),
        grid=(N_TILES,),
        in_specs=[pl.BlockSpec((TOK, N_EXPERTS * CAP_CTX), lambda j: (_ctx_tile(j), 0)),
                  pl.BlockSpec((None, N_EXPERTS * CAP_CTX, D_MODEL), lambda j: (_ctx_tile(j), 0, 0)),
                  pl.BlockSpec((TOK, N_EXPERTS * CAP_DEC), lambda j: (_dec_tile(j), 0)),
                  pl.BlockSpec((None, N_EXPERTS * CAP_DEC, D_MODEL),
                               lambda j: (_dec_tile(j) // DEC_TILES_PER_REQ, 0, 0)),
                  pl.BlockSpec((TOK, D_MODEL), lambda j: (j, 0)),
                  mod[1], _const_spec((1, D_MODEL))],
        out_specs=out_specs, out_shape=out_shape,
        compiler_params=_cparams(),
        name="moe_combine",
    )(gc, yc, gd, yd, x1, mod[0], fg)


def _moe(layer, final, x1, hff, aff, mod, w_gate, w_up, w_down, fg):
    xsc, gsc, gc, xsd, gsd, gd = _moe_gather(aff, hff)
    yc, yd = _moe_ffn(layer, xsc, xsd, gsc, gsd, w_gate, w_up, w_down)
    return _moe_combine(final, gc, yc, gd, yd, x1, mod, fg)


def _axial_rope(n_tokens, rot_dim):
    rows = n_tokens // GRID_W
    per_axis = rot_dim // 4
    freqs = ROPE_THETA ** (-jnp.arange(per_axis, dtype=F32) / per_axis)
    row = jnp.repeat(jnp.arange(rows, dtype=F32), GRID_W)
    col = jnp.tile(jnp.arange(GRID_W, dtype=F32), rows)
    ang = jnp.concatenate([row[:, None] * freqs, col[:, None] * freqs], axis=-1)
    return jnp.cos(ang), jnp.sin(ang)


def _rope_table():
    cg, sg = _axial_rope(DEC_SEQ, HEAD_DIM)
    cm, sm = _axial_rope(DEC_SEQ, MLA_ROPE)
    z = jnp.zeros_like(cm)
    pos = jnp.concatenate([
        jnp.concatenate([cg, cg], -1), jnp.concatenate([-sg, sg], -1),
        jnp.concatenate([cm, cm, z, z], -1), jnp.concatenate([-sm, z, z, z], -1),
        jnp.concatenate([z, sm, z, z], -1)], axis=-1)
    one, zero = jnp.ones((TOK, LANES), F32), jnp.zeros((TOK, LANES), F32)
    ident = jnp.concatenate([one, zero, one, zero, zero], axis=-1)
    return jnp.concatenate([ident, pos], axis=0)


def _router_weights(w):
    hi = w.astype(BF16)
    lo = (w - hi.astype(F32)).astype(BF16)
    z = jnp.zeros(w.shape[:-1] + (LANES - 2 * N_EXPERTS,), BF16)
    return jnp.stack([jnp.concatenate([hi, lo, z], axis=-1),
                      jnp.concatenate([hi, jnp.zeros_like(lo), z], axis=-1)], axis=1)


def _mla_weights(w_qb, w_kvb):
    n = w_qb.shape[0]
    qb = w_qb.astype(BF16).reshape(n, MLA_Q_RANK, MLA_HEADS, MLA_NOPE + MLA_ROPE)
    qb_pe = jnp.pad(qb[..., MLA_NOPE:], ((0, 0), (0, 0), (0, 0), (0, LANES - MLA_ROPE)))
    wqb = jnp.concatenate([qb[..., :MLA_NOPE].reshape(n, MLA_Q_RANK, -1), qb_pe.reshape(n, MLA_Q_RANK, -1)], axis=-1)
    kvb = w_kvb.astype(BF16).reshape(n, MLA_KV_RANK, MLA_HEADS, MLA_NOPE + MLA_V)
    wkvb = jnp.concatenate([kvb[..., :MLA_NOPE].reshape(n, MLA_KV_RANK, -1),
                            kvb[..., MLA_NOPE:].reshape(n, MLA_KV_RANK, -1)], axis=-1)
    return wqb, wkvb


def _s5_weights(a_re, a_im, log_step, b_re, b_im, c_re, c_im):
    n_ld = a_re.shape[0] * a_re.shape[1]
    flat = lambda v: v.astype(F32).reshape((n_ld,) + v.shape[2:])
    a_re, a_im, log_step, b_re, b_im, c_re, c_im = map(flat, (a_re, a_im, log_step, b_re, b_im, c_re, c_im))
    ar, ai = a_re, a_im
    step = jnp.exp(log_step)[..., None]

    def cexp(k):
        mag = jnp.exp(k * ar * step)
        return mag * jnp.cos(k * ai * step), mag * jnp.sin(k * ai * step)

    lr, li = cexp(1.0)
    den = ar * ar + ai * ai
    gr = ((lr - 1.0) * ar + li * ai) / den
    gi = (li * ar - (lr - 1.0) * ai) / den
    br, bi = b_re, b_im
    bbr = gr[..., None] * br - gi[..., None] * bi
    bbi = gr[..., None] * bi + gi[..., None] * br
    gq = S5_GROUPS // S5_Q
    group_of_col = (np.arange(2 * S5_QS) % S5_QS) // S5_STATE
    own = jnp.asarray(np.arange(gq)[:, None] == group_of_col[None, :])

    def rows_in(m):
        m = m.reshape(n_ld, S5_Q, gq, S5_STATE, S5_GROUP).transpose(0, 1, 4, 2, 3)
        return m.reshape(n_ld, S5_Q, S5_GROUP, S5_QS)

    def cols_out(m):
        m = m.reshape(n_ld, S5_Q, gq, S5_GROUP, S5_STATE).transpose(0, 1, 2, 4, 3)
        return m.reshape(n_ld, S5_Q, S5_QS, S5_GROUP)

    b_rows = jnp.concatenate([rows_in(bbr), rows_in(bbi)], axis=-1).astype(BF16)
    wb = jnp.where(own[None, None, :, None, :], b_rows[:, :, None, :, :], 0)
    c_cols = jnp.concatenate([cols_out(c_re), cols_out(-c_im)], axis=-2).astype(BF16)
    wc = jnp.where(own.T[None, None, :, :, None], c_cols[:, :, :, None, :], 0)
    to_q = lambda v: v.reshape(n_ld, S5_Q, S5_QS)
    lam_q = jnp.stack([to_q(lr), to_q(li)], axis=2)
    cr_, ci_ = cexp(float(TOK))
    lam_chunk = jnp.stack([to_q(cr_), to_q(ci_)], axis=2)
    per_layer = lambda v, tail: v.reshape((n_ld // 2, 2) + tail)
    return (per_layer(wb, (S5_Q, S5_QC, 2 * S5_QS)), per_layer(wc, (S5_Q, 2 * S5_QS, S5_QC)),
            per_layer(lam_q, (S5_Q, 2, S5_QS)), per_layer(lam_chunk, (S5_STATE_COLS,)))


def _cmul_cols(a, b):
    a4 = a.reshape(a.shape[:-1] + (S5_Q, 2, S5_QS))
    b4 = b.reshape(b.shape[:-1] + (S5_Q, 2, S5_QS))
    re = a4[..., 0, :] * b4[..., 0, :] - a4[..., 1, :] * b4[..., 1, :]
    im = a4[..., 0, :] * b4[..., 1, :] + a4[..., 1, :] * b4[..., 0, :]
    return jnp.stack([re, im], axis=-2).reshape(re.shape[:-2] + (S5_STATE_COLS,))


def _s5_initial_states(state_s5):
    ns = state_s5.shape[1]
    h0 = state_s5.reshape(DEC_BATCH, ns, 2, 2, S5_Q, S5_QS).transpose(1, 2, 0, 4, 3, 5)
    h0 = h0.reshape(ns, 2, DEC_BATCH, S5_STATE_COLS)
    first = N_CTX_TILES + DEC_TILES_PER_REQ * np.arange(DEC_BATCH)
    s0 = jnp.zeros((ns, 2, N_TILES, S5_STATE_COLS), F32)
    return s0.at[:, 0, first].set(h0[:, 0]).at[:, 1, first + DEC_TILES_PER_REQ - 1].set(h0[:, 1])


def _s5_branch(us, wb, wc, lam_q, s0, dskip, lam_chunk):
    yf, yr, fin = _s5(N_TILES, us.reshape(N_TILES, TOK, S5_DIM), wb, wc, lam_q, s0, dskip)
    fz = fin[:, N_CTX_TILES:].reshape(2, DEC_BATCH, DEC_TILES_PER_REQ, S5_STATE_COLS)
    f1 = fz[0, :, 0]
    f2 = fz[0, :, 1] + _cmul_cols(lam_chunk[0], f1)
    f3 = fz[0, :, 2] + _cmul_cols(lam_chunk[0], f2)
    r2 = fz[1, :, 3]
    r1 = fz[1, :, 2] + _cmul_cols(lam_chunk[1], r2)
    r0 = fz[1, :, 1] + _cmul_cols(lam_chunk[1], r1)
    pad = jnp.zeros((S5_FIX_ROWS - DEC_BATCH * (DEC_TILES_PER_REQ - 1), S5_STATE_COLS), F32)
    sin_f = jnp.concatenate([jnp.stack([f1, f2, f3], 1).reshape(-1, S5_STATE_COLS), pad])
    sin_r = jnp.concatenate([jnp.stack([r0, r1, r2], 1).reshape(-1, S5_STATE_COLS), pad])
    sin = jnp.stack([sin_f, sin_r])
    cf, cr, _ = _s5(S5_FIX_ROWS, None, None, wc, lam_q, (sin, _const_spec(sin.shape)), None)
    rows2d = lambda a: a.reshape(-1, S5_DIM)
    new_state = fin[:, :N_CTX_TILES].reshape(2, BATCH, S5_Q, 2, S5_QS).transpose(1, 0, 3, 2, 4)
    new_state = new_state.reshape(BATCH, 2, 2, S5_GROUPS, S5_STATE)
    return rows2d(yf), rows2d(yr), rows2d(cf), rows2d(cr), new_state


def kernel(x_prompt, x_sample, cache_gqa_k, cache_gqa_v, cache_mla_ckv, cache_mla_kpe, state_s5, c, c_ctx, w_mod, b_mod, norm1_g, norm2_g, attn_w_in, gqa_q_norm, gqa_k_norm, mla_qa_norm, mla_w_qb, mla_kva_norm, mla_w_kvb, attn_w_out, cs_w_in, conv_w, conv_b, conv_ln_g, conv_ln_b, s5_a_re, s5_a_im, s5_log_step, s5_b_re, s5_b_im, s5_c_re, s5_c_im, s5_d, s5_w_glu, s5_b_glu, cs_w_out, moe_router, moe_w_gate, moe_w_up, moe_w_down, final_norm_g):
    x = (x_prompt.reshape(CTX_TOK, D_MODEL), x_sample.reshape(DEC_BATCH * DEC_SEQ, D_MODEL))
    cond8 = jnp.concatenate([c_ctx[None, :], c, jnp.zeros((8 - 1 - DEC_BATCH, D_MODEL), F32)])
    mod = _modulation(cond8, w_mod, b_mod)
    tile_row = np.concatenate([np.zeros(N_CTX_TILES, np.int32),
                               1 + np.repeat(np.arange(DEC_BATCH, dtype=np.int32), DEC_TILES_PER_REQ)])
    modt = mod[:, tile_row].reshape(DEPTH, N_TILES, 6, D_MODEL)
    rope_tab = _rope_table()
    fg = final_norm_g.reshape(1, D_MODEL)
    rows3 = lambda v: v.reshape(v.shape[0], 1, v.shape[-1])
    norm1, norm2 = rows3(norm1_g), rows3(norm2_g)
    attn_w_in_b, cs_w_in_b = attn_w_in.astype(BF16), cs_w_in.astype(BF16)
    wr_all = _router_weights(moe_router)
    wqb_all, wkvb_all = _mla_weights(mla_w_qb, mla_w_kvb)
    qn, kn, qan, kvan = rows3(gqa_q_norm), rows3(gqa_k_norm), rows3(mla_qa_norm), rows3(mla_kva_norm)
    n_attn = cache_gqa_k.shape[1]
    pk_all = cache_gqa_k.reshape(DEC_BATCH, n_attn, PAST_LEN, -1).astype(BF16)
    pv_all = cache_gqa_v.reshape(DEC_BATCH, n_attn, PAST_LEN, -1).astype(BF16)
    pkpe_all = jnp.pad(cache_mla_kpe.astype(BF16), ((0, 0), (0, 0), (0, 0), (0, LANES - MLA_ROPE)))
    conv_b3, ln_g3, ln_b3, dskip3, b_glu3 = map(rows3, (conv_b, conv_ln_g, conv_ln_b, s5_d, s5_b_glu))
    wb_all, wc_all, lam_all, lam_chunk_all = _s5_weights(s5_a_re, s5_a_im, s5_log_step, s5_b_re, s5_b_im,
                                                         s5_c_re, s5_c_im)
    s0_all = _s5_initial_states(state_s5)
    new_k, new_v, new_ckv, new_kpe, new_s5 = [], [], [], [], []
    for l in range(DEPTH):
        i = l // 2
        mod_l = _mod_operand(modt, l)
        if l % 2 == 0:
            wkvb = _sel(wkvb_all, i)
            (qg, kg, vg, kf, vf, ckvf, kpef, qmn, qmp, kvm, kpeb) = _attn_pre(
                x, mod_l, [_sel(norm1, l), _sel(attn_w_in_b, i), _sel(qn, i), _sel(kn, i), _sel(qan, i),
                           _sel(kvan, i), _sel(wqb_all, i), wkvb], rope_tab)
            pkvm = _mla_cache_kv(cache_mla_ckv, i, wkvb)
            o = _attn(i, qg, kg, vg, qmn, qmp, kvm, kpeb, pk_all, pv_all, pkvm, pkpe_all)
            x1, hff, aff = _attn_out(x, o, mod_l, [_sel(attn_w_out, i, single_buffer=True), _sel(norm2, l),
                                                   _sel(wr_all, l)])
            new_k.append(kf.reshape(BATCH, SEQ, GQA_KV_HEADS, HEAD_DIM))
            new_v.append(vf.reshape(BATCH, SEQ, GQA_KV_HEADS, HEAD_DIM))
            new_ckv.append(ckvf.reshape(BATCH, SEQ, MLA_KV_RANK))
            new_kpe.append(kpef.reshape(BATCH, SEQ, MLA_ROPE))
        else:
            us, uc = _cs_pre(x, mod_l, [_sel(norm1, l), _sel(cs_w_in_b, i), _sel(conv_w, i), _sel(conv_b3, i),
                                        _sel(ln_g3, i), _sel(ln_b3, i)])
            yf, yr, cf, cr, ns = _s5_branch(us, _sel(wb_all, i), _sel(wc_all, i), _sel(lam_all, i),
                                            _sel(s0_all, i), _sel(dskip3, i), lam_chunk_all[i])
            new_s5.append(ns)
            x1, hff, aff = _cs_post(uc, yf, yr, cf, cr, x, mod_l,
                                    [_sel(s5_w_glu, i, single_buffer=True), _sel(b_glu3, i),
                                     _sel(cs_w_out, i, single_buffer=True), _sel(norm2, l), _sel(wr_all, l)])
        x = _moe(l, l == DEPTH - 1, x1, hff, aff, mod_l, moe_w_gate, moe_w_up, moe_w_down, fg)
    y_prompt = x[0].reshape(BATCH, SEQ, D_MODEL)
    y_sample = x[1].reshape(DEC_BATCH, DEC_SEQ, D_MODEL)
    return (y_prompt, y_sample, jnp.stack(new_k, axis=1), jnp.stack(new_v, axis=1),
            jnp.stack(new_ckv, axis=1), jnp.stack(new_kpe, axis=1), jnp.stack(new_s5, axis=1))
```
